```python
import math
import jax, jax.numpy as jnp
from jax import lax
import numpy as np

D_MODEL = 1024
BATCH = 8
SEQ = 2048
DEPTH = 4
DEC_BATCH = 32
DEC_SEQ = 4
PAST_LEN = 8192
PAGE_SIZE = 128

N_EVEN = (DEPTH + 1) // 2
N_ODD = DEPTH // 2
A_WIDTH = D_MODEL // 2
A_GROUPS = 8
CHUNK = 128
B_WIDTH = D_MODEL // 2
CONV_W = 31
C_HEADS = 16
KV_GROUPS = 4
HEAD_DIM = D_MODEL // C_HEADS
HPG = C_HEADS // KV_GROUPS
CMP_BLOCK = 32
CMP_STRIDE = 16
CMP_HID = 2 * HEAD_DIM
SEL_BLOCK = 64
N_SEL = 16
WINDOW = 512
WIN_Q_BLOCK = 128
SEL_Q_BLOCK = 32
N_BUCKETS = 32
MAX_DIST = 128
SCALE = HEAD_DIM ** -0.5
D_FF = ((8 * D_MODEL // 3 + 127) // 128) * 128
EPS = 1e-6
NEG = -1e30
FORCE = 1e6

kernel_name = 'hybrid_gmlp_conformer_nsa_step'


def rmsnorm(x, g):
    xf = x.astype(jnp.float32)
    y = xf * lax.rsqrt(jnp.mean(xf * xf, axis=-1, keepdims=True) + EPS)
    return y.astype(x.dtype) * g


def layernorm(x, g, b):
    xf = x.astype(jnp.float32)
    mu = jnp.mean(xf, axis=-1, keepdims=True)
    var = jnp.mean(jnp.square(xf - mu), axis=-1, keepdims=True)
    return ((xf - mu) * lax.rsqrt(var + EPS)).astype(x.dtype) * g + b


def masked_softmax(logits, mask, axis):
    l = jnp.where(mask, logits.astype(jnp.float32), NEG)
    m = jnp.max(l, axis=axis, keepdims=True)
    e = jnp.where(mask, jnp.exp(l - m), 0.0)
    return e / jnp.maximum(jnp.sum(e, axis=axis, keepdims=True), 1e-30)


def t5_bucket(dist):
    exact = N_BUCKETS // 2
    d = jnp.maximum(dist, 0)
    df = jnp.maximum(d, 1).astype(jnp.float32)
    large = exact + (jnp.log(df / exact) / math.log(MAX_DIST / exact) * (N_BUCKETS - exact)).astype(jnp.int32)
    return jnp.where(d < exact, d, jnp.minimum(large, N_BUCKETS - 1))


def adaln_params(c, w, b):
    m = jax.nn.silu(c) @ w + b
    return jnp.split(m[:, None, :], 9, axis=-1)


def modulate(x, g, shift, scale):
    return rmsnorm(x, g) * (1 + scale) + shift


def ffn_half(x, g, shift, scale, gate, w1, w2):
    a, b = jnp.split(modulate(x, g, shift, scale) @ w1, 2, axis=-1)
    return x + 0.5 * gate * ((jax.nn.silu(a) * b) @ w2)


def chunk_spatial_gate(v, ws, bs):
    B, T, C = v.shape
    Tp = -(-T // CHUNK) * CHUNK
    vp = jnp.pad(v, ((0, 0), (0, Tp - T), (0, 0))).reshape(B, Tp // CHUNK, CHUNK, A_GROUPS, C // A_GROUPS)
    w = ws * jnp.tril(jnp.ones((CHUNK, CHUNK), ws.dtype))
    s = jnp.einsum('gts,bnsgd->bntgd', w, vp) + bs.T[None, None, :, :, None]
    return s.reshape(B, Tp, C)[:, :T]


def even_mix(h, conv_state, w_in, w_out, v_g, ws, bs, cw, cb, ln_g, ln_b):
    zu, zv, za, zg = jnp.split(h @ w_in, [A_WIDTH, 2 * A_WIDTH, 2 * A_WIDTH + B_WIDTH], axis=-1)
    u = jax.nn.gelu(zu)
    v = rmsnorm(jax.nn.gelu(zv), v_g)
    a_out = u * chunk_spatial_gate(v, ws, bs)
    glu = za * jax.nn.sigmoid(zg)
    ctx = jnp.concatenate([conv_state.astype(glu.dtype), glu], axis=1)
    conv = lax.conv_general_dilated(ctx, cw[:, None, :].astype(ctx.dtype), (1,), 'VALID',
                                    dimension_numbers=('NWC', 'WIO', 'NWC'),
                                    feature_group_count=B_WIDTH) + cb
    b_out = jax.nn.silu(layernorm(conv, ln_g, ln_b))
    out = jnp.concatenate([a_out, b_out], axis=-1) @ w_out
    return out, ctx[:, -(CONV_W - 1):], v


def compress(k_full, pe, w1, w2):
    B, Tk = k_full.shape[:2]
    n = (Tk - CMP_BLOCK) // CMP_STRIDE + 1
    ids = jnp.arange(n)[:, None] * CMP_STRIDE + jnp.arange(CMP_BLOCK)[None, :]
    blk = k_full[:, ids] + pe[None, None, :, None, :]
    flat = blk.transpose(0, 1, 3, 2, 4).reshape(B, n, KV_GROUPS, CMP_BLOCK * HEAD_DIM)
    return jax.nn.gelu(flat @ w1) @ w2


def cmp_attend(q, kc, vc, q_pos, table):
    B, Tq = q.shape[:2]
    n = kc.shape[1]
    qg = q.reshape(B, Tq, KV_GROUPS, HPG, HEAD_DIM)
    logits = jnp.einsum('btgpd,bngd->bgptn', qg, kc).astype(jnp.float32) * SCALE
    ends = jnp.arange(n) * CMP_STRIDE + CMP_BLOCK - 1
    dist = q_pos[:, None] - ends[None, :]
    bias = table[t5_bucket(dist)].reshape(Tq, n, KV_GROUPS, HPG).transpose(2, 3, 0, 1)
    p = masked_softmax(logits + bias.astype(jnp.float32), dist >= 0, axis=-1)
    o = jnp.einsum('bgptn,bngd->btgpd', p.astype(vc.dtype), vc).reshape(B, Tq, C_HEADS, HEAD_DIM)
    return o, p


def select_blocks(p, q_pos, n_cmp, nsb):
    cs = jnp.arange(n_cmp) * CMP_STRIDE
    ss = jnp.arange(nsb) * SEL_BLOCK
    ov = jnp.clip(jnp.minimum(cs[:, None] + CMP_BLOCK, ss[None, :] + SEL_BLOCK)
                  - jnp.maximum(cs[:, None], ss[None, :]), 0, None).astype(jnp.float32) / CMP_STRIDE
    imp = jnp.einsum('bgptn,nj->bgtj', p, ov)
    j = jnp.arange(nsb)[None, :]
    cur = (q_pos // SEL_BLOCK)[:, None]
    valid = j * SEL_BLOCK <= q_pos[:, None]
    forced = (j == 0) | (j == cur) | (j == cur - 1)
    imp = jnp.where(valid, jnp.where(forced, FORCE, imp), NEG)
    vals, idx = lax.top_k(imp, min(N_SEL, nsb))
    return idx, vals > 0.5 * NEG


def sel_attend(q, kb, vb, idx, valid, q_pos, table):
    B, Tq = q.shape[:2]
    bi = jnp.arange(B)[:, None, None, None]
    gi = jnp.arange(KV_GROUPS)[None, :, None, None]
    kg = kb[bi, gi, idx]
    vg = vb[bi, gi, idx]
    qg = q.reshape(B, Tq, KV_GROUPS, HPG, HEAD_DIM).transpose(0, 2, 1, 3, 4)
    logits = jnp.einsum('bgtpd,bgtksd->bgtpks', qg, kg).astype(jnp.float32) * SCALE
    kpos = idx[..., None] * SEL_BLOCK + jnp.arange(SEL_BLOCK)
    dist = q_pos[None, None, :, None, None] - kpos
    gi5 = jnp.arange(KV_GROUPS)[None, :, None, None, None]
    bias = jnp.moveaxis(table.reshape(N_BUCKETS, KV_GROUPS, HPG)[t5_bucket(dist), gi5], -1, 3)
    mask = ((dist >= 0) & valid[..., None])[:, :, :, None]
    p = masked_softmax(logits + bias.astype(jnp.float32), mask, axis=(-2, -1))
    o = jnp.einsum('bgtpks,bgtksd->bgtpd', p.astype(vg.dtype), vg)
    return o.transpose(0, 2, 1, 3, 4).reshape(B, Tq, C_HEADS, HEAD_DIM)


def selected_branch(q, kb, vb, idx, valid, q_pos, table):
    B, Tq = q.shape[:2]
    if Tq > SEL_Q_BLOCK and Tq % SEL_Q_BLOCK == 0:
        nb = Tq // SEL_Q_BLOCK
        qs = q.reshape(B, nb, SEL_Q_BLOCK, C_HEADS, HEAD_DIM).swapaxes(0, 1)
        ids = idx.reshape(B, KV_GROUPS, nb, SEL_Q_BLOCK, -1).transpose(2, 0, 1, 3, 4)
        vls = valid.reshape(B, KV_GROUPS, nb, SEL_Q_BLOCK, -1).transpose(2, 0, 1, 3, 4)
        ps = q_pos.reshape(nb, SEL_Q_BLOCK)
        o = lax.map(lambda a: sel_attend(a[0], kb, vb, a[1], a[2], a[3], table), (qs, ids, vls, ps))
        return o.swapaxes(0, 1).reshape(B, Tq, C_HEADS, HEAD_DIM)
    return sel_attend(q, kb, vb, idx, valid, q_pos, table)


def window_attend(q, k, v, q_pos, k_pos, table):
    B, N, Tq = q.shape[:3]
    Tk = k.shape[2]
    qg = q.reshape(B, N, Tq, KV_GROUPS, HPG, HEAD_DIM)
    logits = jnp.einsum('bntgpd,bnsgd->bngpts', qg, k).astype(jnp.float32) * SCALE
    dist = q_pos[:, :, None] - k_pos[:, None, :]
    mask = (dist >= 0) & (dist < WINDOW) & (k_pos[:, None, :] >= 0)
    bias = table[t5_bucket(dist)].reshape(N, Tq, Tk, KV_GROUPS, HPG).transpose(0, 3, 4, 1, 2)
    p = masked_softmax(logits + bias.astype(jnp.float32), mask[:, None, None], axis=-1)
    o = jnp.einsum('bngpts,bnsgd->bntgpd', p.astype(v.dtype), v)
    return o.reshape(B, N, Tq, C_HEADS, HEAD_DIM)


def nsa_mix(h, past_cmp, past_sel, win_buf, q_start, w_in, w_out, qn_g, kn_g, pe, w1, w2, table):
    B, T, _ = h.shape
    GH = KV_GROUPS * HEAD_DIM
    sizes = [C_HEADS * HEAD_DIM] + [GH] * 6 + [3 * C_HEADS]
    parts = jnp.split(h @ w_in, np.cumsum(sizes)[:-1].tolist(), axis=-1)
    q = rmsnorm(parts[0].reshape(B, T, C_HEADS, HEAD_DIM), qn_g)

    def kv_rows(a, b):
        return jnp.stack([a.reshape(B, T, KV_GROUPS, HEAD_DIM), b.reshape(B, T, KV_GROUPS, HEAD_DIM)], axis=2)

    cmp_rows = kv_rows(parts[1], parts[2])
    sel_rows = kv_rows(parts[3], parts[4])
    win_rows = kv_rows(parts[5], parts[6])
    gates = jax.nn.sigmoid(parts[7]).reshape(B, T, 3, C_HEADS, 1)
    q_pos = q_start + jnp.arange(T)
    cmp_full = cmp_rows if past_cmp is None else jnp.concatenate([past_cmp, cmp_rows], axis=1)
    sel_full = sel_rows if past_sel is None else jnp.concatenate([past_sel, sel_rows], axis=1)
    kc = rmsnorm(compress(cmp_full[:, :, 0], pe[0], w1[0], w2[0]), kn_g)
    vc = compress(cmp_full[:, :, 1], pe[1], w1[1], w2[1])
    o_cmp, p_cmp = cmp_attend(q, kc, vc, q_pos, table)
    Tk = sel_full.shape[1]
    nsb = -(-Tk // SEL_BLOCK)
    idx, valid = select_blocks(p_cmp, q_pos, kc.shape[1], nsb)

    def to_blocks(a):
        a = jnp.pad(a, ((0, 0), (0, nsb * SEL_BLOCK - Tk), (0, 0), (0, 0)))
        return a.reshape(B, nsb, SEL_BLOCK, KV_GROUPS, HEAD_DIM).transpose(0, 3, 1, 2, 4)

    kb = to_blocks(rmsnorm(sel_full[:, :, 0], kn_g))
    vb = to_blocks(sel_full[:, :, 1])
    o_sel = selected_branch(q, kb, vb, idx, valid, q_pos, table)
    if win_buf is None:
        nqb = T // WIN_Q_BLOCK
        pad = ((0, 0), (WINDOW, 0), (0, 0), (0, 0))
        ids = jnp.arange(nqb)[:, None] * WIN_Q_BLOCK + jnp.arange(WIN_Q_BLOCK + WINDOW)[None, :]
        kctx = jnp.pad(rmsnorm(win_rows[:, :, 0], kn_g), pad)[:, ids]
        vctx = jnp.pad(win_rows[:, :, 1], pad)[:, ids]
        o_win = window_attend(q.reshape(B, nqb, WIN_Q_BLOCK, C_HEADS, HEAD_DIM), kctx, vctx,
                              q_pos.reshape(nqb, WIN_Q_BLOCK), ids - WINDOW, table).reshape(B, T, C_HEADS, HEAD_DIM)
        ctx = win_rows
    else:
        ctx = jnp.concatenate([win_buf, win_rows], axis=1)
        n_ctx = ctx.shape[1]
        k_pos = (q_start + T - n_ctx + jnp.arange(n_ctx))[None]
        o_win = window_attend(q[:, None], rmsnorm(ctx[:, :, 0], kn_g)[:, None], ctx[:, :, 1][:, None],
                              q_pos[None], k_pos, table)[:, 0]
    new_win = ctx[:, -min(WINDOW, ctx.shape[1]):]
    o = gates[:, :, 0] * o_cmp + gates[:, :, 1] * o_sel + gates[:, :, 2] * o_win
    return o.reshape(B, T, C_HEADS * HEAD_DIM) @ w_out, cmp_rows, sel_rows, new_win


def setup_inputs(seed: int = 0) -> dict:
    key = jax.random.key(seed)
    ks = jax.random.split(key, 32)

    def nrm(k, shape, s=1.0):
        return s * jax.random.normal(k, shape, jnp.float32)

    n_pages = PAST_LEN // PAGE_SIZE
    n_used = DEC_BATCH * n_pages
    n_pool = n_used + (n_used + 3) // 4
    page_table = jax.random.permutation(ks[0], n_pool)[:n_used].reshape(DEC_BATCH, n_pages).astype(jnp.int32)
    w_buf = min(WINDOW, PAST_LEN)
    GH = KV_GROUPS * HEAD_DIM
    nsa_in = C_HEADS * HEAD_DIM + 6 * GH + 3 * C_HEADS
    return {
        'x_prompt': nrm(ks[1], (BATCH, SEQ, D_MODEL)),
        'x_sample': nrm(ks[2], (DEC_BATCH, DEC_SEQ, D_MODEL)),
        'c_prompt': nrm(ks[3], (BATCH, D_MODEL)),
        'c_sample': nrm(ks[4], (DEC_BATCH, D_MODEL)),
        'cache_cmp_kv': nrm(ks[5], (n_pool, PAGE_SIZE, N_ODD, 2, KV_GROUPS, HEAD_DIM)),
        'cache_sel_kv': nrm(ks[6], (n_pool, PAGE_SIZE, N_ODD, 2, KV_GROUPS, HEAD_DIM)),
        'page_table': page_table,
        'state_win_kv': nrm(ks[7], (N_ODD, DEC_BATCH, w_buf, 2, KV_GROUPS, HEAD_DIM)),
        'state_conv': nrm(ks[8], (N_EVEN, DEC_BATCH, CONV_W - 1, B_WIDTH), 0.5),
        'ada_w': nrm(ks[9], (DEPTH, D_MODEL, 9 * D_MODEL), 0.5 * D_MODEL ** -0.5),
        'ada_b': nrm(ks[10], (DEPTH, 9 * D_MODEL), 0.02),
        'norm_g': 1.0 + nrm(ks[11], (DEPTH, 3, D_MODEL), 0.05),
        'ffn_w1': nrm(ks[12], (DEPTH, 2, D_MODEL, 2 * D_FF), D_MODEL ** -0.5),
        'ffn_w2': nrm(ks[13], (DEPTH, 2, D_FF, D_MODEL), D_FF ** -0.5),
        'even_w_in': nrm(ks[14], (N_EVEN, D_MODEL, 2 * A_WIDTH + 2 * B_WIDTH), D_MODEL ** -0.5),
        'even_w_out': nrm(ks[15], (N_EVEN, A_WIDTH + B_WIDTH, D_MODEL), (A_WIDTH + B_WIDTH) ** -0.5),
        'gmlp_v_g': 1.0 + nrm(ks[16], (N_EVEN, A_WIDTH), 0.05),
        'gmlp_ws': nrm(ks[17], (N_EVEN, A_GROUPS, CHUNK, CHUNK), CHUNK ** -0.5),
        'gmlp_bs': 1.0 + nrm(ks[18], (N_EVEN, A_GROUPS, CHUNK), 0.05),
        'conv_w': nrm(ks[19], (N_EVEN, CONV_W, B_WIDTH), CONV_W ** -0.5),
        'conv_b': nrm(ks[20], (N_EVEN, B_WIDTH), 0.02),
        'conv_ln_g': 1.0 + nrm(ks[21], (N_EVEN, B_WIDTH), 0.05),
        'conv_ln_b': nrm(ks[22], (N_EVEN, B_WIDTH), 0.02),
        'nsa_w_in': nrm(ks[23], (N_ODD, D_MODEL, nsa_in), D_MODEL ** -0.5),
        'nsa_w_out': nrm(ks[24], (N_ODD, C_HEADS * HEAD_DIM, D_MODEL), (C_HEADS * HEAD_DIM) ** -0.5),
        'q_norm_g': 1.0 + nrm(ks[25], (N_ODD, HEAD_DIM), 0.05),
        'k_norm_g': 1.0 + nrm(ks[26], (N_ODD, HEAD_DIM), 0.05),
        'cmp_pe': nrm(ks[27], (N_ODD, 2, CMP_BLOCK, HEAD_DIM), 0.1),
        'cmp_w1': nrm(ks[28], (N_ODD, 2, CMP_BLOCK * HEAD_DIM, CMP_HID), (CMP_BLOCK * HEAD_DIM) ** -0.5),
        'cmp_w2': nrm(ks[29], (N_ODD, 2, CMP_HID, HEAD_DIM), CMP_HID ** -0.5),
        'rel_bias': nrm(ks[30], (N_BUCKETS, C_HEADS), 0.2),
    }


def reference(x_prompt, x_sample, c_prompt, c_sample, cache_cmp_kv, cache_sel_kv, page_table, state_win_kv,
              state_conv, ada_w, ada_b, norm_g, ffn_w1, ffn_w2, even_w_in, even_w_out, gmlp_v_g, gmlp_ws,
              gmlp_bs, conv_w, conv_b, conv_ln_g, conv_ln_b, nsa_w_in, nsa_w_out, q_norm_g, k_norm_g, cmp_pe,
              cmp_w1, cmp_w2, rel_bias):
    xp, xs = x_prompt, x_sample
    n_seq = page_table.shape[0]
    cmp_p, cmp_s, sel_p, sel_s, win_p, win_s = [], [], [], [], [], []
    conv_p, conv_s, v_s = [], [], []
    for l in range(DEPTH):
        mp = adaln_params(c_prompt, ada_w[l], ada_b[l])
        ms = adaln_params(c_sample, ada_w[l], ada_b[l])
        xp = ffn_half(xp, norm_g[l, 0], mp[0], mp[1], mp[2], ffn_w1[l, 0], ffn_w2[l, 0])
        xs = ffn_half(xs, norm_g[l, 0], ms[0], ms[1], ms[2], ffn_w1[l, 0], ffn_w2[l, 0])
        hp = modulate(xp, norm_g[l, 1], mp[3], mp[4])
        hs = modulate(xs, norm_g[l, 1], ms[3], ms[4])
        if l % 2 == 0:
            e = l // 2
            prm = (even_w_in[e], even_w_out[e], gmlp_v_g[e], gmlp_ws[e], gmlp_bs[e], conv_w[e], conv_b[e],
                   conv_ln_g[e], conv_ln_b[e])
            zero_hist = jnp.zeros((xp.shape[0], CONV_W - 1, B_WIDTH), xp.dtype)
            op, cst_p, _ = even_mix(hp, zero_hist, *prm)
            os_, cst_s, vrow = even_mix(hs, state_conv[e], *prm)
            conv_p.append(cst_p)
            conv_s.append(cst_s)
            v_s.append(vrow)
        else:
            o = l // 2
            prm = (nsa_w_in[o], nsa_w_out[o], q_norm_g[o], k_norm_g[o], cmp_pe[o], cmp_w1[o], cmp_w2[o], rel_bias)
            past_cmp = cache_cmp_kv[page_table, :, o].reshape(n_seq, -1, 2, KV_GROUPS, HEAD_DIM)
            past_sel = cache_sel_kv[page_table, :, o].reshape(n_seq, -1, 2, KV_GROUPS, HEAD_DIM)
            op, rc_p, rs_p, w_p = nsa_mix(hp, None, None, None, 0, *prm)
            os_, rc_s, rs_s, w_s = nsa_mix(hs, past_cmp, past_sel, state_win_kv[o], past_cmp.shape[1], *prm)
            cmp_p.append(rc_p)
            cmp_s.append(rc_s)
            sel_p.append(rs_p)
            sel_s.append(rs_s)
            win_p.append(w_p)
            win_s.append(w_s)
        xp = xp + mp[5] * op
        xs = xs + ms[5] * os_
        xp = ffn_half(xp, norm_g[l, 2], mp[6], mp[7], mp[8], ffn_w1[l, 1], ffn_w2[l, 1])
        xs = ffn_half(xs, norm_g[l, 2], ms[6], ms[7], ms[8], ffn_w1[l, 1], ffn_w2[l, 1])
    new_cmp_kv_prompt = jnp.stack(cmp_p, axis=2)
    new_cmp_kv_sample = jnp.stack(cmp_s, axis=2)
    new_sel_kv_prompt = jnp.stack(sel_p, axis=2)
    new_sel_kv_sample = jnp.stack(sel_s, axis=2)
    new_win_kv_prompt = jnp.stack(win_p, axis=0)
    new_win_kv_sample = jnp.stack(win_s, axis=0)
    new_conv_prompt = jnp.stack(conv_p, axis=0)
    new_conv_sample = jnp.stack(conv_s, axis=0)
    new_gmlp_v_sample = jnp.stack(v_s, axis=0)
    return (xp, xs, new_cmp_kv_prompt, new_cmp_kv_sample, new_sel_kv_prompt, new_sel_kv_sample,
            new_win_kv_prompt, new_win_kv_sample, new_conv_prompt, new_conv_sample, new_gmlp_v_sample)
```

```python
import functools
import math

import numpy as np
import jax
import jax.numpy as jnp
from jax import lax
from jax.experimental import pallas as pl
from jax.experimental.pallas import tpu as pltpu

F32 = jnp.float32
BF16 = jnp.bfloat16

A_GROUPS = 8
CHUNK = 128
CONV_W = 31
C_HEADS = 16
KV_GROUPS = 4
HEAD_DIM = 64
HPG = C_HEADS // KV_GROUPS
GH = KV_GROUPS * HEAD_DIM
CMP_BLOCK = 32
CMP_STRIDE = 16
SEL_BLOCK = 64
N_SEL = 16
WINDOW = 512
N_BUCKETS = 32
MAX_DIST = 128
SCALE = HEAD_DIM ** -0.5
EPS = 1e-6
NEG = -1e30
FORCE = 1e6

VMEM_LIMIT_BYTES = 60 * 2 ** 20
LANES = 128
TQ = 128
STACK = 4
HIST = 32


def _params(*sem):
    return pltpu.CompilerParams(dimension_semantics=sem, vmem_limit_bytes=VMEM_LIMIT_BYTES)


def _dot(a, b):
    return jnp.dot(a, b, preferred_element_type=F32)


def _dot_nt(a, b):
    return lax.dot_general(a, b, (((1,), (1,)), ((), ())), preferred_element_type=F32)


def _dot_tn(a, b):
    return lax.dot_general(a, b, (((0,), (0,)), ((), ())), preferred_element_type=F32)


def _split_bf16(x):
    hi = x.astype(BF16)
    lo = (x - hi.astype(F32)).astype(BF16)
    return hi, lo


def _sigmoid(x):
    return 1.0 / (1.0 + jnp.exp(-x))


def _silu(x):
    return x * _sigmoid(x)


def _gelu(x):
    return 0.5 * x * (1.0 + jnp.tanh(math.sqrt(2.0 / math.pi) * (x + 0.044715 * (x * x * x))))


def _modulate(x, g, shift, scale):
    y = x * lax.rsqrt(jnp.mean(x * x, axis=-1, keepdims=True) + EPS)
    return (y * g) * (1.0 + scale) + shift


def _bucket_np(dist):
    exact = N_BUCKETS // 2
    d = np.maximum(dist, 0)
    df = np.maximum(d, 1).astype(np.float32)
    large = exact + (np.log(df / np.float32(exact)) / np.float32(math.log(MAX_DIST / exact))
                     * np.float32(N_BUCKETS - exact)).astype(np.int32)
    return np.where(d < exact, d, np.minimum(large, N_BUCKETS - 1)).astype(np.int32)


def _full_spec(shape):
    n = len(shape)
    return pl.BlockSpec(shape, lambda *_: (0,) * n)


def _tok_spec(tm, width):
    return pl.BlockSpec((1, tm, width), lambda b, i: (b, i, 0))


def _mod_spec(mod, tm):
    if mod.shape[1] == 1:
        return pl.BlockSpec((1, 1, mod.shape[2]), lambda b, i: (b, 0, 0))
    return pl.BlockSpec((1, tm, mod.shape[2]), lambda b, i: (b, i, 0))


def _ada_kernel(c_ref, w_ref, b_ref, o_ref):
    c = c_ref[...]
    o_ref[0] = _dot(_silu(c).astype(BF16), w_ref[0].astype(BF16)) + b_ref[0]


def _ada_call(c_all, ada_w, ada_b):
    L, D, N = ada_w.shape
    M = c_all.shape[0]
    tn = 1024
    return pl.pallas_call(
        _ada_kernel,
        grid=(L, N // tn),
        in_specs=[pl.BlockSpec((M, D), lambda l, j: (0, 0)),
                  pl.BlockSpec((1, D, tn), lambda l, j: (l, 0, j)),
                  pl.BlockSpec((1, 1, tn), lambda l, j: (l, 0, j))],
        out_specs=pl.BlockSpec((1, M, tn), lambda l, j: (l, 0, j)),
        out_shape=jax.ShapeDtypeStruct((L, M, N), F32),
        compiler_params=_params("arbitrary", "arbitrary"),
        name="ada",
    )(c_all, ada_w, ada_b.reshape(L, 1, N))


def _ffn_kernel(x_ref, sh_ref, sc_ref, gt_ref, g_ref, w1_ref, w2_ref, o_ref, acc_ref, *, ff, tf):
    x = x_ref[0]
    h = _modulate(x, g_ref[...], sh_ref[0], sc_ref[0]).astype(BF16)
    for c in range(ff // tf):
        a = _dot(h, w1_ref[:, c * tf:(c + 1) * tf])
        b = _dot(h, w1_ref[:, ff + c * tf:ff + (c + 1) * tf])
        t = (_silu(a) * b).astype(BF16)
        part = _dot(t, w2_ref[c * tf:(c + 1) * tf, :])
        if c == 0:
            acc_ref[...] = part
        else:
            acc_ref[...] += part
    o_ref[0] = x + 0.5 * gt_ref[0] * acc_ref[...]


def _ffn_call(x3, sh, sc, gt, g, w1, w2, tm):
    NB, R, D = x3.shape
    ff = w2.shape[0]
    tf = 256
    return pl.pallas_call(
        functools.partial(_ffn_kernel, ff=ff, tf=tf),
        grid=(NB, R // tm),
        in_specs=[_tok_spec(tm, D), _mod_spec(sh, tm), _mod_spec(sc, tm), _mod_spec(gt, tm),
                  _full_spec((1, D)), _full_spec(w1.shape), _full_spec(w2.shape)],
        out_specs=_tok_spec(tm, D),
        out_shape=jax.ShapeDtypeStruct(x3.shape, F32),
        scratch_shapes=[pltpu.VMEM((tm, D), F32)],
        compiler_params=_params("arbitrary", "arbitrary"),
        name="ffn",
    )(x3, sh, sc, gt, g.reshape(1, D), w1, w2)


def _even_kernel(x_ref, sh_ref, sc_ref, gt_ref, g_ref, win_ref, wout_ref, vg_ref, ws_ref, bs_ref,
                 cw_ref, cb_ref, lg_ref, lb_ref, o_ref, cs_ref, ext_ref, sa_ref, *, tm, aw):
    @pl.when(pl.program_id(1) == 0)
    def _():
        ext_ref[0:HIST, :] = jnp.zeros((HIST, ext_ref.shape[1]), F32)

    x = x_ref[0]
    h = _modulate(x, g_ref[...], sh_ref[0], sc_ref[0]).astype(BF16)
    z = _dot(h, win_ref[...])
    u = _gelu(z[:, 0:aw])
    gv = _gelu(z[:, aw:2 * aw])
    v = gv * lax.rsqrt(jnp.mean(gv * gv, axis=-1, keepdims=True) + EPS) * vg_ref[...]
    vb = v.astype(BF16)
    row = lax.broadcasted_iota(jnp.int32, (CHUNK, CHUNK), 0)
    col = lax.broadcasted_iota(jnp.int32, (CHUNK, CHUNK), 1)
    wm = [jnp.where(row >= col, ws_ref[gi], 0.0).astype(BF16) for gi in range(A_GROUPS)]
    gw = aw // A_GROUPS
    first_half = col < gw
    for c in range(tm // CHUNK):
        for q in range(aw // LANES):
            vq = vb[c * CHUNK:(c + 1) * CHUNK, q * LANES:(q + 1) * LANES]
            s0 = _dot(wm[2 * q], vq)
            s1 = _dot(wm[2 * q + 1], vq)
            sa_ref[c * CHUNK:(c + 1) * CHUNK, q * LANES:(q + 1) * LANES] = (
                jnp.where(first_half, s0, s1) + bs_ref[:, q * LANES:(q + 1) * LANES])
    a_out = u * sa_ref[...]
    glu = z[:, 2 * aw:3 * aw] * _sigmoid(z[:, 3 * aw:4 * aw])
    ext_ref[HIST:HIST + tm, :] = glu
    off = HIST - (CONV_W - 1)
    conv = cb_ref[...] + ext_ref[pl.ds(off, tm), :] * cw_ref[0:1, :]
    for k in range(1, CONV_W):
        conv = conv + ext_ref[pl.ds(off + k, tm), :] * cw_ref[k:k + 1, :]
    cs_ref[0] = ext_ref[pl.ds(tm + off, CONV_W - 1), :]
    ext_ref[0:HIST, :] = ext_ref[tm:tm + HIST, :]
    mu = jnp.mean(conv, axis=-1, keepdims=True)
    cc = conv - mu
    var = jnp.mean(cc * cc, axis=-1, keepdims=True)
    b_out = _silu(cc * lax.rsqrt(var + EPS) * lg_ref[...] + lb_ref[...])
    out = _dot(a_out.astype(BF16), wout_ref[0:aw, :]) + _dot(b_out.astype(BF16), wout_ref[aw:, :])
    o_ref[0] = x + gt_ref[0] * out


def _even_call(x3, sh, sc, gt, g, w_in, w_out, v_g, ws, bs, cw, cb, ln_g, ln_b, tm):
    B, T, D = x3.shape
    aw = v_g.shape[0]
    bw = cw.shape[1]
    assert aw == bw and aw // A_GROUPS * 2 == LANES and T % tm == 0 and tm % CHUNK == 0
    bs_exp = jnp.repeat(bs.T, aw // A_GROUPS, axis=1)
    cw_pad = jnp.pad(cw, ((0, 1), (0, 0)))
    row = lambda a: a.reshape(1, -1)
    return pl.pallas_call(
        functools.partial(_even_kernel, tm=tm, aw=aw),
        grid=(B, T // tm),
        in_specs=[_tok_spec(tm, D), _mod_spec(sh, tm), _mod_spec(sc, tm), _mod_spec(gt, tm),
                  _full_spec((1, D)), _full_spec(w_in.shape), _full_spec(w_out.shape),
                  _full_spec((1, aw)), _full_spec(ws.shape), _full_spec(bs_exp.shape),
                  _full_spec(cw_pad.shape), _full_spec((1, bw)), _full_spec((1, bw)), _full_spec((1, bw))],
        out_specs=[_tok_spec(tm, D), pl.BlockSpec((1, CONV_W - 1, bw), lambda b, i: (b, 0, 0))],
        out_shape=[jax.ShapeDtypeStruct(x3.shape, F32), jax.ShapeDtypeStruct((B, CONV_W - 1, bw), F32)],
        scratch_shapes=[pltpu.VMEM((tm + HIST, bw), F32), pltpu.VMEM((tm, aw), F32)],
        compiler_params=_params("arbitrary", "arbitrary"),
        name="even_prompt",
    )(x3, sh, sc, gt, row(g), w_in, w_out, row(v_g), ws, bs_exp, cw_pad, row(cb), row(ln_g), row(ln_b))


def _even_s_kernel(x_ref, sh_ref, sc_ref, gt_ref, g_ref, win_ref, wout_ref, vg_ref, coef_ref, bsa_ref,
                   st_ref, cw_ref, cb_ref, lg_ref, lb_ref, o_ref, cs_ref, v_ref, *, ts, bsz, aw):
    x = x_ref[0]
    h = _modulate(x, g_ref[...], sh_ref[0], sc_ref[0]).astype(BF16)
    z = _dot(h, win_ref[...])
    u = _gelu(z[:, 0:aw])
    gv = _gelu(z[:, aw:2 * aw])
    v = gv * lax.rsqrt(jnp.mean(gv * gv, axis=-1, keepdims=True) + EPS) * vg_ref[...]
    glu = z[:, 2 * aw:3 * aw] * _sigmoid(z[:, 3 * aw:4 * aw])
    hist = CONV_W - 1
    sl = lambda a, t: a[t * bsz:(t + 1) * bsz]
    a_parts, b_parts = [], []
    for t in range(ts):
        v_ref[t] = sl(v, t)
        s = bsa_ref[t]
        for j in range(t + 1):
            s = s + coef_ref[t, j] * sl(v, j)
        a_parts.append(sl(u, t) * s)
        conv = cb_ref[...]
        for m in range(t, hist):
            conv = conv + st_ref[m] * cw_ref[m - t:m - t + 1, :]
        for j in range(t + 1):
            conv = conv + sl(glu, j) * cw_ref[hist - t + j:hist - t + j + 1, :]
        mu = jnp.mean(conv, axis=-1, keepdims=True)
        cc = conv - mu
        var = jnp.mean(cc * cc, axis=-1, keepdims=True)
        b_parts.append(_silu(cc * lax.rsqrt(var + EPS) * lg_ref[...] + lb_ref[...]))
    for i in range(hist):
        cs_ref[i] = st_ref[i + ts] if i + ts < hist else sl(glu, i + ts - hist)
    a_out = jnp.concatenate(a_parts, axis=0).astype(BF16)
    b_out = jnp.concatenate(b_parts, axis=0).astype(BF16)
    out = _dot(a_out, wout_ref[0:aw, :]) + _dot(b_out, wout_ref[aw:, :])
    o_ref[0] = x + gt_ref[0] * out


def _even_s_call(x3, sh, sc, gt, g, w_in, w_out, v_g, ws, bs, state, cw, cb, ln_g, ln_b, ts, bsz):
    _, R, D = x3.shape
    aw = v_g.shape[0]
    bw = cw.shape[1]
    gw = aw // A_GROUPS
    hist = CONV_W - 1
    assert ts <= CHUNK and ts <= hist
    coef = jnp.repeat(jnp.transpose(ws[:, :ts, :ts], (1, 2, 0)), gw, axis=2).reshape(ts, ts, 1, aw)
    bsa = jnp.repeat(bs[:, :ts].T, gw, axis=1).reshape(ts, 1, aw)
    st = jnp.transpose(state, (1, 0, 2))
    row = lambda a: a.reshape(1, -1)
    args = (x3, sh, sc, gt, row(g), w_in, w_out, row(v_g), coef, bsa, st, cw, row(cb), row(ln_g), row(ln_b))
    x_new, cs, v = pl.pallas_call(
        functools.partial(_even_s_kernel, ts=ts, bsz=bsz, aw=aw),
        grid=(1,),
        in_specs=[_full_spec(a.shape) for a in args],
        out_specs=[_full_spec(x3.shape), _full_spec((hist, bsz, bw)), _full_spec((ts, bsz, aw))],
        out_shape=[jax.ShapeDtypeStruct(x3.shape, F32), jax.ShapeDtypeStruct((hist, bsz, bw), F32),
                   jax.ShapeDtypeStruct((ts, bsz, aw), F32)],
        compiler_params=_params("arbitrary"),
        name="even_sample",
    )(*args)
    return x_new, jnp.transpose(cs, (1, 0, 2)), jnp.transpose(v, (1, 0, 2))


def _nsa_proj_kernel(x_ref, sh_ref, sc_ref, g_ref, w_ref, q_ref, cmp_ref, sel_ref, win_ref, gate_ref):
    x = x_ref[0]
    h = _modulate(x, g_ref[...], sh_ref[0], sc_ref[0]).astype(BF16)
    z = _dot(h, w_ref[...])
    nq = C_HEADS * HEAD_DIM
    q_ref[0] = z[:, 0:nq]
    cmp_ref[0] = z[:, nq:nq + 2 * GH]
    sel_ref[0] = z[:, nq + 2 * GH:nq + 4 * GH]
    win_ref[0] = z[:, nq + 4 * GH:nq + 6 * GH]
    gate_ref[0] = _sigmoid(z[:, nq + 6 * GH:nq + 6 * GH + LANES])


def _nsa_proj_call(x3, sh, sc, g, w_in_pad, tm):
    NB, R, D = x3.shape
    nq = C_HEADS * HEAD_DIM
    widths = (nq, 2 * GH, 2 * GH, 2 * GH, LANES)
    return pl.pallas_call(
        _nsa_proj_kernel,
        grid=(NB, R // tm),
        in_specs=[_tok_spec(tm, D), _mod_spec(sh, tm), _mod_spec(sc, tm), _full_spec((1, D)),
                  _full_spec(w_in_pad.shape)],
        out_specs=[_tok_spec(tm, w) for w in widths],
        out_shape=[jax.ShapeDtypeStruct((NB, R, w), F32) for w in widths],
        compiler_params=_params("arbitrary", "arbitrary"),
        name="nsa_proj",
    )(x3, sh, sc, g.reshape(1, D), w_in_pad)


def _headnorm_kernel(x_ref, g_ref, o_ref, *, width):
    lane = lax.broadcasted_iota(jnp.int32, (1, LANES), 1)
    lo = lane < HEAD_DIM
    for c in range(width // LANES):
        x = x_ref[:, c * LANES:(c + 1) * LANES]
        sq = x * x
        s_lo = jnp.sum(jnp.where(lo, sq, 0.0), axis=-1, keepdims=True)
        s_hi = jnp.sum(jnp.where(lo, 0.0, sq), axis=-1, keepdims=True)
        ss = jnp.where(lo, s_lo, s_hi)
        o_ref[:, c * LANES:(c + 1) * LANES] = (
            x * lax.rsqrt(ss * (1.0 / HEAD_DIM) + EPS) * g_ref[...]).astype(o_ref.dtype)


def _headnorm_call(x2, g, width, tr):
    N = x2.shape[0]
    g2 = jnp.tile(g, LANES // HEAD_DIM).reshape(1, LANES)
    return pl.pallas_call(
        functools.partial(_headnorm_kernel, width=width),
        grid=(N // tr,),
        in_specs=[pl.BlockSpec((tr, width), lambda i: (i, 0)), _full_spec((1, LANES))],
        out_specs=pl.BlockSpec((tr, width), lambda i: (i, 0)),
        out_shape=jax.ShapeDtypeStruct((N, width), BF16),
        compiler_params=_params("arbitrary"),
        name="headnorm",
    )(x2, g2)


def _compress_core(load_rows, n, pe_ref, w1_ref, w2_ref, kng_ref, o_ref, is_k):
    hid = w2_ref.shape[1]
    nstk = CMP_STRIDE // STACK
    accs = [None] * KV_GROUPS
    ctop = jnp.zeros((1, hid), F32)
    cbot = jnp.zeros((1, hid), F32)
    pe = pe_ref[0].astype(BF16)
    gpl = LANES // HEAD_DIM
    for l4 in range(nstk):
        xs = [[load_rows(l4 * STACK + i, s) for s in range(KV_GROUPS // gpl)] for i in range(STACK)]
        w = w1_ref[0, l4]
        r = _dot(pe, w)
        ctop = ctop + r[l4:l4 + 1, 0:hid]
        cbot = cbot + r[nstk + l4:nstk + l4 + 1, hid:2 * hid]
        for gi in range(KV_GROUPS):
            lo = (gi % gpl) * HEAD_DIM
            xcat = jnp.concatenate([x[gi // gpl][:, lo:lo + HEAD_DIM] for x in xs], axis=1).astype(BF16)
            part = _dot(xcat, w)
            accs[gi] = part if accs[gi] is None else accs[gi] + part
    for gi in range(KV_GROUPS):
        a = accs[gi][:, 0:hid] + ctop
        b = pltpu.roll(accs[gi][:, hid:2 * hid] + cbot, n - 1, 0)
        y = _dot(_gelu(a + b).astype(BF16), w2_ref[0])
        yn = y * lax.rsqrt(jnp.mean(y * y, axis=-1, keepdims=True) + EPS) * kng_ref[...]
        o_ref[0, 0, :, gi * HEAD_DIM:(gi + 1) * HEAD_DIM] = jnp.where(is_k, yn, y).astype(o_ref.dtype)


def _compress_p_kernel(x_ref, pe_ref, w1_ref, w2_ref, kng_ref, o_ref, *, n):
    kv = pl.program_id(1)
    slabs = 2 * GH // LANES
    load = lambda l, s: x_ref[0, pl.ds(l * slabs + kv * (slabs // 2) + s, n, stride=CMP_STRIDE * slabs), :]
    _compress_core(load, n, pe_ref, w1_ref, w2_ref, kng_ref, o_ref, kv == 0)


def _compress_weights(pe, w1, w2):
    hid = w1.shape[-1]
    w1r = w1.reshape(2, CMP_BLOCK, HEAD_DIM, hid)
    pair = jnp.concatenate([w1r[:, :CMP_STRIDE], w1r[:, CMP_STRIDE:]], axis=-1)
    w1c = pair.reshape(2, CMP_STRIDE // STACK, STACK * HEAD_DIM, 2 * hid).astype(BF16)
    return pe.reshape(2, 2 * CMP_STRIDE // STACK, STACK * HEAD_DIM), w1c, w2.astype(BF16)


def _compress_p_call(cmp_rows, pe, w1, w2, kn_g):
    B, T, _ = cmp_rows.shape
    n = T // CMP_STRIDE
    pe, w1r, w2r = _compress_weights(pe, w1, w2)
    hid = w1.shape[-1]
    slabs = 2 * GH // LANES
    cmp_rows = cmp_rows.reshape(B, T * slabs, LANES)
    return pl.pallas_call(
        functools.partial(_compress_p_kernel, n=n),
        grid=(B, 2),
        in_specs=[pl.BlockSpec((1, T * slabs, LANES), lambda b, kv: (b, 0, 0)),
                  pl.BlockSpec((1,) + pe.shape[1:], lambda b, kv: (kv, 0, 0)),
                  pl.BlockSpec((1,) + w1r.shape[1:], lambda b, kv: (kv, 0, 0, 0)),
                  pl.BlockSpec((1, hid, HEAD_DIM), lambda b, kv: (kv, 0, 0)),
                  _full_spec((1, HEAD_DIM))],
        out_specs=pl.BlockSpec((1, 1, n, GH), lambda b, kv: (b, kv, 0, 0)),
        out_shape=jax.ShapeDtypeStruct((B, 2, n, GH), BF16),
        compiler_params=_params("arbitrary", "arbitrary"),
        name="compress_prompt",
    )(cmp_rows, pe, w1r, w2r, kn_g.reshape(1, HEAD_DIM))


def _page_copy(cache_ref, buf_ref, sem_ref, page, slot, p, col, width, page_size):
    return pltpu.make_async_copy(
        cache_ref.at[page, :, pl.ds(col, width)],
        buf_ref.at[slot, pl.ds(p * page_size, page_size), :],
        sem_ref.at[slot])


def _compress_s_kernel(pt_ref, cache_ref, pe_ref, w1_ref, w2_ref, kng_ref, o_ref, buf_ref, sem_ref,
                       *, n, n_pages, page_size, slab0):
    b = pl.program_id(0)
    kv = pl.program_id(1)
    step = b * 2 + kv
    n_steps = pl.num_programs(0) * 2
    nsl = GH // LANES

    def fetch(s, start):
        sb = s // 2
        slot = s % 2
        for p in range(n_pages):
            for sl in range(nsl):
                cp = pltpu.make_async_copy(
                    cache_ref.at[pt_ref[sb, p], :, slab0 + (s % 2) * nsl + sl, :],
                    buf_ref.at[slot, sl, pl.ds(p * page_size, page_size), :],
                    sem_ref.at[slot])
                cp.start() if start else cp.wait()

    @pl.when(step == 0)
    def _():
        fetch(step, True)

    @pl.when(step + 1 < n_steps)
    def _():
        fetch(step + 1, True)

    fetch(step, False)
    slot = step % 2
    load = lambda l, s: buf_ref[slot, s, pl.ds(l, n, stride=CMP_STRIDE), :]
    _compress_core(load, n, pe_ref, w1_ref, w2_ref, kng_ref, o_ref, kv == 0)


def _compress_s_call(cache3, page_table, layer, pe, w1, w2, kn_g):
    Bs, n_pages = page_table.shape
    page_size = cache3.shape[1]
    past = n_pages * page_size
    n = past // CMP_STRIDE
    pe, w1r, w2r = _compress_weights(pe, w1, w2)
    hid = w1.shape[-1]
    grid_spec = pltpu.PrefetchScalarGridSpec(
        num_scalar_prefetch=1,
        grid=(Bs, 2),
        in_specs=[pl.BlockSpec(memory_space=pl.ANY),
                  pl.BlockSpec((1,) + pe.shape[1:], lambda b, kv, pt: (kv, 0, 0)),
                  pl.BlockSpec((1,) + w1r.shape[1:], lambda b, kv, pt: (kv, 0, 0, 0)),
                  pl.BlockSpec((1, hid, HEAD_DIM), lambda b, kv, pt: (kv, 0, 0)),
                  pl.BlockSpec((1, HEAD_DIM), lambda b, kv, pt: (0, 0))],
        out_specs=pl.BlockSpec((1, 1, n, GH), lambda b, kv, pt: (b, kv, 0, 0)),
        scratch_shapes=[pltpu.VMEM((2, GH // LANES, past, LANES), F32), pltpu.SemaphoreType.DMA((2,))],
    )
    cache4 = cache3.reshape(cache3.shape[0], page_size, -1, LANES)
    return pl.pallas_call(
        functools.partial(_compress_s_kernel, n=n, n_pages=n_pages, page_size=page_size,
                          slab0=layer * 2 * GH // LANES),
        grid_spec=grid_spec,
        out_shape=jax.ShapeDtypeStruct((Bs, 2, n, GH), BF16),
        compiler_params=_params("arbitrary", "arbitrary"),
        name="compress_sample",
    )(page_table, cache4, pe, w1r, w2r, kn_g.reshape(1, HEAD_DIM))


def _select(imp, tpos, nsb):
    j = lax.broadcasted_iota(jnp.int32, imp.shape, 1)
    cur = tpos // SEL_BLOCK
    valid = (j * SEL_BLOCK <= tpos) & (j < nsb)
    forced = (j == 0) | (j == cur) | (j == cur - 1)
    impf = jnp.where(valid, jnp.where(forced, FORCE, imp), NEG)
    rank = jnp.zeros(imp.shape, jnp.int32)
    for jp in range(nsb):
        c = impf[:, jp:jp + 1]
        beats = (c > impf) | ((c == impf) & (jp < j))
        rank = rank + beats.astype(jnp.int32)
    return (rank < min(N_SEL, nsb)) & valid


def _overlap_np(n_cmp, nsb, rows, cols):
    cs = np.arange(n_cmp)[:, None] * CMP_STRIDE
    ss = np.arange(nsb)[None, :] * SEL_BLOCK
    ov = np.clip(np.minimum(cs + CMP_BLOCK, ss + SEL_BLOCK) - np.maximum(cs, ss), 0, None).astype(np.float32) / CMP_STRIDE
    out = np.zeros((rows, cols), np.float32)
    out[:n_cmp, :nsb] = ov
    return out


def _cmp_p_kernel(q_ref, kv_ref, bias_ref, ov_ref, o_ref, sel_ref, *, n_cmp, nsb):
    qb = pl.program_id(1)
    q = q_ref[0]
    npad = kv_ref.shape[2]
    t = qb * TQ + lax.broadcasted_iota(jnp.int32, (TQ, npad), 0)
    nidx = lax.broadcasted_iota(jnp.int32, (TQ, npad), 1)
    mask = ((t - (nidx * CMP_STRIDE + CMP_BLOCK - 1)) >= 0) & (nidx < n_cmp)
    tpos = qb * TQ + lax.broadcasted_iota(jnp.int32, (TQ, 1), 0)
    for gi in range(KV_GROUPS):
        kc = kv_ref[0, 0, :, gi * HEAD_DIM:(gi + 1) * HEAD_DIM]
        vc = kv_ref[0, 1, :, gi * HEAD_DIM:(gi + 1) * HEAD_DIM]
        qs = jnp.concatenate([q[:, (gi * HPG + p) * HEAD_DIM:(gi * HPG + p + 1) * HEAD_DIM] for p in range(HPG)], axis=0)
        s = _dot_nt(qs, kc).reshape(HPG, TQ, npad) * SCALE + bias_ref[gi]
        s = jnp.where(mask[None], s, NEG)
        m = jnp.max(s, axis=-1, keepdims=True)
        e = jnp.where(mask[None], jnp.exp(s - m), 0.0)
        prob = e / jnp.maximum(jnp.sum(e, axis=-1, keepdims=True), 1e-30)
        o = _dot(prob.reshape(HPG * TQ, npad).astype(BF16), vc)
        for p in range(HPG):
            h = gi * HPG + p
            o_ref[0, :, h * HEAD_DIM:(h + 1) * HEAD_DIM] = o[p * TQ:(p + 1) * TQ]
        psum = prob[0]
        for p in range(1, HPG):
            psum = psum + prob[p]
        hi, lo = _split_bf16(psum)
        imp = _dot(hi, ov_ref[...]) + _dot(lo, ov_ref[...])
        sel_ref[0, gi] = _select(imp, tpos, nsb).astype(sel_ref.dtype)


def _cmp_p_call(qn, kvc, rel_bias, n_cmp):
    B, T, _ = qn.shape
    npad = kvc.shape[2]
    nsb = -(-T // SEL_BLOCK)
    dist = np.arange(T)[:, None] - (np.arange(npad)[None, :] * CMP_STRIDE + CMP_BLOCK - 1)
    bias = jnp.transpose(rel_bias[_bucket_np(dist)], (2, 0, 1)).reshape(KV_GROUPS, HPG, T, npad)
    ov = jnp.asarray(_overlap_np(n_cmp, nsb, npad, LANES), BF16)
    return pl.pallas_call(
        functools.partial(_cmp_p_kernel, n_cmp=n_cmp, nsb=nsb),
        grid=(B, T // TQ),
        in_specs=[pl.BlockSpec((1, TQ, C_HEADS * HEAD_DIM), lambda b, i: (b, i, 0)),
                  pl.BlockSpec((1, 2, npad, GH), lambda b, i: (b, 0, 0, 0)),
                  pl.BlockSpec((KV_GROUPS, HPG, TQ, npad), lambda b, i: (0, 0, i, 0)),
                  _full_spec(ov.shape)],
        out_specs=[pl.BlockSpec((1, TQ, C_HEADS * HEAD_DIM), lambda b, i: (b, i, 0)),
                   pl.BlockSpec((1, KV_GROUPS, TQ, LANES), lambda b, i: (b, 0, i, 0))],
        out_shape=[jax.ShapeDtypeStruct((B, T, C_HEADS * HEAD_DIM), F32),
                   jax.ShapeDtypeStruct((B, KV_GROUPS, T, LANES), BF16)],
        compiler_params=_params("arbitrary", "arbitrary"),
        name="cmp_attn_prompt",
    )(qn, kvc, bias, ov)


def _pattn_kernel(q_ref, k_ref, v_ref, bias_ref, sel_ref, o_ref, qs_ref, m_ref, l_ref, acc_ref, *, mode, nk):
    qb = pl.program_id(1)
    j = pl.program_id(2)
    if mode == "sel":
        kb = j
        active = j <= qb
    else:
        kb = qb - (nk - 1) + j
        active = kb >= 0

    @pl.when(j == 0)
    def _():
        q = q_ref[0]
        for gi in range(KV_GROUPS):
            for p in range(HPG):
                h = gi * HPG + p
                qs_ref[gi, p * TQ:(p + 1) * TQ, :] = q[:, h * HEAD_DIM:(h + 1) * HEAD_DIM]
        m_ref[...] = jnp.full(m_ref.shape, NEG, F32)
        l_ref[...] = jnp.zeros(l_ref.shape, F32)
        acc_ref[...] = jnp.zeros(acc_ref.shape, F32)

    @pl.when(active)
    def _():
        off = qb - kb
        bidx = jnp.minimum(off, 2)
        k = k_ref[0]
        v = v_ref[0].astype(BF16)
        ti = lax.broadcasted_iota(jnp.int32, (TQ, TQ), 0)
        sj = lax.broadcasted_iota(jnp.int32, (TQ, TQ), 1)
        dist = off * TQ + ti - sj
        base = dist >= 0
        if mode == "win":
            base = base & (dist < WINDOW)
        else:
            expand = (ti == (TQ // SEL_BLOCK) * kb + sj // SEL_BLOCK).astype(BF16)
        for gi in range(KV_GROUPS):
            mask = base
            if mode == "sel":
                mask = base & (_dot(sel_ref[0, gi], expand) > 0.5)
            kg = k[:, gi * HEAD_DIM:(gi + 1) * HEAD_DIM]
            vg = v[:, gi * HEAD_DIM:(gi + 1) * HEAD_DIM]
            s = _dot_nt(qs_ref[gi], kg).reshape(HPG, TQ, TQ) * SCALE + bias_ref[gi, bidx]
            s = jnp.where(mask[None], s, NEG)
            m_old = m_ref[gi]
            m_new = jnp.maximum(m_old, jnp.max(s, axis=-1, keepdims=True))
            e = jnp.where(mask[None], jnp.exp(s - m_new), 0.0)
            alpha = jnp.exp(m_old - m_new)
            l_ref[gi] = alpha * l_ref[gi] + jnp.sum(e, axis=-1, keepdims=True)
            pv = _dot(e.reshape(HPG * TQ, TQ).astype(BF16), vg)
            acc_ref[gi] = alpha * acc_ref[gi] + pv.reshape(HPG, TQ, HEAD_DIM)
            m_ref[gi] = m_new

    @pl.when(j == nk - 1)
    def _():
        for gi in range(KV_GROUPS):
            o = acc_ref[gi] / jnp.maximum(l_ref[gi], 1e-30)
            for p in range(HPG):
                h = gi * HPG + p
                o_ref[0, :, h * HEAD_DIM:(h + 1) * HEAD_DIM] = o[p]


def _toeplitz_bias(rel_bias):
    i = np.arange(TQ)[:, None] - np.arange(TQ)[None, :]
    idx = np.stack([_bucket_np(i), _bucket_np(TQ + i), np.full((TQ, TQ), N_BUCKETS - 1, np.int32)])
    assert 2 * TQ - (TQ - 1) >= MAX_DIST
    tiles = jnp.transpose(rel_bias[idx], (3, 0, 1, 2)).reshape(KV_GROUPS, HPG, 3, TQ, TQ)
    return jnp.transpose(tiles, (0, 2, 1, 3, 4))


def _pattn_call(qn, kn, rows, selmask, bias_tiles, mode):
    B, T, _ = qn.shape
    nq = T // TQ
    if mode == "sel":
        nk = nq
        kmap = lambda b, i, j: (b, jnp.minimum(j, i), 0)
        vmap_ = lambda b, i, j: (b, jnp.minimum(j, i), 1)
    else:
        nk = WINDOW // TQ + 1
        kmap = lambda b, i, j: (b, jnp.maximum(i - (nk - 1) + j, 0), 0)
        vmap_ = lambda b, i, j: (b, jnp.maximum(i - (nk - 1) + j, 0), 1)
    return pl.pallas_call(
        functools.partial(_pattn_kernel, mode=mode, nk=nk),
        grid=(B, nq, nk),
        in_specs=[pl.BlockSpec((1, TQ, C_HEADS * HEAD_DIM), lambda b, i, j: (b, i, 0)),
                  pl.BlockSpec((1, TQ, GH), kmap),
                  pl.BlockSpec((1, TQ, GH), vmap_),
                  pl.BlockSpec(bias_tiles.shape, lambda b, i, j: (0, 0, 0, 0, 0)),
                  pl.BlockSpec((1, KV_GROUPS, TQ, LANES), lambda b, i, j: (b, 0, i, 0))],
        out_specs=pl.BlockSpec((1, TQ, C_HEADS * HEAD_DIM), lambda b, i, j: (b, i, 0)),
        out_shape=jax.ShapeDtypeStruct((B, T, C_HEADS * HEAD_DIM), F32),
        scratch_shapes=[pltpu.VMEM((KV_GROUPS, HPG * TQ, HEAD_DIM), BF16),
                        pltpu.VMEM((KV_GROUPS, HPG, TQ, 1), F32),
                        pltpu.VMEM((KV_GROUPS, HPG, TQ, 1), F32),
                        pltpu.VMEM((KV_GROUPS, HPG, TQ, HEAD_DIM), F32)],
        compiler_params=_params("arbitrary", "arbitrary", "arbitrary"),
        name="attn_prompt_" + mode,
    )(qn, kn, rows, bias_tiles, selmask)


def _combine_kernel(x_ref, gt_ref, oc_ref, os_ref, ow_ref, gate_ref, w_ref, o_ref, mix_ref):
    gate = gate_ref[0]
    for h in range(C_HEADS):
        hs = slice(h * HEAD_DIM, (h + 1) * HEAD_DIM)
        mix_ref[:, hs] = (gate[:, h:h + 1] * oc_ref[0, :, hs]
                          + gate[:, C_HEADS + h:C_HEADS + h + 1] * os_ref[0, :, hs]
                          + gate[:, 2 * C_HEADS + h:2 * C_HEADS + h + 1] * ow_ref[0, :, hs]).astype(BF16)
    o_ref[0] = x_ref[0] + gt_ref[0] * _dot(mix_ref[...], w_ref[...])


def _combine_call(x3, gt, oc, os_, ow, gates, w_out, tm):
    NB, R, D = x3.shape
    nq = C_HEADS * HEAD_DIM
    return pl.pallas_call(
        _combine_kernel,
        grid=(NB, R // tm),
        in_specs=[_tok_spec(tm, D), _mod_spec(gt, tm), _tok_spec(tm, nq), _tok_spec(tm, nq), _tok_spec(tm, nq),
                  _tok_spec(tm, LANES), _full_spec(w_out.shape)],
        out_specs=_tok_spec(tm, D),
        out_shape=jax.ShapeDtypeStruct(x3.shape, F32),
        scratch_shapes=[pltpu.VMEM((tm, nq), BF16)],
        compiler_params=_params("arbitrary", "arbitrary"),
        name="nsa_combine",
    )(x3, gt, oc, os_, ow, gates, w_out)


def _cmp_s_kernel(q_ref, kv_ref, bias_ref, ov_ref, o_ref, sel_ref, *, n_cmp, nsb, ts, past):
    npad = kv_ref.shape[2]
    rows = HPG * ts
    t = past + lax.broadcasted_iota(jnp.int32, (rows, npad), 0) % ts
    nidx = lax.broadcasted_iota(jnp.int32, (rows, npad), 1)
    mask = ((t - (nidx * CMP_STRIDE + CMP_BLOCK - 1)) >= 0) & (nidx < n_cmp)
    tpos = past + lax.broadcasted_iota(jnp.int32, (ts, 1), 0)
    for gi in range(KV_GROUPS):
        kc = kv_ref[0, 0, :, gi * HEAD_DIM:(gi + 1) * HEAD_DIM]
        vc = kv_ref[0, 1, :, gi * HEAD_DIM:(gi + 1) * HEAD_DIM]
        s = _dot_nt(q_ref[0, gi], kc) * SCALE + bias_ref[gi]
        s = jnp.where(mask, s, NEG)
        m = jnp.max(s, axis=-1, keepdims=True)
        e = jnp.where(mask, jnp.exp(s - m), 0.0)
        prob = e / jnp.maximum(jnp.sum(e, axis=-1, keepdims=True), 1e-30)
        o_ref[0, gi] = _dot(prob.astype(BF16), vc)
        hi, lo = _split_bf16(prob)
        imp16 = _dot(hi, ov_ref[...]) + _dot(lo, ov_ref[...])
        imp = imp16[0:ts]
        for p in range(1, HPG):
            imp = imp + imp16[p * ts:(p + 1) * ts]
        sel_ref[0, gi] = _select(imp, tpos, nsb).astype(F32)


def _cmp_s_call(q2, kvc, rel_bias, n_cmp, nsb, ts, past):
    Bs = q2.shape[0]
    npad = kvc.shape[2]
    rows = HPG * ts
    lpad = -(-nsb // LANES) * LANES
    tq = past + np.arange(rows) % ts
    dist = tq[:, None] - (np.arange(npad)[None, :] * CMP_STRIDE + CMP_BLOCK - 1)
    tab = rel_bias[_bucket_np(dist)]
    pidx = np.arange(rows) // ts
    bias = jnp.stack([jnp.stack([tab[r, :, gi * HPG + pidx[r]] for r in range(rows)]) for gi in range(KV_GROUPS)])
    ov = jnp.asarray(_overlap_np(n_cmp, nsb, npad, lpad), BF16)
    return pl.pallas_call(
        functools.partial(_cmp_s_kernel, n_cmp=n_cmp, nsb=nsb, ts=ts, past=past),
        grid=(Bs,),
        in_specs=[pl.BlockSpec((1, KV_GROUPS, rows, HEAD_DIM), lambda b: (b, 0, 0, 0)),
                  pl.BlockSpec((1, 2, npad, GH), lambda b: (b, 0, 0, 0)),
                  _full_spec(bias.shape), _full_spec(ov.shape)],
        out_specs=[pl.BlockSpec((1, KV_GROUPS, rows, HEAD_DIM), lambda b: (b, 0, 0, 0)),
                   pl.BlockSpec((1, KV_GROUPS, ts, lpad), lambda b: (b, 0, 0, 0))],
        out_shape=[jax.ShapeDtypeStruct((Bs, KV_GROUPS, rows, HEAD_DIM), F32),
                   jax.ShapeDtypeStruct((Bs, KV_GROUPS, ts, lpad), F32)],
        compiler_params=_params("arbitrary"),
        name="cmp_attn_sample",
    )(q2, kvc, bias, ov)


def _decode_core(kp, vp, kn, vn, qblk, kng_col, eexp, bias_p, bias_n, mask_p, mask_n):
    qb = (qblk * kng_col).astype(BF16)

    def logits(k, bias):
        hi, lo = _split_bf16(k * k)
        ss = _dot(hi, eexp) + _dot(lo, eexp)
        r = lax.rsqrt(ss * (1.0 / HEAD_DIM) + EPS)
        return _dot(k.astype(BF16), qb) * r * SCALE + bias

    lp = jnp.where(mask_p, logits(kp, bias_p), NEG)
    ln = jnp.where(mask_n, logits(kn, bias_n), NEG)
    m = jnp.maximum(jnp.max(lp, axis=0, keepdims=True), jnp.max(ln, axis=0, keepdims=True))
    ep = jnp.where(mask_p, jnp.exp(lp - m), 0.0)
    en = jnp.where(mask_n, jnp.exp(ln - m), 0.0)
    denom = jnp.sum(ep, axis=0, keepdims=True) + jnp.sum(en, axis=0, keepdims=True)
    inv = 1.0 / jnp.maximum(denom, 1e-30)
    of = _dot_tn((ep * inv).astype(BF16), vp.astype(BF16)) + _dot_tn((en * inv).astype(BF16), vn.astype(BF16))
    ncol = of.shape[0]
    per = ncol // KV_GROUPS
    rg = lax.broadcasted_iota(jnp.int32, (ncol, HEAD_DIM), 0) // per
    o = jnp.zeros((ncol, HEAD_DIM), F32)
    for gi in range(KV_GROUPS):
        o = o + jnp.where(rg == gi, of[:, gi * HEAD_DIM:(gi + 1) * HEAD_DIM], 0.0)
    return o


def _new_key_mask(ts, ncol, rows):
    jn = lax.broadcasted_iota(jnp.int32, (rows, ncol), 0)
    tn = lax.broadcasted_iota(jnp.int32, (rows, ncol), 1) % ts
    return (jn <= tn) & (jn < ts)


def _sel_s_kernel(pt_ref, cache_ref, new_ref, q_ref, kng_ref, eexp_ref, bp_ref, bn_ref, selp_ref, seln_ref,
                  o_ref, buf_ref, sem_ref, *, n_pages, page_size, col0, ts):
    b = pl.program_id(0)
    nb = pl.num_programs(0)
    past = n_pages * page_size
    ncol = q_ref.shape[2]

    def fetch(sb, start):
        slot = sb % 2
        for p in range(n_pages):
            cp = _page_copy(cache_ref, buf_ref, sem_ref, pt_ref[sb, p], slot, p, col0, 2 * GH, page_size)
            cp.start() if start else cp.wait()

    @pl.when(b == 0)
    def _():
        fetch(b, True)

    @pl.when(b + 1 < nb)
    def _():
        fetch(b + 1, True)

    fetch(b, False)
    slot = b % 2
    kp = buf_ref[slot, :, 0:GH]
    vp = buf_ref[slot, :, GH:2 * GH]
    nblk = past // SEL_BLOCK
    mask_p = jnp.broadcast_to(selp_ref[0], (nblk, SEL_BLOCK, ncol)).reshape(past, ncol) > 0.5
    mask_n = _new_key_mask(ts, ncol, new_ref.shape[1]) & (seln_ref[0] > 0.5)
    o_ref[0] = _decode_core(kp, vp, new_ref[0, :, 0:GH], new_ref[0, :, GH:2 * GH], q_ref[0], kng_ref[...],
                            eexp_ref[...], bp_ref[...], bn_ref[...], mask_p, mask_n)


def _win_s_kernel(st_ref, new_ref, q_ref, kng_ref, eexp_ref, bp_ref, bn_ref, o_ref, nw_ref, *, ts, kpos0):
    wb = st_ref.shape[2]
    ncol = q_ref.shape[2]
    jp = lax.broadcasted_iota(jnp.int32, (wb, ncol), 0)
    tp = lax.broadcasted_iota(jnp.int32, (wb, ncol), 1) % ts
    dist = wb + tp - jp
    mask_p = (dist >= 0) & (dist < WINDOW) & (kpos0 + jp >= 0)
    mask_n = _new_key_mask(ts, ncol, new_ref.shape[1])
    st = st_ref[0, 0]
    o_ref[0] = _decode_core(st[:, 0:GH], st[:, GH:2 * GH], new_ref[0, :, 0:GH], new_ref[0, :, GH:2 * GH],
                            q_ref[0], kng_ref[...], eexp_ref[...], bp_ref[...], bn_ref[...], mask_p, mask_n)
    wout = nw_ref.shape[2]
    keep = wout - ts
    nw_ref[0, 0, 0:keep, :] = st_ref[0, 0, pl.ds(wb - keep, keep), :]
    nw_ref[0, 0, keep:wout, :] = new_ref[0, 0:ts, :]


def _decode_consts(rel_bias, kn_g, ts, key_dist):
    ncol = C_HEADS * ts
    head = np.arange(ncol) // ts
    tcol = np.arange(ncol) % ts
    bias = rel_bias[_bucket_np(key_dist[:, tcol]), head[None, :]]
    return bias


def _sel_s_call(cache3, page_table, layer, new_rows, qblk, kng_col, eexp, bias_p, bias_n, selp, seln, ts):
    Bs, n_pages = page_table.shape
    page_size = cache3.shape[1]
    past = n_pages * page_size
    ncol = qblk.shape[2]
    nblk = past // SEL_BLOCK
    m3 = lambda b, pt: (b, 0, 0)
    c2 = lambda b, pt: (0, 0)
    grid_spec = pltpu.PrefetchScalarGridSpec(
        num_scalar_prefetch=1,
        grid=(Bs,),
        in_specs=[pl.BlockSpec(memory_space=pl.ANY),
                  pl.BlockSpec((1,) + new_rows.shape[1:], m3),
                  pl.BlockSpec((1, GH, ncol), m3),
                  pl.BlockSpec(kng_col.shape, c2), pl.BlockSpec(eexp.shape, c2),
                  pl.BlockSpec(bias_p.shape, c2), pl.BlockSpec(bias_n.shape, c2),
                  pl.BlockSpec((1, nblk, 1, ncol), lambda b, pt: (b, 0, 0, 0)),
                  pl.BlockSpec((1, 1, ncol), m3)],
        out_specs=pl.BlockSpec((1, ncol, HEAD_DIM), m3),
        scratch_shapes=[pltpu.VMEM((2, past, 2 * GH), F32), pltpu.SemaphoreType.DMA((2,))],
    )
    return pl.pallas_call(
        functools.partial(_sel_s_kernel, n_pages=n_pages, page_size=page_size, col0=layer * 2 * GH, ts=ts),
        grid_spec=grid_spec,
        out_shape=jax.ShapeDtypeStruct((Bs, ncol, HEAD_DIM), F32),
        compiler_params=_params("arbitrary"),
        name="sel_attn_sample",
    )(page_table, cache3, new_rows, qblk, kng_col, eexp, bias_p, bias_n, selp, seln)


def _win_s_call(state4, layer, new_rows, qblk, kng_col, eexp, bias_p, bias_n, ts, past):
    n_l, Bs, wb, _ = state4.shape
    ncol = qblk.shape[2]
    wout = min(WINDOW, wb + ts)
    m3 = lambda b: (b, 0, 0)
    c2 = lambda b: (0, 0)
    return pl.pallas_call(
        functools.partial(_win_s_kernel, ts=ts, kpos0=past - wb),
        grid=(Bs,),
        in_specs=[pl.BlockSpec((1, 1, wb, 2 * GH), lambda b: (layer, b, 0, 0)),
                  pl.BlockSpec((1,) + new_rows.shape[1:], m3),
                  pl.BlockSpec((1, GH, ncol), m3),
                  pl.BlockSpec(kng_col.shape, c2), pl.BlockSpec(eexp.shape, c2),
                  pl.BlockSpec(bias_p.shape, c2), pl.BlockSpec(bias_n.shape, c2)],
        out_specs=[pl.BlockSpec((1, ncol, HEAD_DIM), m3),
                   pl.BlockSpec((1, 1, wout, 2 * GH), lambda b: (0, b, 0, 0))],
        out_shape=[jax.ShapeDtypeStruct((Bs, ncol, HEAD_DIM), F32),
                   jax.ShapeDtypeStruct((1, Bs, wout, 2 * GH), F32)],
        compiler_params=_params("arbitrary"),
        name="win_attn_sample",
    )(state4, new_rows, qblk, kng_col, eexp, bias_p, bias_n)


def _nsa_prompt(x, mods, g, w_in_pad, w_out, qn_g, kn_g, pe, w1, w2, rel_bias, bias_tiles, tm):
    B, T, D = x.shape
    N = B * T
    nq = C_HEADS * HEAD_DIM
    q, cmp_rows, sel_rows, win_rows, gates = _nsa_proj_call(x, mods[3], mods[4], g, w_in_pad, tm)
    tr = min(2048, N)
    assert N % tr == 0
    qn = _headnorm_call(q.reshape(N, nq), qn_g, nq, tr).reshape(B, T, nq)
    seln = _headnorm_call(sel_rows.reshape(N, 2 * GH), kn_g, GH, tr).reshape(B, T, GH)
    winn = _headnorm_call(win_rows.reshape(N, 2 * GH), kn_g, GH, tr).reshape(B, T, GH)
    n_cmp = (T - CMP_BLOCK) // CMP_STRIDE + 1
    kvc = _compress_p_call(cmp_rows, pe, w1, w2, kn_g)
    o_cmp, selmask = _cmp_p_call(qn, kvc, rel_bias, n_cmp)
    o_sel = _pattn_call(qn, seln, sel_rows, selmask, bias_tiles, "sel")
    o_win = _pattn_call(qn, winn, win_rows, selmask, bias_tiles, "win")
    x = _combine_call(x, mods[5], o_cmp, o_sel, o_win, gates, w_out, tm)
    wk = min(WINDOW, T)
    return x, cmp_rows, sel_rows, win_rows[:, T - wk:]


def _nsa_sample(x, mods, g, w_in_pad, w_out, qn_g, kn_g, pe, w1, w2, rel_bias, cache_cmp3, cache_sel3,
                page_table, state_win4, layer, ts, bsz):
    R = ts * bsz
    nq = C_HEADS * HEAD_DIM
    past = page_table.shape[1] * cache_cmp3.shape[1]
    assert past % CMP_STRIDE == 0 and past % SEL_BLOCK == 0 and ts <= 8 and ts <= CMP_STRIDE
    q, cmp_rows, sel_rows, win_rows, gates = _nsa_proj_call(x, mods[3], mods[4], g, w_in_pad, R)
    qn = _headnorm_call(q.reshape(R, nq), qn_g, nq, R)
    q5 = jnp.transpose(qn.reshape(ts, bsz, KV_GROUPS, HPG, HEAD_DIM), (1, 2, 3, 0, 4))
    q2 = q5.reshape(bsz, KV_GROUPS, HPG * ts, HEAD_DIM)
    qd = jnp.transpose(q5.astype(F32).reshape(bsz, KV_GROUPS, HPG * ts, HEAD_DIM), (0, 1, 3, 2))
    qblk = (qd[:, :, :, None, :] * jnp.eye(KV_GROUPS, dtype=F32)[None, :, None, :, None]).reshape(bsz, GH, C_HEADS * ts)
    ncol = C_HEADS * ts
    tk = past + ts
    n_cmp = (tk - CMP_BLOCK) // CMP_STRIDE + 1
    assert n_cmp <= past // CMP_STRIDE - 1 + 1 and (n_cmp - 1) * CMP_STRIDE + CMP_BLOCK <= past
    nsb = -(-tk // SEL_BLOCK)
    kvc = _compress_s_call(cache_cmp3, page_table, layer, pe, w1, w2, kn_g)
    o_cmp, selw = _cmp_s_call(q2, kvc, rel_bias, n_cmp, nsb, ts, past)
    selt = jnp.transpose(selw, (0, 3, 1, 2))
    selt = jnp.broadcast_to(selt[:, :, :, None, :], selt.shape[:3] + (HPG, ts)).reshape(bsz, -1, 1, ncol)
    nblk = past // SEL_BLOCK
    assert nsb == nblk + 1
    selp, seln = selt[:, :nblk], selt[:, nblk]
    to_seq = lambda a: jnp.pad(jnp.transpose(a.reshape(ts, bsz, 2 * GH), (1, 0, 2)), ((0, 0), (0, 8 - ts), (0, 0)))
    kng_col = jnp.tile(kn_g, KV_GROUPS).reshape(GH, 1)
    eexp = np.zeros((GH, ncol), np.float32)
    for gi in range(KV_GROUPS):
        eexp[gi * HEAD_DIM:(gi + 1) * HEAD_DIM, gi * HPG * ts:(gi + 1) * HPG * ts] = 1.0
    eexp = jnp.asarray(eexp, BF16)
    tq = np.arange(ts)[None, :]
    bias_new = _decode_consts(rel_bias, kn_g, ts, tq - np.arange(8)[:, None])
    bias_selp = _decode_consts(rel_bias, kn_g, ts, past + tq - np.arange(past)[:, None])
    o_sel = _sel_s_call(cache_sel3, page_table, layer, to_seq(sel_rows), qblk, kng_col, eexp, bias_selp, bias_new,
                        selp, seln, ts)
    wb = state_win4.shape[2]
    bias_winp = _decode_consts(rel_bias, kn_g, ts, wb + tq - np.arange(wb)[:, None])
    o_win, new_win = _win_s_call(state_win4, layer, to_seq(win_rows), qblk, kng_col, eexp, bias_winp, bias_new,
                                 ts, past)
    back = lambda o: jnp.transpose(o.reshape(bsz, KV_GROUPS, HPG, ts, HEAD_DIM), (3, 0, 1, 2, 4)).reshape(1, R, nq)
    x = _combine_call(x, mods[5], back(o_cmp), back(o_sel), back(o_win), gates, w_out, R)
    rows_out = lambda a: jnp.transpose(a.reshape(ts, bsz, 2, KV_GROUPS, HEAD_DIM), (1, 0, 2, 3, 4))
    return x, rows_out(cmp_rows), rows_out(sel_rows), new_win[0].reshape(bsz, -1, 2, KV_GROUPS, HEAD_DIM)


def kernel(x_prompt, x_sample, c_prompt, c_sample, cache_cmp_kv, cache_sel_kv, page_table, state_win_kv, state_conv, ada_w, ada_b, norm_g, ffn_w1, ffn_w2, even_w_in, even_w_out, gmlp_v_g, gmlp_ws, gmlp_bs, conv_w, conv_b, conv_ln_g, conv_ln_b, nsa_w_in, nsa_w_out, q_norm_g, k_norm_g, cmp_pe, cmp_w1, cmp_w2, rel_bias):
    B, T, D = x_prompt.shape
    Bs, Ts, _ = x_sample.shape
    depth = ada_w.shape[0]
    n_odd = nsa_w_in.shape[0]
    tm = min(512, T)
    Rs = Ts * Bs

    ada = _ada_call(jnp.concatenate([c_prompt, c_sample], axis=0), ada_w, ada_b)
    xp = x_prompt
    xs = jnp.transpose(x_sample, (1, 0, 2)).reshape(1, Rs, D)
    n_pool, page_size = cache_cmp_kv.shape[:2]
    cache_cmp3 = cache_cmp_kv.reshape(n_pool, page_size, -1)
    cache_sel3 = cache_sel_kv.reshape(n_pool, page_size, -1)
    state_win4 = state_win_kv.reshape(state_win_kv.shape[:3] + (2 * GH,))
    bias_tiles = _toeplitz_bias(rel_bias)
    pad_cols = (-nsa_w_in.shape[2] + C_HEADS * HEAD_DIM + 6 * GH + LANES)

    cmp_p, cmp_s, sel_p, sel_s, win_p, win_s, conv_p, conv_s, v_s = ([] for _ in range(9))
    for l in range(depth):
        mp = [ada[l, :B, k * D:(k + 1) * D].reshape(B, 1, D) for k in range(9)]
        ms = [jnp.tile(ada[l, B:, k * D:(k + 1) * D], (Ts, 1)).reshape(1, Rs, D) for k in range(9)]
        w1 = ffn_w1[l].astype(BF16)
        w2 = ffn_w2[l].astype(BF16)
        xp = _ffn_call(xp, mp[0], mp[1], mp[2], norm_g[l, 0], w1[0], w2[0], tm)
        xs = _ffn_call(xs, ms[0], ms[1], ms[2], norm_g[l, 0], w1[0], w2[0], Rs)
        if l % 2 == 0:
            e = l // 2
            w_in = even_w_in[e].astype(BF16)
            w_out = even_w_out[e].astype(BF16)
            prm = (gmlp_v_g[e], gmlp_ws[e], gmlp_bs[e])
            cprm = (conv_w[e], conv_b[e], conv_ln_g[e], conv_ln_b[e])
            xp, cst_p = _even_call(xp, mp[3], mp[4], mp[5], norm_g[l, 1], w_in, w_out, *prm, *cprm, tm)
            xs, cst_s, vrow = _even_s_call(xs, ms[3], ms[4], ms[5], norm_g[l, 1], w_in, w_out, *prm,
                                           state_conv[e], *cprm, Ts, Bs)
            conv_p.append(cst_p)
            conv_s.append(cst_s)
            v_s.append(vrow)
        else:
            o = l // 2
            w_in_pad = jnp.pad(nsa_w_in[o], ((0, 0), (0, pad_cols))).astype(BF16)
            w_out = nsa_w_out[o].astype(BF16)
            prm = (norm_g[l, 1], w_in_pad, w_out, q_norm_g[o], k_norm_g[o], cmp_pe[o], cmp_w1[o], cmp_w2[o], rel_bias)
            xp, rc_p, rs_p, w_p = _nsa_prompt(xp, mp, *prm, bias_tiles, tm)
            xs, rc_s, rs_s, w_s = _nsa_sample(xs, ms, *prm, cache_cmp3, cache_sel3, page_table, state_win4, o, Ts, Bs)
            kv6 = lambda a: a.reshape(B, -1, 2, KV_GROUPS, HEAD_DIM)
            cmp_p.append(kv6(rc_p))
            sel_p.append(kv6(rs_p))
            win_p.append(kv6(w_p))
            cmp_s.append(rc_s)
            sel_s.append(rs_s)
            win_s.append(w_s)
        xp = _ffn_call(xp, mp[6], mp[7], mp[8], norm_g[l, 2], w1[1], w2[1], tm)
        xs = _ffn_call(xs, ms[6], ms[7], ms[8], norm_g[l, 2], w1[1], w2[1], Rs)
    y_sample = jnp.transpose(xs.reshape(Ts, Bs, D), (1, 0, 2))
    return (xp, y_sample, jnp.stack(cmp_p, axis=2), jnp.stack(cmp_s, axis=2), jnp.stack(sel_p, axis=2),
            jnp.stack(sel_s, axis=2), jnp.stack(win_p, axis=0), jnp.stack(win_s, axis=0),
            jnp.stack(conv_p, axis=0), jnp.stack(conv_s, axis=0), jnp.stack(v_s, axis=0))
```

```python
import functools
import math

import numpy as np
import jax
import jax.numpy as jnp
from jax import lax
from jax.experimental import pallas as pl
from jax.experimental.pallas import tpu as pltpu

F32 = jnp.float32
BF16 = jnp.bfloat16

A_GROUPS = 8
CHUNK = 128
CONV_W = 31
C_HEADS = 16
KV_GROUPS = 4
HEAD_DIM = 64
HPG = C_HEADS // KV_GROUPS
GH = KV_GROUPS * HEAD_DIM
CMP_BLOCK = 32
CMP_STRIDE = 16
SEL_BLOCK = 64
N_SEL = 16
WINDOW = 512
N_BUCKETS = 32
MAX_DIST = 128
SCALE = HEAD_DIM ** -0.5
EPS = 1e-6
NEG = -1e30
FORCE = 1e6

VMEM_LIMIT_BYTES = 60 * 2 ** 20
LANES = 128
TQ = 128
TK = 256
STACK = 4
HIST = 32


def _params(*sem):
    return pltpu.CompilerParams(dimension_semantics=sem, vmem_limit_bytes=VMEM_LIMIT_BYTES)


def _dot(a, b):
    return jnp.dot(a, b, preferred_element_type=F32)


def _dot_nt(a, b):
    return lax.dot_general(a, b, (((1,), (1,)), ((), ())), preferred_element_type=F32)


def _dot_tn(a, b):
    return lax.dot_general(a, b, (((0,), (0,)), ((), ())), preferred_element_type=F32)


def _split_bf16(x):
    hi = x.astype(BF16)
    lo = (x - hi.astype(F32)).astype(BF16)
    return hi, lo


def _sigmoid(x):
    return 1.0 / (1.0 + jnp.exp(-x))


def _silu(x):
    return x * _sigmoid(x)


def _gelu(x):
    return 0.5 * x * (1.0 + jnp.tanh(math.sqrt(2.0 / math.pi) * (x + 0.044715 * (x * x * x))))


def _modulate(x, g, shift, scale):
    y = x * lax.rsqrt(jnp.mean(x * x, axis=-1, keepdims=True) + EPS)
    return (y * g) * (1.0 + scale) + shift


def _bucket_np(dist):
    exact = N_BUCKETS // 2
    d = np.maximum(dist, 0)
    df = np.maximum(d, 1).astype(np.float32)
    large = exact + (np.log(df / np.float32(exact)) / np.float32(math.log(MAX_DIST / exact))
                     * np.float32(N_BUCKETS - exact)).astype(np.int32)
    return np.where(d < exact, d, np.minimum(large, N_BUCKETS - 1)).astype(np.int32)


def _bucket_thresholds():
    b = _bucket_np(np.arange(4 * MAX_DIST))
    assert (np.diff(b) >= 0).all() and b[MAX_DIST] == N_BUCKETS - 1
    return [int(np.argmax(b >= k)) for k in range(N_BUCKETS)]


_THR = _bucket_thresholds()


def _bias_lut(d, rb_ref, h):
    val = jnp.full(d.shape, rb_ref[0, h], F32)
    for k in range(1, N_BUCKETS):
        val = jnp.where(d >= _THR[k], rb_ref[k, h], val)
    return val


_SMEM_SPEC = pl.BlockSpec(memory_space=pltpu.SMEM)


def _full_spec(shape):
    n = len(shape)
    return pl.BlockSpec(shape, lambda *_: (0,) * n)


def _tok_spec(tm, width):
    return pl.BlockSpec((1, tm, width), lambda b, i: (b, i, 0))


def _mod_spec(mod, tm):
    if mod.shape[1] == 1:
        return pl.BlockSpec((1, 1, mod.shape[2]), lambda b, i: (b, 0, 0))
    return pl.BlockSpec((1, tm, mod.shape[2]), lambda b, i: (b, i, 0))


def _ada_kernel(c_ref, w_ref, b_ref, o_ref):
    c = c_ref[...]
    o_ref[0] = _dot(_silu(c).astype(BF16), w_ref[0].astype(BF16)) + b_ref[0]


def _ada_call(c_all, ada_w, ada_b):
    L, D, N = ada_w.shape
    M = c_all.shape[0]
    tn = 1024
    return pl.pallas_call(
        _ada_kernel,
        grid=(L, N // tn),
        in_specs=[pl.BlockSpec((M, D), lambda l, j: (0, 0)),
                  pl.BlockSpec((1, D, tn), lambda l, j: (l, 0, j)),
                  pl.BlockSpec((1, 1, tn), lambda l, j: (l, 0, j))],
        out_specs=pl.BlockSpec((1, M, tn), lambda l, j: (l, 0, j)),
        out_shape=jax.ShapeDtypeStruct((L, M, N), F32),
        compiler_params=_params("arbitrary", "arbitrary"),
        name="ada",
    )(c_all, ada_w, ada_b.reshape(L, 1, N))


def _ffn_kernel(x_ref, sh_ref, sc_ref, gt_ref, g_ref, w1_ref, w2_ref, o_ref, acc_ref, *, ff, tf):
    x = x_ref[0]
    h = _modulate(x, g_ref[...], sh_ref[0], sc_ref[0]).astype(BF16)
    for c in range(ff // tf):
        a = _dot(h, w1_ref[:, c * tf:(c + 1) * tf])
        b = _dot(h, w1_ref[:, ff + c * tf:ff + (c + 1) * tf])
        t = (_silu(a) * b).astype(BF16)
        part = _dot(t, w2_ref[c * tf:(c + 1) * tf, :])
        if c == 0:
            acc_ref[...] = part
        else:
            acc_ref[...] += part
    o_ref[0] = x + 0.5 * gt_ref[0] * acc_ref[...]


def _ffn_call(x3, sh, sc, gt, g, w1, w2, tm):
    NB, R, D = x3.shape
    ff = w2.shape[0]
    tf = 256
    return pl.pallas_call(
        functools.partial(_ffn_kernel, ff=ff, tf=tf),
        grid=(NB, R // tm),
        in_specs=[_tok_spec(tm, D), _mod_spec(sh, tm), _mod_spec(sc, tm), _mod_spec(gt, tm),
                  _full_spec((1, D)), _full_spec(w1.shape), _full_spec(w2.shape)],
        out_specs=_tok_spec(tm, D),
        out_shape=jax.ShapeDtypeStruct(x3.shape, F32),
        scratch_shapes=[pltpu.VMEM((tm, D), F32)],
        compiler_params=_params("arbitrary", "arbitrary"),
        name="ffn",
    )(x3, sh, sc, gt, g.reshape(1, D), w1, w2)


def _even_kernel(x_ref, sh_ref, sc_ref, gt_ref, g_ref, win_ref, wout_ref, vg_ref, ws_ref, bs_ref,
                 cw_ref, cb_ref, lg_ref, lb_ref, o_ref, cs_ref, ext_ref, sa_ref, *, tm, aw):
    @pl.when(pl.program_id(1) == 0)
    def _():
        ext_ref[0:HIST, :] = jnp.zeros((HIST, ext_ref.shape[1]), F32)

    x = x_ref[0]
    h = _modulate(x, g_ref[...], sh_ref[0], sc_ref[0]).astype(BF16)
    z = _dot(h, win_ref[...])
    u = _gelu(z[:, 0:aw])
    gv = _gelu(z[:, aw:2 * aw])
    v = gv * lax.rsqrt(jnp.mean(gv * gv, axis=-1, keepdims=True) + EPS) * vg_ref[...]
    vb = v.astype(BF16)
    row = lax.broadcasted_iota(jnp.int32, (CHUNK, CHUNK), 0)
    col = lax.broadcasted_iota(jnp.int32, (CHUNK, CHUNK), 1)
    wm = [jnp.where(row >= col, ws_ref[gi], 0.0).astype(BF16) for gi in range(A_GROUPS)]
    gw = aw // A_GROUPS
    first_half = col < gw
    for c in range(tm // CHUNK):
        for q in range(aw // LANES):
            vq = vb[c * CHUNK:(c + 1) * CHUNK, q * LANES:(q + 1) * LANES]
            s0 = _dot(wm[2 * q], vq)
            s1 = _dot(wm[2 * q + 1], vq)
            sa_ref[c * CHUNK:(c + 1) * CHUNK, q * LANES:(q + 1) * LANES] = (
                jnp.where(first_half, s0, s1) + bs_ref[:, q * LANES:(q + 1) * LANES])
    a_out = u * sa_ref[...]
    glu = z[:, 2 * aw:3 * aw] * _sigmoid(z[:, 3 * aw:4 * aw])
    ext_ref[HIST:HIST + tm, :] = glu
    off = HIST - (CONV_W - 1)
    conv = cb_ref[...] + ext_ref[pl.ds(off, tm), :] * cw_ref[0:1, :]
    for k in range(1, CONV_W):
        conv = conv + ext_ref[pl.ds(off + k, tm), :] * cw_ref[k:k + 1, :]
    cs_ref[0] = ext_ref[pl.ds(tm + off, CONV_W - 1), :]
    ext_ref[0:HIST, :] = ext_ref[tm:tm + HIST, :]
    mu = jnp.mean(conv, axis=-1, keepdims=True)
    cc = conv - mu
    var = jnp.mean(cc * cc, axis=-1, keepdims=True)
    b_out = _silu(cc * lax.rsqrt(var + EPS) * lg_ref[...] + lb_ref[...])
    out = _dot(a_out.astype(BF16), wout_ref[0:aw, :]) + _dot(b_out.astype(BF16), wout_ref[aw:, :])
    o_ref[0] = x + gt_ref[0] * out


def _even_call(x3, sh, sc, gt, g, w_in, w_out, v_g, ws, bs, cw, cb, ln_g, ln_b, tm):
    B, T, D = x3.shape
    aw = v_g.shape[0]
    bw = cw.shape[1]
    assert aw == bw and aw // A_GROUPS * 2 == LANES and T % tm == 0 and tm % CHUNK == 0
    bs_exp = jnp.repeat(bs.T, aw // A_GROUPS, axis=1)
    cw_pad = jnp.pad(cw, ((0, 1), (0, 0)))
    row = lambda a: a.reshape(1, -1)
    return pl.pallas_call(
        functools.partial(_even_kernel, tm=tm, aw=aw),
        grid=(B, T // tm),
        in_specs=[_tok_spec(tm, D), _mod_spec(sh, tm), _mod_spec(sc, tm), _mod_spec(gt, tm),
                  _full_spec((1, D)), _full_spec(w_in.shape), _full_spec(w_out.shape),
                  _full_spec((1, aw)), _full_spec(ws.shape), _full_spec(bs_exp.shape),
                  _full_spec(cw_pad.shape), _full_spec((1, bw)), _full_spec((1, bw)), _full_spec((1, bw))],
        out_specs=[_tok_spec(tm, D), pl.BlockSpec((1, CONV_W - 1, bw), lambda b, i: (b, 0, 0))],
        out_shape=[jax.ShapeDtypeStruct(x3.shape, F32), jax.ShapeDtypeStruct((B, CONV_W - 1, bw), F32)],
        scratch_shapes=[pltpu.VMEM((tm + HIST, bw), F32), pltpu.VMEM((tm, aw), F32)],
        compiler_params=_params("arbitrary", "arbitrary"),
        name="even_prompt",
    )(x3, sh, sc, gt, row(g), w_in, w_out, row(v_g), ws, bs_exp, cw_pad, row(cb), row(ln_g), row(ln_b))


def _even_s_kernel(x_ref, sh_ref, sc_ref, gt_ref, g_ref, win_ref, wout_ref, vg_ref, coef_ref, bsa_ref,
                   st_ref, cw_ref, cb_ref, lg_ref, lb_ref, o_ref, cs_ref, v_ref, *, ts, bsz, aw):
    x = x_ref[0]
    h = _modulate(x, g_ref[...], sh_ref[0], sc_ref[0]).astype(BF16)
    z = _dot(h, win_ref[...])
    u = _gelu(z[:, 0:aw])
    gv = _gelu(z[:, aw:2 * aw])
    v = gv * lax.rsqrt(jnp.mean(gv * gv, axis=-1, keepdims=True) + EPS) * vg_ref[...]
    glu = z[:, 2 * aw:3 * aw] * _sigmoid(z[:, 3 * aw:4 * aw])
    hist = CONV_W - 1
    sl = lambda a, t: a[t * bsz:(t + 1) * bsz]
    a_parts, b_parts = [], []
    for t in range(ts):
        v_ref[t] = sl(v, t)
        s = bsa_ref[t]
        for j in range(t + 1):
            s = s + coef_ref[t, j] * sl(v, j)
        a_parts.append(sl(u, t) * s)
        conv = cb_ref[...]
        for m in range(t, hist):
            conv = conv + st_ref[m] * cw_ref[m - t:m - t + 1, :]
        for j in range(t + 1):
            conv = conv + sl(glu, j) * cw_ref[hist - t + j:hist - t + j + 1, :]
        mu = jnp.mean(conv, axis=-1, keepdims=True)
        cc = conv - mu
        var = jnp.mean(cc * cc, axis=-1, keepdims=True)
        b_parts.append(_silu(cc * lax.rsqrt(var + EPS) * lg_ref[...] + lb_ref[...]))
    for i in range(hist):
        cs_ref[i] = st_ref[i + ts] if i + ts < hist else sl(glu, i + ts - hist)
    a_out = jnp.concatenate(a_parts, axis=0).astype(BF16)
    b_out = jnp.concatenate(b_parts, axis=0).astype(BF16)
    out = _dot(a_out, wout_ref[0:aw, :]) + _dot(b_out, wout_ref[aw:, :])
    o_ref[0] = x + gt_ref[0] * out


def _even_s_call(x3, sh, sc, gt, g, w_in, w_out, v_g, ws, bs, state, cw, cb, ln_g, ln_b, ts, bsz):
    _, R, D = x3.shape
    aw = v_g.shape[0]
    bw = cw.shape[1]
    gw = aw // A_GROUPS
    hist = CONV_W - 1
    assert ts <= CHUNK and ts <= hist
    coef = jnp.repeat(jnp.transpose(ws[:, :ts, :ts], (1, 2, 0)), gw, axis=2).reshape(ts, ts, 1, aw)
    bsa = jnp.repeat(bs[:, :ts].T, gw, axis=1).reshape(ts, 1, aw)
    st = jnp.transpose(state, (1, 0, 2))
    row = lambda a: a.reshape(1, -1)
    args = (x3, sh, sc, gt, row(g), w_in, w_out, row(v_g), coef, bsa, st, cw, row(cb), row(ln_g), row(ln_b))
    x_new, cs, v = pl.pallas_call(
        functools.partial(_even_s_kernel, ts=ts, bsz=bsz, aw=aw),
        grid=(1,),
        in_specs=[_full_spec(a.shape) for a in args],
        out_specs=[_full_spec(x3.shape), _full_spec((hist, bsz, bw)), _full_spec((ts, bsz, aw))],
        out_shape=[jax.ShapeDtypeStruct(x3.shape, F32), jax.ShapeDtypeStruct((hist, bsz, bw), F32),
                   jax.ShapeDtypeStruct((ts, bsz, aw), F32)],
        compiler_params=_params("arbitrary"),
        name="even_sample",
    )(*args)
    return x_new, jnp.transpose(cs, (1, 0, 2)), jnp.transpose(v, (1, 0, 2))


def _nsa_proj_kernel(x_ref, sh_ref, sc_ref, g_ref, w_ref, q_ref, cmp_ref, sel_ref, win_ref, gate_ref, gatet_ref):
    x = x_ref[0]
    h = _modulate(x, g_ref[...], sh_ref[0], sc_ref[0]).astype(BF16)
    z = _dot(h, w_ref[...])
    nq = C_HEADS * HEAD_DIM
    q_ref[0] = z[:, 0:nq]
    cmp_ref[0] = z[:, nq:nq + 2 * GH]
    sel_ref[0] = z[:, nq + 2 * GH:nq + 4 * GH]
    win_ref[0] = z[:, nq + 4 * GH:nq + 6 * GH]
    gate = _sigmoid(z[:, nq + 6 * GH:nq + 6 * GH + LANES])
    gate_ref[0] = gate
    gatet_ref[0] = gate.T


def _nsa_proj_call(x3, sh, sc, g, w_in_pad, tm):
    NB, R, D = x3.shape
    nq = C_HEADS * HEAD_DIM
    widths = (nq, 2 * GH, 2 * GH, 2 * GH, LANES)
    return pl.pallas_call(
        _nsa_proj_kernel,
        grid=(NB, R // tm),
        in_specs=[_tok_spec(tm, D), _mod_spec(sh, tm), _mod_spec(sc, tm), _full_spec((1, D)),
                  _full_spec(w_in_pad.shape)],
        out_specs=[_tok_spec(tm, w) for w in widths] + [pl.BlockSpec((1, LANES, tm), lambda b, i: (b, 0, i))],
        out_shape=[jax.ShapeDtypeStruct((NB, R, w), F32) for w in widths]
        + [jax.ShapeDtypeStruct((NB, LANES, R), F32)],
        compiler_params=_params("arbitrary", "arbitrary"),
        name="nsa_proj",
    )(x3, sh, sc, g.reshape(1, D), w_in_pad)


def _headnorm_kernel(x_ref, g_ref, o_ref, *, width, scale):
    lane = lax.broadcasted_iota(jnp.int32, (1, LANES), 1)
    lo = lane < HEAD_DIM
    for c in range(width // LANES):
        x = x_ref[:, c * LANES:(c + 1) * LANES]
        sq = x * x
        s_lo = jnp.sum(jnp.where(lo, sq, 0.0), axis=-1, keepdims=True)
        s_hi = jnp.sum(jnp.where(lo, 0.0, sq), axis=-1, keepdims=True)
        ss = jnp.where(lo, s_lo, s_hi)
        y = x * lax.rsqrt(ss * (1.0 / HEAD_DIM) + EPS) * g_ref[...]
        if scale != 1.0:
            y = y * scale
        o_ref[:, c * LANES:(c + 1) * LANES] = y.astype(o_ref.dtype)


def _headnorm_call(x2, g, width, tr, scale=1.0):
    assert math.frexp(scale)[0] == 0.5
    N = x2.shape[0]
    g2 = jnp.tile(g, LANES // HEAD_DIM).reshape(1, LANES)
    return pl.pallas_call(
        functools.partial(_headnorm_kernel, width=width, scale=scale),
        grid=(N // tr,),
        in_specs=[pl.BlockSpec((tr, width), lambda i: (i, 0)), _full_spec((1, LANES))],
        out_specs=pl.BlockSpec((tr, width), lambda i: (i, 0)),
        out_shape=jax.ShapeDtypeStruct((N, width), BF16),
        compiler_params=_params("arbitrary"),
        name="headnorm",
    )(x2, g2)


def _compress_core(load_rows, n, pe_ref, w1_ref, w2_ref, kng_ref, o_ref, is_k):
    hid = w2_ref.shape[1]
    nstk = CMP_STRIDE // STACK
    accs = [None] * KV_GROUPS
    ctop = jnp.zeros((1, hid), F32)
    cbot = jnp.zeros((1, hid), F32)
    pe = pe_ref[0].astype(BF16)
    gpl = LANES // HEAD_DIM
    for l4 in range(nstk):
        xs = [[load_rows(l4 * STACK + i, s) for s in range(KV_GROUPS // gpl)] for i in range(STACK)]
        w = w1_ref[0, l4]
        r = _dot(pe, w)
        ctop = ctop + r[l4:l4 + 1, 0:hid]
        cbot = cbot + r[nstk + l4:nstk + l4 + 1, hid:2 * hid]
        for gi in range(KV_GROUPS):
            lo = (gi % gpl) * HEAD_DIM
            xcat = jnp.concatenate([x[gi // gpl][:, lo:lo + HEAD_DIM] for x in xs], axis=1).astype(BF16)
            part = _dot(xcat, w)
            accs[gi] = part if accs[gi] is None else accs[gi] + part
    for gi in range(KV_GROUPS):
        a = accs[gi][:, 0:hid] + ctop
        b = pltpu.roll(accs[gi][:, hid:2 * hid] + cbot, n - 1, 0)
        y = _dot(_gelu(a + b).astype(BF16), w2_ref[0])
        yn = y * lax.rsqrt(jnp.mean(y * y, axis=-1, keepdims=True) + EPS) * kng_ref[...]
        o_ref[0, 0, :, gi * HEAD_DIM:(gi + 1) * HEAD_DIM] = jnp.where(is_k, yn, y).astype(o_ref.dtype)


def _compress_p_kernel(x_ref, pe_ref, w1_ref, w2_ref, kng_ref, o_ref, *, n):
    kv = pl.program_id(1)
    slabs = 2 * GH // LANES
    load = lambda l, s: x_ref[0, pl.ds(l * slabs + kv * (slabs // 2) + s, n, stride=CMP_STRIDE * slabs), :]
    _compress_core(load, n, pe_ref, w1_ref, w2_ref, kng_ref, o_ref, kv == 0)


def _compress_weights(pe, w1, w2):
    hid = w1.shape[-1]
    w1r = w1.reshape(2, CMP_BLOCK, HEAD_DIM, hid)
    pair = jnp.concatenate([w1r[:, :CMP_STRIDE], w1r[:, CMP_STRIDE:]], axis=-1)
    w1c = pair.reshape(2, CMP_STRIDE // STACK, STACK * HEAD_DIM, 2 * hid).astype(BF16)
    return pe.reshape(2, 2 * CMP_STRIDE // STACK, STACK * HEAD_DIM), w1c, w2.astype(BF16)


def _compress_p_call(cmp_rows, pe, w1, w2, kn_g):
    B, T, _ = cmp_rows.shape
    n = T // CMP_STRIDE
    pe, w1r, w2r = _compress_weights(pe, w1, w2)
    hid = w1.shape[-1]
    slabs = 2 * GH // LANES
    cmp_rows = cmp_rows.reshape(B, T * slabs, LANES)
    return pl.pallas_call(
        functools.partial(_compress_p_kernel, n=n),
        grid=(B, 2),
        in_specs=[pl.BlockSpec((1, T * slabs, LANES), lambda b, kv: (b, 0, 0)),
                  pl.BlockSpec((1,) + pe.shape[1:], lambda b, kv: (kv, 0, 0)),
                  pl.BlockSpec((1,) + w1r.shape[1:], lambda b, kv: (kv, 0, 0, 0)),
                  pl.BlockSpec((1, hid, HEAD_DIM), lambda b, kv: (kv, 0, 0)),
                  _full_spec((1, HEAD_DIM))],
        out_specs=pl.BlockSpec((1, 1, n, GH), lambda b, kv: (b, kv, 0, 0)),
        out_shape=jax.ShapeDtypeStruct((B, 2, n, GH), BF16),
        compiler_params=_params("arbitrary", "arbitrary"),
        name="compress_prompt",
    )(cmp_rows, pe, w1r, w2r, kn_g.reshape(1, HEAD_DIM))


def _page_copy(cache_ref, buf_ref, sem_ref, page, slot, p, col, width, page_size):
    return pltpu.make_async_copy(
        cache_ref.at[page, :, pl.ds(col, width)],
        buf_ref.at[slot, pl.ds(p * page_size, page_size), :],
        sem_ref.at[slot])


def _compress_s_kernel(pt_ref, cache_ref, pe_ref, w1_ref, w2_ref, kng_ref, o_ref, buf_ref, sem_ref,
                       *, n, n_pages, page_size, slab0):
    b = pl.program_id(0)
    kv = pl.program_id(1)
    step = b * 2 + kv
    n_steps = pl.num_programs(0) * 2
    nsl = GH // LANES

    def fetch(s, start):
        sb = s // 2
        slot = s % 2
        for p in range(n_pages):
            for sl in range(nsl):
                cp = pltpu.make_async_copy(
                    cache_ref.at[pt_ref[sb, p], :, slab0 + (s % 2) * nsl + sl, :],
                    buf_ref.at[slot, sl, pl.ds(p * page_size, page_size), :],
                    sem_ref.at[slot])
                cp.start() if start else cp.wait()

    @pl.when(step == 0)
    def _():
        fetch(step, True)

    @pl.when(step + 1 < n_steps)
    def _():
        fetch(step + 1, True)

    fetch(step, False)
    slot = step % 2
    load = lambda l, s: buf_ref[slot, s, pl.ds(l, n, stride=CMP_STRIDE), :]
    _compress_core(load, n, pe_ref, w1_ref, w2_ref, kng_ref, o_ref, kv == 0)


def _compress_s_call(cache3, page_table, layer, pe, w1, w2, kn_g):
    Bs, n_pages = page_table.shape
    page_size = cache3.shape[1]
    past = n_pages * page_size
    n = past // CMP_STRIDE
    pe, w1r, w2r = _compress_weights(pe, w1, w2)
    hid = w1.shape[-1]
    grid_spec = pltpu.PrefetchScalarGridSpec(
        num_scalar_prefetch=1,
        grid=(Bs, 2),
        in_specs=[pl.BlockSpec(memory_space=pl.ANY),
                  pl.BlockSpec((1,) + pe.shape[1:], lambda b, kv, pt: (kv, 0, 0)),
                  pl.BlockSpec((1,) + w1r.shape[1:], lambda b, kv, pt: (kv, 0, 0, 0)),
                  pl.BlockSpec((1, hid, HEAD_DIM), lambda b, kv, pt: (kv, 0, 0)),
                  pl.BlockSpec((1, HEAD_DIM), lambda b, kv, pt: (0, 0))],
        out_specs=pl.BlockSpec((1, 1, n, GH), lambda b, kv, pt: (b, kv, 0, 0)),
        scratch_shapes=[pltpu.VMEM((2, GH // LANES, past, LANES), F32), pltpu.SemaphoreType.DMA((2,))],
    )
    cache4 = cache3.reshape(cache3.shape[0], page_size, -1, LANES)
    return pl.pallas_call(
        functools.partial(_compress_s_kernel, n=n, n_pages=n_pages, page_size=page_size,
                          slab0=layer * 2 * GH // LANES),
        grid_spec=grid_spec,
        out_shape=jax.ShapeDtypeStruct((Bs, 2, n, GH), BF16),
        compiler_params=_params("arbitrary", "arbitrary"),
        name="compress_sample",
    )(page_table, cache4, pe, w1r, w2r, kn_g.reshape(1, HEAD_DIM))


def _select(imp, tpos, nsb):
    j = lax.broadcasted_iota(jnp.int32, imp.shape, 1)
    cur = tpos // SEL_BLOCK
    valid = (j * SEL_BLOCK <= tpos) & (j < nsb)
    forced = (j == 0) | (j == cur) | (j == cur - 1)
    impf = jnp.where(valid, jnp.where(forced, FORCE, imp), NEG)
    rank = jnp.zeros(imp.shape, jnp.int32)
    for jp in range(nsb):
        c = impf[:, jp:jp + 1]
        beats = (c > impf) | ((c == impf) & (jp < j))
        rank = rank + beats.astype(jnp.int32)
    return (rank < min(N_SEL, nsb)) & valid


def _select_t(imp, tpos, nsb):
    j = lax.broadcasted_iota(jnp.int32, imp.shape, 0)
    cur = tpos // SEL_BLOCK
    valid = (j * SEL_BLOCK <= tpos) & (j < nsb)
    forced = (j == 0) | (j == cur) | (j == cur - 1)
    impf = jnp.where(valid, jnp.where(forced, FORCE, imp), NEG)
    rank = jnp.zeros(imp.shape, jnp.int32)
    for jp in range(nsb):
        c = impf[jp:jp + 1, :]
        beats = (c > impf) | ((c == impf) & (jp < j))
        rank = rank + beats.astype(jnp.int32)
    return (rank < min(N_SEL, nsb)) & valid


def _overlap_np(n_cmp, nsb, rows, cols):
    cs = np.arange(n_cmp)[:, None] * CMP_STRIDE
    ss = np.arange(nsb)[None, :] * SEL_BLOCK
    ov = np.clip(np.minimum(cs + CMP_BLOCK, ss + SEL_BLOCK) - np.maximum(cs, ss), 0, None).astype(np.float32) / CMP_STRIDE
    out = np.zeros((rows, cols), np.float32)
    out[:n_cmp, :nsb] = ov
    return out


def _stacked_qt(qt, gi):
    return jnp.concatenate([qt[(gi * HPG + p) * HEAD_DIM:(gi * HPG + p + 1) * HEAD_DIM] for p in range(HPG)],
                           axis=1).astype(BF16)


def _cmp_p_kernel(rb_ref, q_ref, kv_ref, ovt_ref, gt_ref, o_ref, sel_ref, bias_ref, ot_ref, *, n_cmp, nsb):
    qb = pl.program_id(0)
    npad = kv_ref.shape[2]
    cols = HPG * TQ

    @pl.when(pl.program_id(1) == 0)
    def _():
        n = lax.broadcasted_iota(jnp.int32, (npad, TQ), 0)
        t = qb * TQ + lax.broadcasted_iota(jnp.int32, (npad, TQ), 1)
        d = t - (n * CMP_STRIDE + CMP_BLOCK - 1)
        for h in range(C_HEADS):
            bias_ref[:, h * TQ:(h + 1) * TQ] = _bias_lut(d, rb_ref, h)

    qt = q_ref[0].astype(F32).T
    n4 = lax.broadcasted_iota(jnp.int32, (npad, cols), 0)
    t4 = qb * TQ + (lax.broadcasted_iota(jnp.int32, (npad, cols), 1) & (TQ - 1))
    mask = ((t4 - (n4 * CMP_STRIDE + CMP_BLOCK - 1)) >= 0) & (n4 < n_cmp)
    kc = kv_ref[0, 0]
    vct = kv_ref[0, 1].astype(F32).T
    tpos = qb * TQ + lax.broadcasted_iota(jnp.int32, (nsb, TQ), 1)
    for gi in range(KV_GROUPS):
        s = _dot(kc[:, gi * HEAD_DIM:(gi + 1) * HEAD_DIM], _stacked_qt(qt, gi)) + bias_ref[:, gi * cols:(gi + 1) * cols]
        s = jnp.where(mask, s, NEG)
        m = jnp.max(s, axis=0, keepdims=True)
        e = jnp.where(mask, jnp.exp(s - m), 0.0)
        prob = e * (1.0 / jnp.maximum(jnp.sum(e, axis=0, keepdims=True), 1e-30))
        o = _dot(vct[gi * HEAD_DIM:(gi + 1) * HEAD_DIM].astype(BF16), prob.astype(BF16))
        psum = prob[:, 0:TQ]
        for p in range(HPG):
            h = gi * HPG + p
            ot_ref[h * HEAD_DIM:(h + 1) * HEAD_DIM, :] = o[:, p * TQ:(p + 1) * TQ] * gt_ref[0, h:h + 1, :]
            if p:
                psum = psum + prob[:, p * TQ:(p + 1) * TQ]
        hi, lo = _split_bf16(psum)
        imp = _dot(ovt_ref[...], hi) + _dot(ovt_ref[...], lo)
        sel_ref[0, gi] = _select_t(imp, tpos, nsb).astype(F32)
    o_ref[0] = ot_ref[...].T


def _cmp_p_call(qn, kvc, rel_bias, gates_t, n_cmp):
    B, T, nq = qn.shape
    npad = kvc.shape[2]
    nsb = -(-T // SEL_BLOCK)
    assert nsb % 8 == 0
    ovt = jnp.asarray(_overlap_np(n_cmp, nsb, npad, nsb).T, BF16)
    return pl.pallas_call(
        functools.partial(_cmp_p_kernel, n_cmp=n_cmp, nsb=nsb),
        grid=(T // TQ, B),
        in_specs=[_SMEM_SPEC,
                  pl.BlockSpec((1, TQ, nq), lambda i, b: (b, i, 0)),
                  pl.BlockSpec((1, 2, npad, GH), lambda i, b: (b, 0, 0, 0)),
                  _full_spec(ovt.shape),
                  pl.BlockSpec((1, LANES, TQ), lambda i, b: (b, 0, i))],
        out_specs=[pl.BlockSpec((1, TQ, nq), lambda i, b: (b, i, 0)),
                   pl.BlockSpec((1, KV_GROUPS, nsb, TQ), lambda i, b: (b, 0, 0, i))],
        out_shape=[jax.ShapeDtypeStruct((B, T, nq), F32),
                   jax.ShapeDtypeStruct((B, KV_GROUPS, nsb, T), F32)],
        scratch_shapes=[pltpu.VMEM((npad, C_HEADS * TQ), F32), pltpu.VMEM((nq, TQ), F32)],
        compiler_params=_params("arbitrary", "arbitrary"),
        name="cmp_attn_prompt",
    )(rel_bias, qn, kvc, ovt, gates_t)


ONES_ROWS = 16
N_BIAS_TILES = 4


def _pattn_kernel(qb_ref, kb_ref, fl_ref, rb_ref, q_ref, k_ref, v_ref, *rest, mode, gate_row0):
    if mode == "sel":
        sel_ref, gt_ref, o_ref, qst_ref, m_ref, acc_ref, bias_ref, ot_ref = rest
    else:
        gt_ref, o_ref, qst_ref, m_ref, acc_ref, bias_ref, ot_ref = rest
    i = pl.program_id(1)
    qb = qb_ref[i]
    kb = kb_ref[i]
    cols = HPG * TQ

    @pl.when((pl.program_id(0) == 0) & (i == 0))
    def _():
        row = lax.broadcasted_iota(jnp.int32, (TK, TQ), 0)
        col = lax.broadcasted_iota(jnp.int32, (TK, TQ), 1)
        for h in range(C_HEADS):
            for oi in range(N_BIAS_TILES - 1):
                bias_ref[oi, :, h * TQ:(h + 1) * TQ] = _bias_lut(oi * TQ + col - row, rb_ref, h)
            bias_ref[N_BIAS_TILES - 1, :, h * TQ:(h + 1) * TQ] = jnp.full((TK, TQ), rb_ref[N_BUCKETS - 1, h], F32)

    @pl.when((fl_ref[i] & 1) == 1)
    def _():
        qt = q_ref[0].astype(F32).T
        for gi in range(KV_GROUPS):
            qst_ref[gi] = _stacked_qt(qt, gi)
        m_ref[...] = jnp.full(m_ref.shape, NEG, F32)
        acc_ref[...] = jnp.zeros(acc_ref.shape, F32)

    off = qb * TQ - kb * TK
    oi = jnp.minimum(off // TQ, N_BIAS_TILES - 1)
    k = k_ref[0]
    vt = v_ref[0].T
    row = lax.broadcasted_iota(jnp.int32, (TK, cols), 0)
    tq = lax.broadcasted_iota(jnp.int32, (TK, cols), 1) & (TQ - 1)
    d = off + tq - row
    base = d >= 0
    if mode == "win":
        base = base & (d < WINDOW)
    ones = jnp.ones((ONES_ROWS, TK), BF16)
    for gi in range(KV_GROUPS):
        mask = base
        if mode == "sel":
            sm = sel_ref[0, gi, 0]
            blk = jnp.concatenate([jnp.broadcast_to(sm[r:r + 1], (SEL_BLOCK, TQ)) for r in range(TK // SEL_BLOCK)], axis=0)
            mask = base & (jnp.concatenate([blk] * HPG, axis=1) > 0.5)
        s = _dot(k[:, gi * HEAD_DIM:(gi + 1) * HEAD_DIM], qst_ref[gi]) + bias_ref[oi, :, gi * cols:(gi + 1) * cols]
        s = jnp.where(mask, s, NEG)
        m_old = m_ref[gi]
        m_new = jnp.maximum(m_old, jnp.max(s, axis=0, keepdims=True))
        e = jnp.exp(s - m_new).astype(BF16)
        alpha = jnp.exp(m_old - m_new)
        vext = jnp.concatenate([vt[gi * HEAD_DIM:(gi + 1) * HEAD_DIM].astype(BF16), ones], axis=0)
        acc_ref[gi] = alpha * acc_ref[gi] + _dot(vext, e)
        m_ref[gi] = m_new

    @pl.when((fl_ref[i] & 2) == 2)
    def _():
        for gi in range(KV_GROUPS):
            a = acc_ref[gi]
            o = a[0:HEAD_DIM] * (1.0 / jnp.maximum(a[HEAD_DIM:HEAD_DIM + 1], 1e-30))
            for p in range(HPG):
                h = gi * HPG + p
                ot_ref[h * HEAD_DIM:(h + 1) * HEAD_DIM, :] = (
                    o[:, p * TQ:(p + 1) * TQ] * gt_ref[0, gate_row0 + h:gate_row0 + h + 1, :])
        o_ref[0] = ot_ref[...].T


def _pair_tables(nq, mode):
    qbs, kbs, fls = [], [], []
    for qb in range(nq):
        hi = (qb * TQ + TQ - 1) // TK
        lo = 0 if mode == "sel" else max((qb * TQ - WINDOW + 1) // TK, 0)
        for kb in range(lo, hi + 1):
            qbs.append(qb)
            kbs.append(kb)
            fls.append((1 if kb == lo else 0) | (2 if kb == hi else 0))
    return tuple(jnp.asarray(np.array(a, np.int32)) for a in (qbs, kbs, fls))


def _pattn_call(qn, kn, rows, selmask, gates_t, rel_bias, mode):
    B, T, nq_w = qn.shape
    assert T % TK == 0 and TK % TQ == 0 and (N_BIAS_TILES - 1) * TQ - (TK - 1) >= MAX_DIST
    qbs, kbs, fls = _pair_tables(T // TQ, mode)
    in_specs = [_SMEM_SPEC,
                pl.BlockSpec((1, TQ, nq_w), lambda b, i, qb, kb, fl: (b, qb[i], 0)),
                pl.BlockSpec((1, TK, GH), lambda b, i, qb, kb, fl: (b, kb[i], 0)),
                pl.BlockSpec((1, TK, GH), lambda b, i, qb, kb, fl: (b, kb[i], 1))]
    args = [rel_bias, qn, kn, rows]
    if mode == "sel":
        nblk = TK // SEL_BLOCK
        in_specs.append(pl.BlockSpec((1, KV_GROUPS, 1, nblk, TQ), lambda b, i, qb, kb, fl: (b, 0, kb[i], 0, qb[i])))
        args.append(selmask.reshape(B, KV_GROUPS, T // TK, nblk, T))
    in_specs.append(pl.BlockSpec((1, LANES, TQ), lambda b, i, qb, kb, fl: (b, 0, qb[i])))
    args.append(gates_t)
    cols = HPG * TQ
    grid_spec = pltpu.PrefetchScalarGridSpec(
        num_scalar_prefetch=3,
        grid=(B, int(qbs.shape[0])),
        in_specs=in_specs,
        out_specs=pl.BlockSpec((1, TQ, nq_w), lambda b, i, qb, kb, fl: (b, qb[i], 0)),
        scratch_shapes=[pltpu.VMEM((KV_GROUPS, HEAD_DIM, cols), BF16),
                        pltpu.VMEM((KV_GROUPS, 1, cols), F32),
                        pltpu.VMEM((KV_GROUPS, HEAD_DIM + ONES_ROWS, cols), F32),
                        pltpu.VMEM((N_BIAS_TILES, TK, C_HEADS * TQ), F32),
                        pltpu.VMEM((nq_w, TQ), F32)],
    )
    return pl.pallas_call(
        functools.partial(_pattn_kernel, mode=mode, gate_row0=C_HEADS * (1 if mode == "sel" else 2)),
        grid_spec=grid_spec,
        out_shape=jax.ShapeDtypeStruct((B, T, nq_w), F32),
        compiler_params=_params("arbitrary", "arbitrary"),
        name="attn_prompt_" + mode,
    )(qbs, kbs, fls, *args)


def _combine_kernel(x_ref, gt_ref, oc_ref, os_ref, ow_ref, gate_ref, w_ref, o_ref, mix_ref):
    gate = gate_ref[0]
    for h in range(C_HEADS):
        hs = slice(h * HEAD_DIM, (h + 1) * HEAD_DIM)
        mix_ref[:, hs] = (gate[:, h:h + 1] * oc_ref[0, :, hs]
                          + gate[:, C_HEADS + h:C_HEADS + h + 1] * os_ref[0, :, hs]
                          + gate[:, 2 * C_HEADS + h:2 * C_HEADS + h + 1] * ow_ref[0, :, hs]).astype(BF16)
    o_ref[0] = x_ref[0] + gt_ref[0] * _dot(mix_ref[...], w_ref[...])


def _combine_call(x3, gt, oc, os_, ow, gates, w_out, tm):
    NB, R, D = x3.shape
    nq = C_HEADS * HEAD_DIM
    return pl.pallas_call(
        _combine_kernel,
        grid=(NB, R // tm),
        in_specs=[_tok_spec(tm, D), _mod_spec(gt, tm), _tok_spec(tm, nq), _tok_spec(tm, nq), _tok_spec(tm, nq),
                  _tok_spec(tm, LANES), _full_spec(w_out.shape)],
        out_specs=_tok_spec(tm, D),
        out_shape=jax.ShapeDtypeStruct(x3.shape, F32),
        scratch_shapes=[pltpu.VMEM((tm, nq), BF16)],
        compiler_params=_params("arbitrary", "arbitrary"),
        name="nsa_combine",
    )(x3, gt, oc, os_, ow, gates, w_out)


def _sum_proj_kernel(x_ref, gt_ref, oc_ref, os_ref, ow_ref, w_ref, o_ref):
    mix = (oc_ref[0] + os_ref[0] + ow_ref[0]).astype(BF16)
    o_ref[0] = x_ref[0] + gt_ref[0] * _dot(mix, w_ref[...])


def _sum_proj_call(x3, gt, oc, os_, ow, w_out, tm):
    NB, R, D = x3.shape
    nq = C_HEADS * HEAD_DIM
    return pl.pallas_call(
        _sum_proj_kernel,
        grid=(NB, R // tm),
        in_specs=[_tok_spec(tm, D), _mod_spec(gt, tm), _tok_spec(tm, nq), _tok_spec(tm, nq), _tok_spec(tm, nq),
                  _full_spec(w_out.shape)],
        out_specs=_tok_spec(tm, D),
        out_shape=jax.ShapeDtypeStruct(x3.shape, F32),
        compiler_params=_params("arbitrary", "arbitrary"),
        name="nsa_out_proj",
    )(x3, gt, oc, os_, ow, w_out)


def _lut_kernel(rb_ref, d_ref, h_ref, o_ref):
    d = d_ref[...]
    hh = h_ref[...]
    out = jnp.zeros(d.shape, F32)
    for h in range(C_HEADS):
        out = jnp.where(hh == h, _bias_lut(d, rb_ref, h), out)
    o_ref[...] = out


def _lut_call(rel_bias, dist, head):
    dist = np.ascontiguousarray(np.broadcast_to(dist, head.shape)).astype(np.int32)
    head = np.ascontiguousarray(head).astype(np.int32)
    return pl.pallas_call(
        _lut_kernel,
        grid=(1,),
        in_specs=[_SMEM_SPEC, _full_spec(dist.shape), _full_spec(head.shape)],
        out_specs=_full_spec(dist.shape),
        out_shape=jax.ShapeDtypeStruct(dist.shape, F32),
        compiler_params=_params("arbitrary"),
        name="bias_lut",
    )(rel_bias, jnp.asarray(dist), jnp.asarray(head))


def _cmp_s_kernel(q_ref, kv_ref, bias_ref, ov_ref, o_ref, sel_ref, *, n_cmp, nsb, ts, past):
    npad = kv_ref.shape[2]
    rows = HPG * ts
    t = past + lax.broadcasted_iota(jnp.int32, (rows, npad), 0) % ts
    nidx = lax.broadcasted_iota(jnp.int32, (rows, npad), 1)
    mask = ((t - (nidx * CMP_STRIDE + CMP_BLOCK - 1)) >= 0) & (nidx < n_cmp)
    tpos = past + lax.broadcasted_iota(jnp.int32, (ts, 1), 0)
    for gi in range(KV_GROUPS):
        kc = kv_ref[0, 0, :, gi * HEAD_DIM:(gi + 1) * HEAD_DIM]
        vc = kv_ref[0, 1, :, gi * HEAD_DIM:(gi + 1) * HEAD_DIM]
        s = _dot_nt(q_ref[0, gi], kc) + bias_ref[gi]
        s = jnp.where(mask, s, NEG)
        m = jnp.max(s, axis=-1, keepdims=True)
        e = jnp.where(mask, jnp.exp(s - m), 0.0)
        prob = e / jnp.maximum(jnp.sum(e, axis=-1, keepdims=True), 1e-30)
        o_ref[0, gi] = _dot(prob.astype(BF16), vc)
        hi, lo = _split_bf16(prob)
        imp16 = _dot(hi, ov_ref[...]) + _dot(lo, ov_ref[...])
        imp = imp16[0:ts]
        for p in range(1, HPG):
            imp = imp + imp16[p * ts:(p + 1) * ts]
        sel_ref[0, gi] = _select(imp, tpos, nsb).astype(F32)


def _cmp_s_bias(rel_bias, npad, ts, past):
    rows = HPG * ts
    r = np.arange(KV_GROUPS * rows)
    dist = (past + r % ts)[:, None] - (np.arange(npad)[None, :] * CMP_STRIDE + CMP_BLOCK - 1)
    head = np.broadcast_to((r // ts)[:, None], dist.shape)
    return _lut_call(rel_bias, dist, head).reshape(KV_GROUPS, rows, npad)


def _cmp_s_call(q2, kvc, bias, n_cmp, nsb, ts, past):
    Bs = q2.shape[0]
    npad = kvc.shape[2]
    rows = HPG * ts
    lpad = -(-nsb // LANES) * LANES
    ov = jnp.asarray(_overlap_np(n_cmp, nsb, npad, lpad), BF16)
    return pl.pallas_call(
        functools.partial(_cmp_s_kernel, n_cmp=n_cmp, nsb=nsb, ts=ts, past=past),
        grid=(Bs,),
        in_specs=[pl.BlockSpec((1, KV_GROUPS, rows, HEAD_DIM), lambda b: (b, 0, 0, 0)),
                  pl.BlockSpec((1, 2, npad, GH), lambda b: (b, 0, 0, 0)),
                  _full_spec(bias.shape), _full_spec(ov.shape)],
        out_specs=[pl.BlockSpec((1, KV_GROUPS, rows, HEAD_DIM), lambda b: (b, 0, 0, 0)),
                   pl.BlockSpec((1, KV_GROUPS, ts, lpad), lambda b: (b, 0, 0, 0))],
        out_shape=[jax.ShapeDtypeStruct((Bs, KV_GROUPS, rows, HEAD_DIM), F32),
                   jax.ShapeDtypeStruct((Bs, KV_GROUPS, ts, lpad), F32)],
        compiler_params=_params("arbitrary"),
        name="cmp_attn_sample",
    )(q2, kvc, bias, ov)


def _decode_core(kp, vp, kn, vn, qblk, kng_col, eexp, bias_p, bias_n, mask_p, mask_n):
    qb = (qblk * kng_col).astype(BF16)

    def logits(k, bias):
        hi, lo = _split_bf16(k * k)
        ss = _dot(hi, eexp) + _dot(lo, eexp)
        r = lax.rsqrt(ss * (1.0 / HEAD_DIM) + EPS)
        return _dot(k.astype(BF16), qb) * r + bias

    lp = jnp.where(mask_p, logits(kp, bias_p), NEG)
    ln = jnp.where(mask_n, logits(kn, bias_n), NEG)
    m = jnp.maximum(jnp.max(lp, axis=0, keepdims=True), jnp.max(ln, axis=0, keepdims=True))
    ep = jnp.where(mask_p, jnp.exp(lp - m), 0.0)
    en = jnp.where(mask_n, jnp.exp(ln - m), 0.0)
    denom = jnp.sum(ep, axis=0, keepdims=True) + jnp.sum(en, axis=0, keepdims=True)
    inv = 1.0 / jnp.maximum(denom, 1e-30)
    of = _dot_tn((ep * inv).astype(BF16), vp.astype(BF16)) + _dot_tn((en * inv).astype(BF16), vn.astype(BF16))
    ncol = of.shape[0]
    per = ncol // KV_GROUPS
    rg = lax.broadcasted_iota(jnp.int32, (ncol, HEAD_DIM), 0) // per
    o = jnp.zeros((ncol, HEAD_DIM), F32)
    for gi in range(KV_GROUPS):
        o = o + jnp.where(rg == gi, of[:, gi * HEAD_DIM:(gi + 1) * HEAD_DIM], 0.0)
    return o


def _new_key_mask(ts, ncol, rows):
    jn = lax.broadcasted_iota(jnp.int32, (rows, ncol), 0)
    tn = lax.broadcasted_iota(jnp.int32, (rows, ncol), 1) % ts
    return (jn <= tn) & (jn < ts)


def _sel_s_kernel(pt_ref, cache_ref, new_ref, q_ref, kng_ref, eexp_ref, bp_ref, bn_ref, selp_ref, seln_ref,
                  o_ref, buf_ref, sem_ref, *, n_pages, page_size, col0, ts):
    b = pl.program_id(0)
    nb = pl.num_programs(0)
    past = n_pages * page_size
    ncol = q_ref.shape[2]

    def fetch(sb, start):
        slot = sb % 2
        for p in range(n_pages):
            cp = _page_copy(cache_ref, buf_ref, sem_ref, pt_ref[sb, p], slot, p, col0, 2 * GH, page_size)
            cp.start() if start else cp.wait()

    @pl.when(b == 0)
    def _():
        fetch(b, True)

    @pl.when(b + 1 < nb)
    def _():
        fetch(b + 1, True)

    fetch(b, False)
    slot = b % 2
    kp = buf_ref[slot, :, 0:GH]
    vp = buf_ref[slot, :, GH:2 * GH]
    nblk = past // SEL_BLOCK
    mask_p = jnp.broadcast_to(selp_ref[0], (nblk, SEL_BLOCK, ncol)).reshape(past, ncol) > 0.5
    mask_n = _new_key_mask(ts, ncol, new_ref.shape[1]) & (seln_ref[0] > 0.5)
    o_ref[0] = _decode_core(kp, vp, new_ref[0, :, 0:GH], new_ref[0, :, GH:2 * GH], q_ref[0], kng_ref[...],
                            eexp_ref[...], bp_ref[...], bn_ref[...], mask_p, mask_n)


def _win_s_kernel(st_ref, new_ref, q_ref, kng_ref, eexp_ref, bp_ref, bn_ref, o_ref, nw_ref, *, ts, kpos0):
    wb = st_ref.shape[2]
    ncol = q_ref.shape[2]
    jp = lax.broadcasted_iota(jnp.int32, (wb, ncol), 0)
    tp = lax.broadcasted_iota(jnp.int32, (wb, ncol), 1) % ts
    dist = wb + tp - jp
    mask_p = (dist >= 0) & (dist < WINDOW) & (kpos0 + jp >= 0)
    mask_n = _new_key_mask(ts, ncol, new_ref.shape[1])
    st = st_ref[0, 0]
    o_ref[0] = _decode_core(st[:, 0:GH], st[:, GH:2 * GH], new_ref[0, :, 0:GH], new_ref[0, :, GH:2 * GH],
                            q_ref[0], kng_ref[...], eexp_ref[...], bp_ref[...], bn_ref[...], mask_p, mask_n)
    wout = nw_ref.shape[2]
    keep = wout - ts
    nw_ref[0, 0, 0:keep, :] = st_ref[0, 0, pl.ds(wb - keep, keep), :]
    nw_ref[0, 0, keep:wout, :] = new_ref[0, 0:ts, :]


def _decode_bias(rel_bias, ts, key_dist):
    ncol = C_HEADS * ts
    dist = key_dist[:, np.arange(ncol) % ts]
    head = np.broadcast_to((np.arange(ncol) // ts)[None, :], dist.shape)
    return _lut_call(rel_bias, dist, head)


def _sample_bias_tables(rel_bias, ts, past, wb, npad):
    tq = np.arange(ts)[None, :]
    tail = 2 * MAX_DIST
    assert past >= tail and tail - ts >= MAX_DIST
    sel_tail = _decode_bias(rel_bias, ts, tail + tq - np.arange(tail)[:, None])
    sel_past = jnp.concatenate([jnp.broadcast_to(sel_tail[0:1], (past - tail, sel_tail.shape[1])), sel_tail], axis=0)
    return dict(
        new=_decode_bias(rel_bias, ts, tq - np.arange(8)[:, None]),
        sel=sel_past,
        win=_decode_bias(rel_bias, ts, wb + tq - np.arange(wb)[:, None]),
        cmp=_cmp_s_bias(rel_bias, npad, ts, past))


def _sel_s_call(cache3, page_table, layer, new_rows, qblk, kng_col, eexp, bias_p, bias_n, selp, seln, ts):
    Bs, n_pages = page_table.shape
    page_size = cache3.shape[1]
    past = n_pages * page_size
    ncol = qblk.shape[2]
    nblk = past // SEL_BLOCK
    m3 = lambda b, pt: (b, 0, 0)
    c2 = lambda b, pt: (0, 0)
    grid_spec = pltpu.PrefetchScalarGridSpec(
        num_scalar_prefetch=1,
        grid=(Bs,),
        in_specs=[pl.BlockSpec(memory_space=pl.ANY),
                  pl.BlockSpec((1,) + new_rows.shape[1:], m3),
                  pl.BlockSpec((1, GH, ncol), m3),
                  pl.BlockSpec(kng_col.shape, c2), pl.BlockSpec(eexp.shape, c2),
                  pl.BlockSpec(bias_p.shape, c2), pl.BlockSpec(bias_n.shape, c2),
                  pl.BlockSpec((1, nblk, 1, ncol), lambda b, pt: (b, 0, 0, 0)),
                  pl.BlockSpec((1, 1, ncol), m3)],
        out_specs=pl.BlockSpec((1, ncol, HEAD_DIM), m3),
        scratch_shapes=[pltpu.VMEM((2, past, 2 * GH), F32), pltpu.SemaphoreType.DMA((2,))],
    )
    return pl.pallas_call(
        functools.partial(_sel_s_kernel, n_pages=n_pages, page_size=page_size, col0=layer * 2 * GH, ts=ts),
        grid_spec=grid_spec,
        out_shape=jax.ShapeDtypeStruct((Bs, ncol, HEAD_DIM), F32),
        compiler_params=_params("arbitrary"),
        name="sel_attn_sample",
    )(page_table, cache3, new_rows, qblk, kng_col, eexp, bias_p, bias_n, selp, seln)


def _win_s_call(state4, layer, new_rows, qblk, kng_col, eexp, bias_p, bias_n, ts, past):
    n_l, Bs, wb, _ = state4.shape
    ncol = qblk.shape[2]
    wout = min(WINDOW, wb + ts)
    m3 = lambda b: (b, 0, 0)
    c2 = lambda b: (0, 0)
    return pl.pallas_call(
        functools.partial(_win_s_kernel, ts=ts, kpos0=past - wb),
        grid=(Bs,),
        in_specs=[pl.BlockSpec((1, 1, wb, 2 * GH), lambda b: (layer, b, 0, 0)),
                  pl.BlockSpec((1,) + new_rows.shape[1:], m3),
                  pl.BlockSpec((1, GH, ncol), m3),
                  pl.BlockSpec(kng_col.shape, c2), pl.BlockSpec(eexp.shape, c2),
                  pl.BlockSpec(bias_p.shape, c2), pl.BlockSpec(bias_n.shape, c2)],
        out_specs=[pl.BlockSpec((1, ncol, HEAD_DIM), m3),
                   pl.BlockSpec((1, 1, wout, 2 * GH), lambda b: (0, b, 0, 0))],
        out_shape=[jax.ShapeDtypeStruct((Bs, ncol, HEAD_DIM), F32),
                   jax.ShapeDtypeStruct((1, Bs, wout, 2 * GH), F32)],
        compiler_params=_params("arbitrary"),
        name="win_attn_sample",
    )(state4, new_rows, qblk, kng_col, eexp, bias_p, bias_n)


def _nsa_prompt(x, mods, g, w_in_pad, w_out, qn_g, kn_g, pe, w1, w2, rel_bias, tm):
    B, T, D = x.shape
    N = B * T
    nq = C_HEADS * HEAD_DIM
    q, cmp_rows, sel_rows, win_rows, _, gates_t = _nsa_proj_call(x, mods[3], mods[4], g, w_in_pad, tm)
    tr = min(2048, N)
    assert N % tr == 0
    qn = _headnorm_call(q.reshape(N, nq), qn_g, nq, tr, SCALE).reshape(B, T, nq)
    seln = _headnorm_call(sel_rows.reshape(N, 2 * GH), kn_g, GH, tr).reshape(B, T, GH)
    winn = _headnorm_call(win_rows.reshape(N, 2 * GH), kn_g, GH, tr).reshape(B, T, GH)
    n_cmp = (T - CMP_BLOCK) // CMP_STRIDE + 1
    kvc = _compress_p_call(cmp_rows, pe, w1, w2, kn_g)
    o_cmp, selmask = _cmp_p_call(qn, kvc, rel_bias, gates_t, n_cmp)
    o_sel = _pattn_call(qn, seln, sel_rows, selmask, gates_t, rel_bias, "sel")
    o_win = _pattn_call(qn, winn, win_rows, None, gates_t, rel_bias, "win")
    x = _sum_proj_call(x, mods[5], o_cmp, o_sel, o_win, w_out, tm)
    wk = min(WINDOW, T)
    return x, cmp_rows, sel_rows, win_rows[:, T - wk:]


def _nsa_sample(x, mods, g, w_in_pad, w_out, qn_g, kn_g, pe, w1, w2, bias_tabs, cache_cmp3, cache_sel3,
                page_table, state_win4, layer, ts, bsz):
    R = ts * bsz
    nq = C_HEADS * HEAD_DIM
    past = page_table.shape[1] * cache_cmp3.shape[1]
    assert past % CMP_STRIDE == 0 and past % SEL_BLOCK == 0 and ts <= 8 and ts <= CMP_STRIDE
    q, cmp_rows, sel_rows, win_rows, gates, _ = _nsa_proj_call(x, mods[3], mods[4], g, w_in_pad, R)
    qn = _headnorm_call(q.reshape(R, nq), qn_g, nq, R, SCALE)
    q5 = jnp.transpose(qn.reshape(ts, bsz, KV_GROUPS, HPG, HEAD_DIM), (1, 2, 3, 0, 4))
    q2 = q5.reshape(bsz, KV_GROUPS, HPG * ts, HEAD_DIM)
    qd = jnp.transpose(q5.astype(F32).reshape(bsz, KV_GROUPS, HPG * ts, HEAD_DIM), (0, 1, 3, 2))
    qblk = (qd[:, :, :, None, :] * jnp.eye(KV_GROUPS, dtype=F32)[None, :, None, :, None]).reshape(bsz, GH, C_HEADS * ts)
    ncol = C_HEADS * ts
    tk = past + ts
    n_cmp = (tk - CMP_BLOCK) // CMP_STRIDE + 1
    assert n_cmp <= past // CMP_STRIDE - 1 + 1 and (n_cmp - 1) * CMP_STRIDE + CMP_BLOCK <= past
    nsb = -(-tk // SEL_BLOCK)
    kvc = _compress_s_call(cache_cmp3, page_table, layer, pe, w1, w2, kn_g)
    o_cmp, selw = _cmp_s_call(q2, kvc, bias_tabs["cmp"], n_cmp, nsb, ts, past)
    selt = jnp.transpose(selw, (0, 3, 1, 2))
    selt = jnp.broadcast_to(selt[:, :, :, None, :], selt.shape[:3] + (HPG, ts)).reshape(bsz, -1, 1, ncol)
    nblk = past // SEL_BLOCK
    assert nsb == nblk + 1
    selp, seln = selt[:, :nblk], selt[:, nblk]
    to_seq = lambda a: jnp.pad(jnp.transpose(a.reshape(ts, bsz, 2 * GH), (1, 0, 2)), ((0, 0), (0, 8 - ts), (0, 0)))
    kng_col = jnp.tile(kn_g, KV_GROUPS).reshape(GH, 1)
    eexp = np.zeros((GH, ncol), np.float32)
    for gi in range(KV_GROUPS):
        eexp[gi * HEAD_DIM:(gi + 1) * HEAD_DIM, gi * HPG * ts:(gi + 1) * HPG * ts] = 1.0
    eexp = jnp.asarray(eexp, BF16)
    o_sel = _sel_s_call(cache_sel3, page_table, layer, to_seq(sel_rows), qblk, kng_col, eexp, bias_tabs["sel"],
                        bias_tabs["new"], selp, seln, ts)
    o_win, new_win = _win_s_call(state_win4, layer, to_seq(win_rows), qblk, kng_col, eexp, bias_tabs["win"],
                                 bias_tabs["new"], ts, past)
    back = lambda o: jnp.transpose(o.reshape(bsz, KV_GROUPS, HPG, ts, HEAD_DIM), (3, 0, 1, 2, 4)).reshape(1, R, nq)
    x = _combine_call(x, mods[5], back(o_cmp), back(o_sel), back(o_win), gates, w_out, R)
    rows_out = lambda a: jnp.transpose(a.reshape(ts, bsz, 2, KV_GROUPS, HEAD_DIM), (1, 0, 2, 3, 4))
    return x, rows_out(cmp_rows), rows_out(sel_rows), new_win[0].reshape(bsz, -1, 2, KV_GROUPS, HEAD_DIM)


def kernel(x_prompt, x_sample, c_prompt, c_sample, cache_cmp_kv, cache_sel_kv, page_table, state_win_kv, state_conv, ada_w, ada_b, norm_g, ffn_w1, ffn_w2, even_w_in, even_w_out, gmlp_v_g, gmlp_ws, gmlp_bs, conv_w, conv_b, conv_ln_g, conv_ln_b, nsa_w_in, nsa_w_out, q_norm_g, k_norm_g, cmp_pe, cmp_w1, cmp_w2, rel_bias):
    B, T, D = x_prompt.shape
    Bs, Ts, _ = x_sample.shape
    depth = ada_w.shape[0]
    n_odd = nsa_w_in.shape[0]
    tm = min(512, T)
    Rs = Ts * Bs

    ada = _ada_call(jnp.concatenate([c_prompt, c_sample], axis=0), ada_w, ada_b)
    xp = x_prompt
    xs = jnp.transpose(x_sample, (1, 0, 2)).reshape(1, Rs, D)
    n_pool, page_size = cache_cmp_kv.shape[:2]
    cache_cmp3 = cache_cmp_kv.reshape(n_pool, page_size, -1)
    cache_sel3 = cache_sel_kv.reshape(n_pool, page_size, -1)
    state_win4 = state_win_kv.reshape(state_win_kv.shape[:3] + (2 * GH,))
    past = page_table.shape[1] * page_size
    bias_tabs = _sample_bias_tables(rel_bias, Ts, past, state_win_kv.shape[2], past // CMP_STRIDE)
    pad_cols = (-nsa_w_in.shape[2] + C_HEADS * HEAD_DIM + 6 * GH + LANES)

    cmp_p, cmp_s, sel_p, sel_s, win_p, win_s, conv_p, conv_s, v_s = ([] for _ in range(9))
    for l in range(depth):
        mp = [ada[l, :B, k * D:(k + 1) * D].reshape(B, 1, D) for k in range(9)]
        ms = [jnp.tile(ada[l, B:, k * D:(k + 1) * D], (Ts, 1)).reshape(1, Rs, D) for k in range(9)]
        w1 = ffn_w1[l].astype(BF16)
        w2 = ffn_w2[l].astype(BF16)
        xp = _ffn_call(xp, mp[0], mp[1], mp[2], norm_g[l, 0], w1[0], w2[0], tm)
        xs = _ffn_call(xs, ms[0], ms[1], ms[2], norm_g[l, 0], w1[0], w2[0], Rs)
        if l % 2 == 0:
            e = l // 2
            w_in = even_w_in[e].astype(BF16)
            w_out = even_w_out[e].astype(BF16)
            prm = (gmlp_v_g[e], gmlp_ws[e], gmlp_bs[e])
            cprm = (conv_w[e], conv_b[e], conv_ln_g[e], conv_ln_b[e])
            xp, cst_p = _even_call(xp, mp[3], mp[4], mp[5], norm_g[l, 1], w_in, w_out, *prm, *cprm, tm)
            xs, cst_s, vrow = _even_s_call(xs, ms[3], ms[4], ms[5], norm_g[l, 1], w_in, w_out, *prm,
                                           state_conv[e], *cprm, Ts, Bs)
            conv_p.append(cst_p)
            conv_s.append(cst_s)
            v_s.append(vrow)
        else:
            o = l // 2
            w_in_pad = jnp.pad(nsa_w_in[o], ((0, 0), (0, pad_cols))).astype(BF16)
            w_out = nsa_w_out[o].astype(BF16)
            prm = (norm_g[l, 1], w_in_pad, w_out, q_norm_g[o], k_norm_g[o], cmp_pe[o], cmp_w1[o], cmp_w2[o])
            xp, rc_p, rs_p, w_p = _nsa_prompt(xp, mp, *prm, rel_bias, tm)
            xs, rc_s, rs_s, w_s = _nsa_sample(xs, ms, *prm, bias_tabs, cache_cmp3, cache_sel3, page_table, state_win4,
                                              o, Ts, Bs)
            kv6 = lambda a: a.reshape(B, -1, 2, KV_GROUPS, HEAD_DIM)
            cmp_p.append(kv6(rc_p))
            sel_p.append(kv6(rs_p))
            win_p.append(kv6(w_p))
            cmp_s.append(rc_s)
            sel_s.append(rs_s)
            win_s.append(w_s)
        xp = _ffn_call(xp, mp[6], mp[7], mp[8], norm_g[l, 2], w1[1], w2[1], tm)
        xs = _ffn_call(xs, ms[6], ms[7], ms[8], norm_g[l, 2], w1[1], w2[1], Rs)
    y_sample = jnp.transpose(xs.reshape(Ts, Bs, D), (1, 0, 2))
    return (xp, y_sample, jnp.stack(cmp_p, axis=2), jnp.stack(cmp_s, axis=2), jnp.stack(sel_p, axis=2),
            jnp.stack(sel_s, axis=2), jnp.stack(win_p, axis=0), jnp.stack(win_s, axis=0),
            jnp.stack(conv_p, axis=0), jnp.stack(conv_s, axis=0), jnp.stack(v_s, axis=0))
```

```python
import functools
import math

import numpy as np
import jax
import jax.numpy as jnp
from jax import lax
from jax.experimental import pallas as pl
from jax.experimental.pallas import tpu as pltpu

F32 = jnp.float32
BF16 = jnp.bfloat16

A_GROUPS = 8
CHUNK = 128
CONV_W = 31
C_HEADS = 16
KV_GROUPS = 4
HEAD_DIM = 64
HPG = C_HEADS // KV_GROUPS
GH = KV_GROUPS * HEAD_DIM
CMP_BLOCK = 32
CMP_STRIDE = 16
SEL_BLOCK = 64
N_SEL = 16
WINDOW = 512
N_BUCKETS = 32
MAX_DIST = 128
SCALE = HEAD_DIM ** -0.5
EPS = 1e-6
NEG = -1e30
FORCE = 1e6

VMEM_LIMIT_BYTES = 60 * 2 ** 20
LANES = 128
TQ = 128
TK = 256
STACK = 4
HIST = 32


def _params(*sem):
    return pltpu.CompilerParams(dimension_semantics=sem, vmem_limit_bytes=VMEM_LIMIT_BYTES)


def _dot(a, b):
    return jnp.dot(a, b, preferred_element_type=F32)


def _dot_nt(a, b):
    return lax.dot_general(a, b, (((1,), (1,)), ((), ())), preferred_element_type=F32)


def _dot_tn(a, b):
    return lax.dot_general(a, b, (((0,), (0,)), ((), ())), preferred_element_type=F32)


def _split_bf16(x):
    hi = x.astype(BF16)
    lo = (x - hi.astype(F32)).astype(BF16)
    return hi, lo


def _sigmoid(x):
    return 1.0 / (1.0 + jnp.exp(-x))


def _silu(x):
    return x * _sigmoid(x)


def _gelu(x):
    return 0.5 * x * (1.0 + jnp.tanh(math.sqrt(2.0 / math.pi) * (x + 0.044715 * (x * x * x))))


def _modulate(x, g, shift, scale):
    y = x * lax.rsqrt(jnp.mean(x * x, axis=-1, keepdims=True) + EPS)
    return (y * g) * (1.0 + scale) + shift


def _bucket_np(dist):
    exact = N_BUCKETS // 2
    d = np.maximum(dist, 0)
    df = np.maximum(d, 1).astype(np.float32)
    large = exact + (np.log(df / np.float32(exact)) / np.float32(math.log(MAX_DIST / exact))
                     * np.float32(N_BUCKETS - exact)).astype(np.int32)
    return np.where(d < exact, d, np.minimum(large, N_BUCKETS - 1)).astype(np.int32)


def _bucket_thresholds():
    b = _bucket_np(np.arange(4 * MAX_DIST))
    assert (np.diff(b) >= 0).all() and b[MAX_DIST] == N_BUCKETS - 1
    return [int(np.argmax(b >= k)) for k in range(N_BUCKETS)]


_THR = _bucket_thresholds()


def _bias_lut(d, rb_ref, h):
    val = jnp.full(d.shape, rb_ref[0, h], F32)
    for k in range(1, N_BUCKETS):
        val = jnp.where(d >= _THR[k], rb_ref[k, h], val)
    return val


_SMEM_SPEC = pl.BlockSpec(memory_space=pltpu.SMEM)


def _full_spec(shape):
    n = len(shape)
    return pl.BlockSpec(shape, lambda *_: (0,) * n)


def _tok_spec(tm, width):
    return pl.BlockSpec((1, tm, width), lambda b, i: (b, i, 0))


def _mod_spec(mod, tm):
    if mod.shape[1] == 1:
        return pl.BlockSpec((1, 1, mod.shape[2]), lambda b, i: (b, 0, 0))
    return pl.BlockSpec((1, tm, mod.shape[2]), lambda b, i: (b, i, 0))


def _ada_kernel(c_ref, w_ref, b_ref, o_ref):
    c = c_ref[...]
    o_ref[0] = _dot(_silu(c).astype(BF16), w_ref[0].astype(BF16)) + b_ref[0]


def _ada_call(c_all, ada_w, ada_b):
    L, D, N = ada_w.shape
    M = c_all.shape[0]
    tn = 1024
    return pl.pallas_call(
        _ada_kernel,
        grid=(L, N // tn),
        in_specs=[pl.BlockSpec((M, D), lambda l, j: (0, 0)),
                  pl.BlockSpec((1, D, tn), lambda l, j: (l, 0, j)),
                  pl.BlockSpec((1, 1, tn), lambda l, j: (l, 0, j))],
        out_specs=pl.BlockSpec((1, M, tn), lambda l, j: (l, 0, j)),
        out_shape=jax.ShapeDtypeStruct((L, M, N), F32),
        compiler_params=_params("arbitrary", "arbitrary"),
        name="ada",
    )(c_all, ada_w, ada_b.reshape(L, 1, N))


def _ffn_kernel(x_ref, sh_ref, sc_ref, gt_ref, g_ref, w1_ref, w2_ref, o_ref, acc_ref, *, ff, tf):
    x = x_ref[0]
    h = _modulate(x, g_ref[...], sh_ref[0], sc_ref[0]).astype(BF16)
    for c in range(ff // tf):
        a = _dot(h, w1_ref[:, c * tf:(c + 1) * tf])
        b = _dot(h, w1_ref[:, ff + c * tf:ff + (c + 1) * tf])
        t = (_silu(a) * b).astype(BF16)
        part = _dot(t, w2_ref[c * tf:(c + 1) * tf, :])
        if c == 0:
            acc_ref[...] = part
        else:
            acc_ref[...] += part
    o_ref[0] = x + 0.5 * gt_ref[0] * acc_ref[...]


def _ffn_call(x3, sh, sc, gt, g, w1, w2, tm):
    NB, R, D = x3.shape
    ff = w2.shape[0]
    tf = 256
    return pl.pallas_call(
        functools.partial(_ffn_kernel, ff=ff, tf=tf),
        grid=(NB, R // tm),
        in_specs=[_tok_spec(tm, D), _mod_spec(sh, tm), _mod_spec(sc, tm), _mod_spec(gt, tm),
                  _full_spec((1, D)), _full_spec(w1.shape), _full_spec(w2.shape)],
        out_specs=_tok_spec(tm, D),
        out_shape=jax.ShapeDtypeStruct(x3.shape, F32),
        scratch_shapes=[pltpu.VMEM((tm, D), F32)],
        compiler_params=_params("arbitrary", "arbitrary"),
        name="ffn",
    )(x3, sh, sc, gt, g.reshape(1, D), w1, w2)


def _even_kernel(x_ref, sh_ref, sc_ref, gt_ref, g_ref, win_ref, wout_ref, vg_ref, ws_ref, bs_ref,
                 cw_ref, cb_ref, lg_ref, lb_ref, o_ref, cs_ref, ext_ref, sa_ref, *, tm, aw):
    @pl.when(pl.program_id(1) == 0)
    def _():
        ext_ref[0:HIST, :] = jnp.zeros((HIST, ext_ref.shape[1]), F32)

    x = x_ref[0]
    h = _modulate(x, g_ref[...], sh_ref[0], sc_ref[0]).astype(BF16)
    z = _dot(h, win_ref[...])
    u = _gelu(z[:, 0:aw])
    gv = _gelu(z[:, aw:2 * aw])
    v = gv * lax.rsqrt(jnp.mean(gv * gv, axis=-1, keepdims=True) + EPS) * vg_ref[...]
    vb = v.astype(BF16)
    row = lax.broadcasted_iota(jnp.int32, (CHUNK, CHUNK), 0)
    col = lax.broadcasted_iota(jnp.int32, (CHUNK, CHUNK), 1)
    wm = [jnp.where(row >= col, ws_ref[gi], 0.0).astype(BF16) for gi in range(A_GROUPS)]
    gw = aw // A_GROUPS
    first_half = col < gw
    for c in range(tm // CHUNK):
        for q in range(aw // LANES):
            vq = vb[c * CHUNK:(c + 1) * CHUNK, q * LANES:(q + 1) * LANES]
            s0 = _dot(wm[2 * q], vq)
            s1 = _dot(wm[2 * q + 1], vq)
            sa_ref[c * CHUNK:(c + 1) * CHUNK, q * LANES:(q + 1) * LANES] = (
                jnp.where(first_half, s0, s1) + bs_ref[:, q * LANES:(q + 1) * LANES])
    a_out = u * sa_ref[...]
    glu = z[:, 2 * aw:3 * aw] * _sigmoid(z[:, 3 * aw:4 * aw])
    ext_ref[HIST:HIST + tm, :] = glu
    off = HIST - (CONV_W - 1)
    conv = cb_ref[...] + ext_ref[pl.ds(off, tm), :] * cw_ref[0:1, :]
    for k in range(1, CONV_W):
        conv = conv + ext_ref[pl.ds(off + k, tm), :] * cw_ref[k:k + 1, :]
    cs_ref[0] = ext_ref[pl.ds(tm + off, CONV_W - 1), :]
    ext_ref[0:HIST, :] = ext_ref[tm:tm + HIST, :]
    mu = jnp.mean(conv, axis=-1, keepdims=True)
    cc = conv - mu
    var = jnp.mean(cc * cc, axis=-1, keepdims=True)
    b_out = _silu(cc * lax.rsqrt(var + EPS) * lg_ref[...] + lb_ref[...])
    out = _dot(a_out.astype(BF16), wout_ref[0:aw, :]) + _dot(b_out.astype(BF16), wout_ref[aw:, :])
    o_ref[0] = x + gt_ref[0] * out


def _even_call(x3, sh, sc, gt, g, w_in, w_out, v_g, ws, bs, cw, cb, ln_g, ln_b, tm):
    B, T, D = x3.shape
    aw = v_g.shape[0]
    bw = cw.shape[1]
    assert aw == bw and aw // A_GROUPS * 2 == LANES and T % tm == 0 and tm % CHUNK == 0
    bs_exp = jnp.repeat(bs.T, aw // A_GROUPS, axis=1)
    cw_pad = jnp.pad(cw, ((0, 1), (0, 0)))
    row = lambda a: a.reshape(1, -1)
    return pl.pallas_call(
        functools.partial(_even_kernel, tm=tm, aw=aw),
        grid=(B, T // tm),
        in_specs=[_tok_spec(tm, D), _mod_spec(sh, tm), _mod_spec(sc, tm), _mod_spec(gt, tm),
                  _full_spec((1, D)), _full_spec(w_in.shape), _full_spec(w_out.shape),
                  _full_spec((1, aw)), _full_spec(ws.shape), _full_spec(bs_exp.shape),
                  _full_spec(cw_pad.shape), _full_spec((1, bw)), _full_spec((1, bw)), _full_spec((1, bw))],
        out_specs=[_tok_spec(tm, D), pl.BlockSpec((1, CONV_W - 1, bw), lambda b, i: (b, 0, 0))],
        out_shape=[jax.ShapeDtypeStruct(x3.shape, F32), jax.ShapeDtypeStruct((B, CONV_W - 1, bw), F32)],
        scratch_shapes=[pltpu.VMEM((tm + HIST, bw), F32), pltpu.VMEM((tm, aw), F32)],
        compiler_params=_params("arbitrary", "arbitrary"),
        name="even_prompt",
    )(x3, sh, sc, gt, row(g), w_in, w_out, row(v_g), ws, bs_exp, cw_pad, row(cb), row(ln_g), row(ln_b))


def _even_s_kernel(x_ref, sh_ref, sc_ref, gt_ref, g_ref, win_ref, wout_ref, vg_ref, coef_ref, bsa_ref,
                   st_ref, cw_ref, cb_ref, lg_ref, lb_ref, o_ref, cs_ref, v_ref, *, ts, bsz, aw):
    x = x_ref[0]
    h = _modulate(x, g_ref[...], sh_ref[0], sc_ref[0]).astype(BF16)
    z = _dot(h, win_ref[...])
    u = _gelu(z[:, 0:aw])
    gv = _gelu(z[:, aw:2 * aw])
    v = gv * lax.rsqrt(jnp.mean(gv * gv, axis=-1, keepdims=True) + EPS) * vg_ref[...]
    glu = z[:, 2 * aw:3 * aw] * _sigmoid(z[:, 3 * aw:4 * aw])
    hist = CONV_W - 1
    sl = lambda a, t: a[t * bsz:(t + 1) * bsz]
    a_parts, b_parts = [], []
    for t in range(ts):
        v_ref[t] = sl(v, t)
        s = bsa_ref[t]
        for j in range(t + 1):
            s = s + coef_ref[t, j] * sl(v, j)
        a_parts.append(sl(u, t) * s)
        conv = cb_ref[...]
        for m in range(t, hist):
            conv = conv + st_ref[m] * cw_ref[m - t:m - t + 1, :]
        for j in range(t + 1):
            conv = conv + sl(glu, j) * cw_ref[hist - t + j:hist - t + j + 1, :]
        mu = jnp.mean(conv, axis=-1, keepdims=True)
        cc = conv - mu
        var = jnp.mean(cc * cc, axis=-1, keepdims=True)
        b_parts.append(_silu(cc * lax.rsqrt(var + EPS) * lg_ref[...] + lb_ref[...]))
    for i in range(hist):
        cs_ref[i] = st_ref[i + ts] if i + ts < hist else sl(glu, i + ts - hist)
    a_out = jnp.concatenate(a_parts, axis=0).astype(BF16)
    b_out = jnp.concatenate(b_parts, axis=0).astype(BF16)
    out = _dot(a_out, wout_ref[0:aw, :]) + _dot(b_out, wout_ref[aw:, :])
    o_ref[0] = x + gt_ref[0] * out


def _even_s_call(x3, sh, sc, gt, g, w_in, w_out, v_g, ws, bs, state, cw, cb, ln_g, ln_b, ts, bsz):
    _, R, D = x3.shape
    aw = v_g.shape[0]
    bw = cw.shape[1]
    gw = aw // A_GROUPS
    hist = CONV_W - 1
    assert ts <= CHUNK and ts <= hist
    coef = jnp.repeat(jnp.transpose(ws[:, :ts, :ts], (1, 2, 0)), gw, axis=2).reshape(ts, ts, 1, aw)
    bsa = jnp.repeat(bs[:, :ts].T, gw, axis=1).reshape(ts, 1, aw)
    st = jnp.transpose(state, (1, 0, 2))
    row = lambda a: a.reshape(1, -1)
    args = (x3, sh, sc, gt, row(g), w_in, w_out, row(v_g), coef, bsa, st, cw, row(cb), row(ln_g), row(ln_b))
    x_new, cs, v = pl.pallas_call(
        functools.partial(_even_s_kernel, ts=ts, bsz=bsz, aw=aw),
        grid=(1,),
        in_specs=[_full_spec(a.shape) for a in args],
        out_specs=[_full_spec(x3.shape), _full_spec((hist, bsz, bw)), _full_spec((ts, bsz, aw))],
        out_shape=[jax.ShapeDtypeStruct(x3.shape, F32), jax.ShapeDtypeStruct((hist, bsz, bw), F32),
                   jax.ShapeDtypeStruct((ts, bsz, aw), F32)],
        compiler_params=_params("arbitrary"),
        name="even_sample",
    )(*args)
    return x_new, jnp.transpose(cs, (1, 0, 2)), jnp.transpose(v, (1, 0, 2))


def _nsa_proj_kernel(x_ref, sh_ref, sc_ref, g_ref, w_ref, q_ref, cmp_ref, sel_ref, win_ref, gate_ref, gatet_ref):
    x = x_ref[0]
    h = _modulate(x, g_ref[...], sh_ref[0], sc_ref[0]).astype(BF16)
    z = _dot(h, w_ref[...])
    nq = C_HEADS * HEAD_DIM
    q_ref[0] = z[:, 0:nq]
    cmp_ref[0] = z[:, nq:nq + 2 * GH]
    sel_ref[0] = z[:, nq + 2 * GH:nq + 4 * GH]
    win_ref[0] = z[:, nq + 4 * GH:nq + 6 * GH]
    gate = _sigmoid(z[:, nq + 6 * GH:nq + 6 * GH + LANES])
    gate_ref[0] = gate
    gatet_ref[0] = gate.T


def _nsa_proj_call(x3, sh, sc, g, w_in_pad, tm):
    NB, R, D = x3.shape
    nq = C_HEADS * HEAD_DIM
    widths = (nq, 2 * GH, 2 * GH, 2 * GH, LANES)
    return pl.pallas_call(
        _nsa_proj_kernel,
        grid=(NB, R // tm),
        in_specs=[_tok_spec(tm, D), _mod_spec(sh, tm), _mod_spec(sc, tm), _full_spec((1, D)),
                  _full_spec(w_in_pad.shape)],
        out_specs=[_tok_spec(tm, w) for w in widths] + [pl.BlockSpec((1, LANES, tm), lambda b, i: (b, 0, i))],
        out_shape=[jax.ShapeDtypeStruct((NB, R, w), F32) for w in widths]
        + [jax.ShapeDtypeStruct((NB, LANES, R), F32)],
        compiler_params=_params("arbitrary", "arbitrary"),
        name="nsa_proj",
    )(x3, sh, sc, g.reshape(1, D), w_in_pad)


def _headnorm_kernel(x_ref, g_ref, o_ref, *, width, scale):
    lane = lax.broadcasted_iota(jnp.int32, (1, LANES), 1)
    lo = lane < HEAD_DIM
    for c in range(width // LANES):
        x = x_ref[:, c * LANES:(c + 1) * LANES]
        sq = x * x
        s_lo = jnp.sum(jnp.where(lo, sq, 0.0), axis=-1, keepdims=True)
        s_hi = jnp.sum(jnp.where(lo, 0.0, sq), axis=-1, keepdims=True)
        ss = jnp.where(lo, s_lo, s_hi)
        y = x * lax.rsqrt(ss * (1.0 / HEAD_DIM) + EPS) * g_ref[...]
        if scale != 1.0:
            y = y * scale
        o_ref[:, c * LANES:(c + 1) * LANES] = y.astype(o_ref.dtype)


def _headnorm_call(x2, g, width, tr, scale=1.0):
    assert math.frexp(scale)[0] == 0.5
    N = x2.shape[0]
    g2 = jnp.tile(g, LANES // HEAD_DIM).reshape(1, LANES)
    return pl.pallas_call(
        functools.partial(_headnorm_kernel, width=width, scale=scale),
        grid=(N // tr,),
        in_specs=[pl.BlockSpec((tr, width), lambda i: (i, 0)), _full_spec((1, LANES))],
        out_specs=pl.BlockSpec((tr, width), lambda i: (i, 0)),
        out_shape=jax.ShapeDtypeStruct((N, width), BF16),
        compiler_params=_params("arbitrary"),
        name="headnorm",
    )(x2, g2)


def _compress_core(load_rows, n, pe_ref, w1_ref, w2_ref, kng_ref, o_ref, is_k):
    hid = w2_ref.shape[1]
    nstk = CMP_STRIDE // STACK
    accs = [None] * KV_GROUPS
    ctop = jnp.zeros((1, hid), F32)
    cbot = jnp.zeros((1, hid), F32)
    pe = pe_ref[0].astype(BF16)
    for l4 in range(nstk):
        w = w1_ref[0, l4]
        r = _dot(pe, w)
        ctop = ctop + r[l4:l4 + 1, 0:hid]
        cbot = cbot + r[nstk + l4:nstk + l4 + 1, hid:2 * hid]
        for gi in range(KV_GROUPS):
            xcat = jnp.concatenate([load_rows(l4 * STACK + i, gi) for i in range(STACK)], axis=1).astype(BF16)
            part = _dot(xcat, w)
            accs[gi] = part if accs[gi] is None else accs[gi] + part
    for gi in range(KV_GROUPS):
        a = accs[gi][:, 0:hid] + ctop
        b = pltpu.roll(accs[gi][:, hid:2 * hid] + cbot, n - 1, 0)
        y = _dot(_gelu(a + b).astype(BF16), w2_ref[0])
        yn = y * lax.rsqrt(jnp.mean(y * y, axis=-1, keepdims=True) + EPS) * kng_ref[...]
        o_ref[0, 0, :, gi * HEAD_DIM:(gi + 1) * HEAD_DIM] = jnp.where(is_k, yn, y).astype(o_ref.dtype)


def _compress_p_kernel(x_ref, pe_ref, w1_ref, w2_ref, kng_ref, o_ref, *, n):
    kv = pl.program_id(1)
    slabs = 2 * GH // LANES
    gpl = LANES // HEAD_DIM

    def load(l, g):
        slab = x_ref[0, pl.ds(l * slabs + kv * (slabs // 2) + g // gpl, n, stride=CMP_STRIDE * slabs), :]
        return slab[:, (g % gpl) * HEAD_DIM:(g % gpl + 1) * HEAD_DIM]

    _compress_core(load, n, pe_ref, w1_ref, w2_ref, kng_ref, o_ref, kv == 0)


def _compress_weights(pe, w1, w2):
    hid = w1.shape[-1]
    w1r = w1.reshape(2, CMP_BLOCK, HEAD_DIM, hid)
    pair = jnp.concatenate([w1r[:, :CMP_STRIDE], w1r[:, CMP_STRIDE:]], axis=-1)
    w1c = pair.reshape(2, CMP_STRIDE // STACK, STACK * HEAD_DIM, 2 * hid).astype(BF16)
    return pe.reshape(2, 2 * CMP_STRIDE // STACK, STACK * HEAD_DIM), w1c, w2.astype(BF16)


def _compress_p_call(cmp_rows, pe, w1, w2, kn_g):
    B, T, _ = cmp_rows.shape
    n = T // CMP_STRIDE
    pe, w1r, w2r = _compress_weights(pe, w1, w2)
    hid = w1.shape[-1]
    slabs = 2 * GH // LANES
    cmp_rows = cmp_rows.reshape(B, T * slabs, LANES)
    return pl.pallas_call(
        functools.partial(_compress_p_kernel, n=n),
        grid=(B, 2),
        in_specs=[pl.BlockSpec((1, T * slabs, LANES), lambda b, kv: (b, 0, 0)),
                  pl.BlockSpec((1,) + pe.shape[1:], lambda b, kv: (kv, 0, 0)),
                  pl.BlockSpec((1,) + w1r.shape[1:], lambda b, kv: (kv, 0, 0, 0)),
                  pl.BlockSpec((1, hid, HEAD_DIM), lambda b, kv: (kv, 0, 0)),
                  _full_spec((1, HEAD_DIM))],
        out_specs=pl.BlockSpec((1, 1, n, GH), lambda b, kv: (b, kv, 0, 0)),
        out_shape=jax.ShapeDtypeStruct((B, 2, n, GH), BF16),
        compiler_params=_params("arbitrary", "arbitrary"),
        name="compress_prompt",
    )(cmp_rows, pe, w1r, w2r, kn_g.reshape(1, HEAD_DIM))


def _compress_s_kernel(pt_ref, cache_ref, pe_ref, w1_ref, w2_ref, kng_ref, o_ref, buf_ref, sem_ref,
                       *, n, n_pages, page_size, layer):
    b = pl.program_id(0)
    kv = pl.program_id(1)
    step = b * 2 + kv
    n_steps = pl.num_programs(0) * 2

    def fetch(s, start):
        sb = s // 2
        slot = s % 2
        for p in range(n_pages):
            cp = pltpu.make_async_copy(
                cache_ref.at[pt_ref[sb, p], :, layer, s % 2],
                buf_ref.at[slot, pl.ds(p * page_size, page_size)],
                sem_ref.at[slot])
            cp.start() if start else cp.wait()

    @pl.when(step == 0)
    def _():
        fetch(step, True)

    @pl.when(step + 1 < n_steps)
    def _():
        fetch(step + 1, True)

    fetch(step, False)
    slot = step % 2
    load = lambda l, g: buf_ref[slot, pl.ds(l, n, stride=CMP_STRIDE), g, :]
    _compress_core(load, n, pe_ref, w1_ref, w2_ref, kng_ref, o_ref, kv == 0)


def _compress_s_call(cache6, page_table, layer, pe, w1, w2, kn_g):
    Bs, n_pages = page_table.shape
    page_size = cache6.shape[1]
    past = n_pages * page_size
    n = past // CMP_STRIDE
    pe, w1r, w2r = _compress_weights(pe, w1, w2)
    hid = w1.shape[-1]
    grid_spec = pltpu.PrefetchScalarGridSpec(
        num_scalar_prefetch=1,
        grid=(Bs, 2),
        in_specs=[pl.BlockSpec(memory_space=pl.ANY),
                  pl.BlockSpec((1,) + pe.shape[1:], lambda b, kv, pt: (kv, 0, 0)),
                  pl.BlockSpec((1,) + w1r.shape[1:], lambda b, kv, pt: (kv, 0, 0, 0)),
                  pl.BlockSpec((1, hid, HEAD_DIM), lambda b, kv, pt: (kv, 0, 0)),
                  pl.BlockSpec((1, HEAD_DIM), lambda b, kv, pt: (0, 0))],
        out_specs=pl.BlockSpec((1, 1, n, GH), lambda b, kv, pt: (b, kv, 0, 0)),
        scratch_shapes=[pltpu.VMEM((2, past, KV_GROUPS, HEAD_DIM), F32), pltpu.SemaphoreType.DMA((2,))],
    )
    return pl.pallas_call(
        functools.partial(_compress_s_kernel, n=n, n_pages=n_pages, page_size=page_size, layer=layer),
        grid_spec=grid_spec,
        out_shape=jax.ShapeDtypeStruct((Bs, 2, n, GH), BF16),
        compiler_params=_params("arbitrary", "arbitrary"),
        name="compress_sample",
    )(page_table, cache6, pe, w1r, w2r, kn_g.reshape(1, HEAD_DIM))


def _select(imp, tpos, nsb):
    j = lax.broadcasted_iota(jnp.int32, imp.shape, 1)
    cur = tpos // SEL_BLOCK
    valid = (j * SEL_BLOCK <= tpos) & (j < nsb)
    forced = (j == 0) | (j == cur) | (j == cur - 1)
    impf = jnp.where(valid, jnp.where(forced, FORCE, imp), NEG)
    rank = jnp.zeros(imp.shape, jnp.int32)
    for jp in range(nsb):
        c = impf[:, jp:jp + 1]
        beats = (c > impf) | ((c == impf) & (jp < j))
        rank = rank + beats.astype(jnp.int32)
    return (rank < min(N_SEL, nsb)) & valid


def _select_t(imp, tpos, nsb):
    j = lax.broadcasted_iota(jnp.int32, imp.shape, 0)
    cur = tpos // SEL_BLOCK
    valid = (j * SEL_BLOCK <= tpos) & (j < nsb)
    forced = (j == 0) | (j == cur) | (j == cur - 1)
    impf = jnp.where(valid, jnp.where(forced, FORCE, imp), NEG)
    rank = jnp.zeros(imp.shape, jnp.int32)
    for jp in range(nsb):
        c = impf[jp:jp + 1, :]
        beats = (c > impf) | ((c == impf) & (jp < j))
        rank = rank + beats.astype(jnp.int32)
    return (rank < min(N_SEL, nsb)) & valid


def _overlap_np(n_cmp, nsb, rows, cols):
    cs = np.arange(n_cmp)[:, None] * CMP_STRIDE
    ss = np.arange(nsb)[None, :] * SEL_BLOCK
    ov = np.clip(np.minimum(cs + CMP_BLOCK, ss + SEL_BLOCK) - np.maximum(cs, ss), 0, None).astype(np.float32) / CMP_STRIDE
    out = np.zeros((rows, cols), np.float32)
    out[:n_cmp, :nsb] = ov
    return out


def _stacked_qt(qt, gi):
    return jnp.concatenate([qt[(gi * HPG + p) * HEAD_DIM:(gi * HPG + p + 1) * HEAD_DIM] for p in range(HPG)],
                           axis=1).astype(BF16)


def _cmp_p_kernel(rb_ref, q_ref, kv_ref, ovt_ref, gt_ref, o_ref, sel_ref, bias_ref, ot_ref, *, n_cmp, nsb):
    qb = pl.program_id(0)
    npad = kv_ref.shape[2]
    cols = HPG * TQ

    @pl.when(pl.program_id(1) == 0)
    def _():
        n = lax.broadcasted_iota(jnp.int32, (npad, TQ), 0)
        t = qb * TQ + lax.broadcasted_iota(jnp.int32, (npad, TQ), 1)
        d = t - (n * CMP_STRIDE + CMP_BLOCK - 1)
        for h in range(C_HEADS):
            bias_ref[:, h * TQ:(h + 1) * TQ] = _bias_lut(d, rb_ref, h)

    qt = q_ref[0].astype(F32).T
    n4 = lax.broadcasted_iota(jnp.int32, (npad, cols), 0)
    t4 = qb * TQ + (lax.broadcasted_iota(jnp.int32, (npad, cols), 1) & (TQ - 1))
    mask = ((t4 - (n4 * CMP_STRIDE + CMP_BLOCK - 1)) >= 0) & (n4 < n_cmp)
    kc = kv_ref[0, 0]
    vct = kv_ref[0, 1].astype(F32).T
    tpos = qb * TQ + lax.broadcasted_iota(jnp.int32, (nsb, TQ), 1)
    for gi in range(KV_GROUPS):
        s = _dot(kc[:, gi * HEAD_DIM:(gi + 1) * HEAD_DIM], _stacked_qt(qt, gi)) + bias_ref[:, gi * cols:(gi + 1) * cols]
        s = jnp.where(mask, s, NEG)
        m = jnp.max(s, axis=0, keepdims=True)
        e = jnp.where(mask, jnp.exp(s - m), 0.0)
        prob = e * (1.0 / jnp.maximum(jnp.sum(e, axis=0, keepdims=True), 1e-30))
        o = _dot(vct[gi * HEAD_DIM:(gi + 1) * HEAD_DIM].astype(BF16), prob.astype(BF16))
        psum = prob[:, 0:TQ]
        for p in range(HPG):
            h = gi * HPG + p
            ot_ref[h * HEAD_DIM:(h + 1) * HEAD_DIM, :] = o[:, p * TQ:(p + 1) * TQ] * gt_ref[0, h:h + 1, :]
            if p:
                psum = psum + prob[:, p * TQ:(p + 1) * TQ]
        hi, lo = _split_bf16(psum)
        imp = _dot(ovt_ref[...], hi) + _dot(ovt_ref[...], lo)
        sel_ref[0, gi] = _select_t(imp, tpos, nsb).astype(F32)
    o_ref[0] = ot_ref[...].T


def _cmp_p_call(qn, kvc, rel_bias, gates_t, n_cmp):
    B, T, nq = qn.shape
    npad = kvc.shape[2]
    nsb = -(-T // SEL_BLOCK)
    assert nsb % 8 == 0
    ovt = jnp.asarray(_overlap_np(n_cmp, nsb, npad, nsb).T, BF16)
    return pl.pallas_call(
        functools.partial(_cmp_p_kernel, n_cmp=n_cmp, nsb=nsb),
        grid=(T // TQ, B),
        in_specs=[_SMEM_SPEC,
                  pl.BlockSpec((1, TQ, nq), lambda i, b: (b, i, 0)),
                  pl.BlockSpec((1, 2, npad, GH), lambda i, b: (b, 0, 0, 0)),
                  _full_spec(ovt.shape),
                  pl.BlockSpec((1, LANES, TQ), lambda i, b: (b, 0, i))],
        out_specs=[pl.BlockSpec((1, TQ, nq), lambda i, b: (b, i, 0)),
                   pl.BlockSpec((1, KV_GROUPS, nsb, TQ), lambda i, b: (b, 0, 0, i))],
        out_shape=[jax.ShapeDtypeStruct((B, T, nq), F32),
                   jax.ShapeDtypeStruct((B, KV_GROUPS, nsb, T), F32)],
        scratch_shapes=[pltpu.VMEM((npad, C_HEADS * TQ), F32), pltpu.VMEM((nq, TQ), F32)],
        compiler_params=_params("arbitrary", "arbitrary"),
        name="cmp_attn_prompt",
    )(rel_bias, qn, kvc, ovt, gates_t)


ONES_ROWS = 16
N_BIAS_TILES = 4


def _pattn_kernel(qb_ref, kb_ref, fl_ref, rb_ref, q_ref, k_ref, v_ref, *rest, mode, gate_row0):
    if mode == "sel":
        sel_ref, gt_ref, o_ref, qst_ref, m_ref, acc_ref, bias_ref, ot_ref = rest
    else:
        gt_ref, o_ref, qst_ref, m_ref, acc_ref, bias_ref, ot_ref = rest
    i = pl.program_id(1)
    qb = qb_ref[i]
    kb = kb_ref[i]
    cols = HPG * TQ

    @pl.when((pl.program_id(0) == 0) & (i == 0))
    def _():
        row = lax.broadcasted_iota(jnp.int32, (TK, TQ), 0)
        col = lax.broadcasted_iota(jnp.int32, (TK, TQ), 1)
        for h in range(C_HEADS):
            for oi in range(N_BIAS_TILES - 1):
                bias_ref[oi, :, h * TQ:(h + 1) * TQ] = _bias_lut(oi * TQ + col - row, rb_ref, h)
            bias_ref[N_BIAS_TILES - 1, :, h * TQ:(h + 1) * TQ] = jnp.full((TK, TQ), rb_ref[N_BUCKETS - 1, h], F32)

    @pl.when((fl_ref[i] & 1) == 1)
    def _():
        qt = q_ref[0].astype(F32).T
        for gi in range(KV_GROUPS):
            qst_ref[gi] = _stacked_qt(qt, gi)
        m_ref[...] = jnp.full(m_ref.shape, NEG, F32)
        acc_ref[...] = jnp.zeros(acc_ref.shape, F32)

    off = qb * TQ - kb * TK
    oi = jnp.minimum(off // TQ, N_BIAS_TILES - 1)
    k = k_ref[0]
    vt = v_ref[0].T
    d = off + lax.broadcasted_iota(jnp.int32, (TK, TQ), 1) - lax.broadcasted_iota(jnp.int32, (TK, TQ), 0)
    base = d >= 0
    if mode == "win":
        base = base & (d < WINDOW)
    ones = jnp.ones((ONES_ROWS, TK), BF16)
    for gi in range(KV_GROUPS):
        mask = base
        if mode == "sel":
            sm = sel_ref[0, gi, 0]
            blk = jnp.concatenate([jnp.broadcast_to(sm[r:r + 1], (SEL_BLOCK, TQ)) for r in range(TK // SEL_BLOCK)], axis=0)
            mask = base & (blk > 0.5)
        kg = k[:, gi * HEAD_DIM:(gi + 1) * HEAD_DIM]
        vext = jnp.concatenate([vt[gi * HEAD_DIM:(gi + 1) * HEAD_DIM].astype(BF16), ones], axis=0)
        for p in range(HPG):
            cs = slice(p * TQ, (p + 1) * TQ)
            hs = slice((gi * HPG + p) * TQ, (gi * HPG + p + 1) * TQ)
            s = _dot(kg, qst_ref[gi, :, cs]) + bias_ref[oi, :, hs]
            s = jnp.where(mask, s, NEG)
            m_old = m_ref[gi, :, cs]
            m_new = jnp.maximum(m_old, jnp.max(s, axis=0, keepdims=True))
            e = jnp.exp(s - m_new).astype(BF16)
            alpha = jnp.exp(m_old - m_new)
            acc_ref[gi, :, cs] = alpha * acc_ref[gi, :, cs] + _dot(vext, e)
            m_ref[gi, :, cs] = m_new

    @pl.when((fl_ref[i] & 2) == 2)
    def _():
        for gi in range(KV_GROUPS):
            a = acc_ref[gi]
            o = a[0:HEAD_DIM] * (1.0 / jnp.maximum(a[HEAD_DIM:HEAD_DIM + 1], 1e-30))
            for p in range(HPG):
                h = gi * HPG + p
                ot_ref[h * HEAD_DIM:(h + 1) * HEAD_DIM, :] = (
                    o[:, p * TQ:(p + 1) * TQ] * gt_ref[0, gate_row0 + h:gate_row0 + h + 1, :])
        o_ref[0] = ot_ref[...].T


def _pair_tables(nq, mode):
    qbs, kbs, fls = [], [], []
    for qb in range(nq):
        hi = (qb * TQ + TQ - 1) // TK
        lo = 0 if mode == "sel" else max((qb * TQ - WINDOW + 1) // TK, 0)
        for kb in range(lo, hi + 1):
            qbs.append(qb)
            kbs.append(kb)
            fls.append((1 if kb == lo else 0) | (2 if kb == hi else 0))
    return tuple(jnp.asarray(np.array(a, np.int32)) for a in (qbs, kbs, fls))


def _pattn_call(qn, kn, rows, selmask, gates_t, rel_bias, mode):
    B, T, nq_w = qn.shape
    assert T % TK == 0 and TK % TQ == 0 and (N_BIAS_TILES - 1) * TQ - (TK - 1) >= MAX_DIST
    qbs, kbs, fls = _pair_tables(T // TQ, mode)
    in_specs = [_SMEM_SPEC,
                pl.BlockSpec((1, TQ, nq_w), lambda b, i, qb, kb, fl: (b, qb[i], 0)),
                pl.BlockSpec((1, TK, GH), lambda b, i, qb, kb, fl: (b, kb[i], 0)),
                pl.BlockSpec((1, TK, GH), lambda b, i, qb, kb, fl: (b, kb[i], 1))]
    args = [rel_bias, qn, kn, rows]
    if mode == "sel":
        nblk = TK // SEL_BLOCK
        in_specs.append(pl.BlockSpec((1, KV_GROUPS, 1, nblk, TQ), lambda b, i, qb, kb, fl: (b, 0, kb[i], 0, qb[i])))
        args.append(selmask.reshape(B, KV_GROUPS, T // TK, nblk, T))
    in_specs.append(pl.BlockSpec((1, LANES, TQ), lambda b, i, qb, kb, fl: (b, 0, qb[i])))
    args.append(gates_t)
    cols = HPG * TQ
    grid_spec = pltpu.PrefetchScalarGridSpec(
        num_scalar_prefetch=3,
        grid=(B, int(qbs.shape[0])),
        in_specs=in_specs,
        out_specs=pl.BlockSpec((1, TQ, nq_w), lambda b, i, qb, kb, fl: (b, qb[i], 0)),
        scratch_shapes=[pltpu.VMEM((KV_GROUPS, HEAD_DIM, cols), BF16),
                        pltpu.VMEM((KV_GROUPS, 1, cols), F32),
                        pltpu.VMEM((KV_GROUPS, HEAD_DIM + ONES_ROWS, cols), F32),
                        pltpu.VMEM((N_BIAS_TILES, TK, C_HEADS * TQ), F32),
                        pltpu.VMEM((nq_w, TQ), F32)],
    )
    return pl.pallas_call(
        functools.partial(_pattn_kernel, mode=mode, gate_row0=C_HEADS * (1 if mode == "sel" else 2)),
        grid_spec=grid_spec,
        out_shape=jax.ShapeDtypeStruct((B, T, nq_w), F32),
        compiler_params=_params("arbitrary", "arbitrary"),
        name="attn_prompt_" + mode,
    )(qbs, kbs, fls, *args)


def _combine_kernel(x_ref, gt_ref, oc_ref, os_ref, ow_ref, gate_ref, w_ref, o_ref, mix_ref):
    gate = gate_ref[0]
    for h in range(C_HEADS):
        hs = slice(h * HEAD_DIM, (h + 1) * HEAD_DIM)
        mix_ref[:, hs] = (gate[:, h:h + 1] * oc_ref[0, :, hs]
                          + gate[:, C_HEADS + h:C_HEADS + h + 1] * os_ref[0, :, hs]
                          + gate[:, 2 * C_HEADS + h:2 * C_HEADS + h + 1] * ow_ref[0, :, hs]).astype(BF16)
    o_ref[0] = x_ref[0] + gt_ref[0] * _dot(mix_ref[...], w_ref[...])


def _combine_call(x3, gt, oc, os_, ow, gates, w_out, tm):
    NB, R, D = x3.shape
    nq = C_HEADS * HEAD_DIM
    return pl.pallas_call(
        _combine_kernel,
        grid=(NB, R // tm),
        in_specs=[_tok_spec(tm, D), _mod_spec(gt, tm), _tok_spec(tm, nq), _tok_spec(tm, nq), _tok_spec(tm, nq),
                  _tok_spec(tm, LANES), _full_spec(w_out.shape)],
        out_specs=_tok_spec(tm, D),
        out_shape=jax.ShapeDtypeStruct(x3.shape, F32),
        scratch_shapes=[pltpu.VMEM((tm, nq), BF16)],
        compiler_params=_params("arbitrary", "arbitrary"),
        name="nsa_combine",
    )(x3, gt, oc, os_, ow, gates, w_out)


def _sum_proj_kernel(x_ref, gt_ref, oc_ref, os_ref, ow_ref, w_ref, o_ref):
    mix = (oc_ref[0] + os_ref[0] + ow_ref[0]).astype(BF16)
    o_ref[0] = x_ref[0] + gt_ref[0] * _dot(mix, w_ref[...])


def _sum_proj_call(x3, gt, oc, os_, ow, w_out, tm):
    NB, R, D = x3.shape
    nq = C_HEADS * HEAD_DIM
    return pl.pallas_call(
        _sum_proj_kernel,
        grid=(NB, R // tm),
        in_specs=[_tok_spec(tm, D), _mod_spec(gt, tm), _tok_spec(tm, nq), _tok_spec(tm, nq), _tok_spec(tm, nq),
                  _full_spec(w_out.shape)],
        out_specs=_tok_spec(tm, D),
        out_shape=jax.ShapeDtypeStruct(x3.shape, F32),
        compiler_params=_params("arbitrary", "arbitrary"),
        name="nsa_out_proj",
    )(x3, gt, oc, os_, ow, w_out)


def _lut_kernel(rb_ref, d_ref, h_ref, o_ref):
    d = d_ref[...]
    hh = h_ref[...]
    out = jnp.zeros(d.shape, F32)
    for h in range(C_HEADS):
        out = jnp.where(hh == h, _bias_lut(d, rb_ref, h), out)
    o_ref[...] = out


def _lut_call(rel_bias, dist, head):
    dist = np.ascontiguousarray(np.broadcast_to(dist, head.shape)).astype(np.int32)
    head = np.ascontiguousarray(head).astype(np.int32)
    return pl.pallas_call(
        _lut_kernel,
        grid=(1,),
        in_specs=[_SMEM_SPEC, _full_spec(dist.shape), _full_spec(head.shape)],
        out_specs=_full_spec(dist.shape),
        out_shape=jax.ShapeDtypeStruct(dist.shape, F32),
        compiler_params=_params("arbitrary"),
        name="bias_lut",
    )(rel_bias, jnp.asarray(dist), jnp.asarray(head))


def _cmp_s_kernel(q_ref, kv_ref, bias_ref, ov_ref, o_ref, sel_ref, *, n_cmp, nsb, ts, past):
    npad = kv_ref.shape[2]
    rows = HPG * ts
    t = past + lax.broadcasted_iota(jnp.int32, (rows, npad), 0) % ts
    nidx = lax.broadcasted_iota(jnp.int32, (rows, npad), 1)
    mask = ((t - (nidx * CMP_STRIDE + CMP_BLOCK - 1)) >= 0) & (nidx < n_cmp)
    tpos = past + lax.broadcasted_iota(jnp.int32, (ts, 1), 0)
    for gi in range(KV_GROUPS):
        kc = kv_ref[0, 0, :, gi * HEAD_DIM:(gi + 1) * HEAD_DIM]
        vc = kv_ref[0, 1, :, gi * HEAD_DIM:(gi + 1) * HEAD_DIM]
        s = _dot_nt(q_ref[0, gi], kc) + bias_ref[gi]
        s = jnp.where(mask, s, NEG)
        m = jnp.max(s, axis=-1, keepdims=True)
        e = jnp.where(mask, jnp.exp(s - m), 0.0)
        prob = e / jnp.maximum(jnp.sum(e, axis=-1, keepdims=True), 1e-30)
        o_ref[0, gi] = _dot(prob.astype(BF16), vc)
        hi, lo = _split_bf16(prob)
        imp16 = _dot(hi, ov_ref[...]) + _dot(lo, ov_ref[...])
        imp = imp16[0:ts]
        for p in range(1, HPG):
            imp = imp + imp16[p * ts:(p + 1) * ts]
        sel_ref[0, gi] = _select(imp, tpos, nsb).astype(F32)


def _cmp_s_bias(rel_bias, npad, ts, past):
    rows = HPG * ts
    r = np.arange(KV_GROUPS * rows)
    dist = (past + r % ts)[:, None] - (np.arange(npad)[None, :] * CMP_STRIDE + CMP_BLOCK - 1)
    head = np.broadcast_to((r // ts)[:, None], dist.shape)
    return _lut_call(rel_bias, dist, head).reshape(KV_GROUPS, rows, npad)


def _cmp_s_call(q2, kvc, bias, n_cmp, nsb, ts, past):
    Bs = q2.shape[0]
    npad = kvc.shape[2]
    rows = HPG * ts
    lpad = -(-nsb // LANES) * LANES
    ov = jnp.asarray(_overlap_np(n_cmp, nsb, npad, lpad), BF16)
    return pl.pallas_call(
        functools.partial(_cmp_s_kernel, n_cmp=n_cmp, nsb=nsb, ts=ts, past=past),
        grid=(Bs,),
        in_specs=[pl.BlockSpec((1, KV_GROUPS, rows, HEAD_DIM), lambda b: (b, 0, 0, 0)),
                  pl.BlockSpec((1, 2, npad, GH), lambda b: (b, 0, 0, 0)),
                  _full_spec(bias.shape), _full_spec(ov.shape)],
        out_specs=[pl.BlockSpec((1, KV_GROUPS, rows, HEAD_DIM), lambda b: (b, 0, 0, 0)),
                   pl.BlockSpec((1, KV_GROUPS, ts, lpad), lambda b: (b, 0, 0, 0))],
        out_shape=[jax.ShapeDtypeStruct((Bs, KV_GROUPS, rows, HEAD_DIM), F32),
                   jax.ShapeDtypeStruct((Bs, KV_GROUPS, ts, lpad), F32)],
        compiler_params=_params("arbitrary"),
        name="cmp_attn_sample",
    )(q2, kvc, bias, ov)


def _decode_core(kp, vp, kn, vn, qblk, kng_col, eexp, bias_p, bias_n, mask_p, mask_n):
    qb = (qblk * kng_col).astype(BF16)

    def logits(k, bias):
        hi, lo = _split_bf16(k * k)
        ss = _dot(hi, eexp) + _dot(lo, eexp)
        r = lax.rsqrt(ss * (1.0 / HEAD_DIM) + EPS)
        return _dot(k.astype(BF16), qb) * r + bias

    lp = jnp.where(mask_p, logits(kp, bias_p), NEG)
    ln = jnp.where(mask_n, logits(kn, bias_n), NEG)
    m = jnp.maximum(jnp.max(lp, axis=0, keepdims=True), jnp.max(ln, axis=0, keepdims=True))
    ep = jnp.where(mask_p, jnp.exp(lp - m), 0.0)
    en = jnp.where(mask_n, jnp.exp(ln - m), 0.0)
    denom = jnp.sum(ep, axis=0, keepdims=True) + jnp.sum(en, axis=0, keepdims=True)
    inv = 1.0 / jnp.maximum(denom, 1e-30)
    of = _dot_tn((ep * inv).astype(BF16), vp.astype(BF16)) + _dot_tn((en * inv).astype(BF16), vn.astype(BF16))
    ncol = of.shape[0]
    per = ncol // KV_GROUPS
    rg = lax.broadcasted_iota(jnp.int32, (ncol, HEAD_DIM), 0) // per
    o = jnp.zeros((ncol, HEAD_DIM), F32)
    for gi in range(KV_GROUPS):
        o = o + jnp.where(rg == gi, of[:, gi * HEAD_DIM:(gi + 1) * HEAD_DIM], 0.0)
    return o


def _new_key_mask(ts, ncol, rows):
    jn = lax.broadcasted_iota(jnp.int32, (rows, ncol), 0)
    tn = lax.broadcasted_iota(jnp.int32, (rows, ncol), 1) % ts
    return (jn <= tn) & (jn < ts)


def _col_of_row(v):
    n = v.shape[1]
    r = lax.broadcasted_iota(jnp.int32, (n, n), 0)
    c = lax.broadcasted_iota(jnp.int32, (n, n), 1)
    return jnp.sum(jnp.where(r == c, jnp.broadcast_to(v, (n, n)), 0.0), axis=1, keepdims=True)


def _sel_s_kernel(pt_ref, cache_ref, new_ref, q_ref, kng_ref, eexp_ref, bp_ref, bn_ref, selp_ref, seln_ref,
                  o_ref, buf_ref, sem_ref, m_ref, l_ref, acc_ref, *, cp, page_size, layer, ts):
    b = pl.program_id(0)
    c = pl.program_id(1)
    nch = pl.num_programs(1)
    step = b * nch + c
    n_steps = pl.num_programs(0) * nch
    chunk = cp * page_size
    ncol = q_ref.shape[2]

    def fetch(s, start):
        sb = s // nch
        sc = s % nch
        slot = s % 2
        for p in range(cp):
            for kv in range(2):
                cpy = pltpu.make_async_copy(
                    cache_ref.at[pt_ref[sb, sc * cp + p], :, layer, kv],
                    buf_ref.at[slot, kv, pl.ds(p * page_size, page_size)],
                    sem_ref.at[slot])
                cpy.start() if start else cpy.wait()

    @pl.when(step == 0)
    def _():
        fetch(step, True)

    @pl.when(step + 1 < n_steps)
    def _():
        fetch(step + 1, True)

    fetch(step, False)
    slot = step % 2

    @pl.when(c == 0)
    def _():
        m_ref[...] = jnp.full(m_ref.shape, NEG, F32)
        l_ref[...] = jnp.zeros(l_ref.shape, F32)
        acc_ref[...] = jnp.zeros(acc_ref.shape, F32)

    qb = (q_ref[0] * kng_ref[...]).astype(BF16)
    eexp = eexp_ref[...]

    def update(k, v, bias, mask):
        hi, lo = _split_bf16(k * k)
        r = lax.rsqrt((_dot(hi, eexp) + _dot(lo, eexp)) * (1.0 / HEAD_DIM) + EPS)
        lg = jnp.where(mask, _dot(k.astype(BF16), qb) * r + bias, NEG)
        m_old = m_ref[...]
        m_new = jnp.maximum(m_old, jnp.max(lg, axis=0, keepdims=True))
        e = jnp.where(mask, jnp.exp(lg - m_new), 0.0)
        alpha = jnp.exp(m_old - m_new)
        l_ref[...] = alpha * l_ref[...] + jnp.sum(e, axis=0, keepdims=True)
        acc_ref[...] = _col_of_row(alpha) * acc_ref[...] + _dot_tn(e.astype(BF16), v.astype(BF16))
        m_ref[...] = m_new

    kcat = jnp.concatenate([buf_ref[slot, 0, :, gi, :] for gi in range(KV_GROUPS)], axis=1)
    vcat = jnp.concatenate([buf_ref[slot, 1, :, gi, :] for gi in range(KV_GROUPS)], axis=1)
    nblk = chunk // SEL_BLOCK
    mask_p = jnp.broadcast_to(selp_ref[0], (nblk, SEL_BLOCK, ncol)).reshape(chunk, ncol) > 0.5
    update(kcat, vcat, bp_ref[...], mask_p)

    @pl.when(c == nch - 1)
    def _():
        mask_n = _new_key_mask(ts, ncol, new_ref.shape[1]) & (seln_ref[0] > 0.5)
        update(new_ref[0, :, 0:GH], new_ref[0, :, GH:2 * GH], bn_ref[...], mask_n)
        of = acc_ref[...] * _col_of_row(1.0 / jnp.maximum(l_ref[...], 1e-30))
        rg = lax.broadcasted_iota(jnp.int32, (ncol, HEAD_DIM), 0) // (ncol // KV_GROUPS)
        o = jnp.zeros((ncol, HEAD_DIM), F32)
        for gi in range(KV_GROUPS):
            o = o + jnp.where(rg == gi, of[:, gi * HEAD_DIM:(gi + 1) * HEAD_DIM], 0.0)
        o_ref[0] = o


def _win_s_kernel(st_ref, new_ref, q_ref, kng_ref, eexp_ref, bp_ref, bn_ref, o_ref, nw_ref, *, ts, kpos0):
    wb = st_ref.shape[2]
    ncol = q_ref.shape[2]
    jp = lax.broadcasted_iota(jnp.int32, (wb, ncol), 0)
    tp = lax.broadcasted_iota(jnp.int32, (wb, ncol), 1) % ts
    dist = wb + tp - jp
    mask_p = (dist >= 0) & (dist < WINDOW) & (kpos0 + jp >= 0)
    mask_n = _new_key_mask(ts, ncol, new_ref.shape[1])
    st = st_ref[0, 0]
    o_ref[0] = _decode_core(st[:, 0:GH], st[:, GH:2 * GH], new_ref[0, :, 0:GH], new_ref[0, :, GH:2 * GH],
                            q_ref[0], kng_ref[...], eexp_ref[...], bp_ref[...], bn_ref[...], mask_p, mask_n)
    wout = nw_ref.shape[2]
    keep = wout - ts
    nw_ref[0, 0, 0:keep, :] = st_ref[0, 0, pl.ds(wb - keep, keep), :]
    nw_ref[0, 0, keep:wout, :] = new_ref[0, 0:ts, :]


def _decode_bias(rel_bias, ts, key_dist):
    ncol = C_HEADS * ts
    dist = key_dist[:, np.arange(ncol) % ts]
    head = np.broadcast_to((np.arange(ncol) // ts)[None, :], dist.shape)
    return _lut_call(rel_bias, dist, head)


def _sample_bias_tables(rel_bias, ts, past, wb, npad):
    tq = np.arange(ts)[None, :]
    tail = 2 * MAX_DIST
    assert past >= tail and tail - ts >= MAX_DIST
    sel_tail = _decode_bias(rel_bias, ts, tail + tq - np.arange(tail)[:, None])
    sel_past = jnp.concatenate([jnp.broadcast_to(sel_tail[0:1], (past - tail, sel_tail.shape[1])), sel_tail], axis=0)
    return dict(
        new=_decode_bias(rel_bias, ts, tq - np.arange(8)[:, None]),
        sel=sel_past,
        win=_decode_bias(rel_bias, ts, wb + tq - np.arange(wb)[:, None]),
        cmp=_cmp_s_bias(rel_bias, npad, ts, past))


SEL_CHUNK_PAGES = 16


def _sel_s_call(cache6, page_table, layer, new_rows, qblk, kng_col, eexp, bias_p, bias_n, selp, seln, ts):
    Bs, n_pages = page_table.shape
    page_size = cache6.shape[1]
    cp = min(SEL_CHUNK_PAGES, n_pages)
    assert n_pages % cp == 0 and page_size % SEL_BLOCK == 0
    chunk = cp * page_size
    ncol = qblk.shape[2]
    m3 = lambda b, c, pt: (b, 0, 0)
    c2 = lambda b, c, pt: (0, 0)
    grid_spec = pltpu.PrefetchScalarGridSpec(
        num_scalar_prefetch=1,
        grid=(Bs, n_pages // cp),
        in_specs=[pl.BlockSpec(memory_space=pl.ANY),
                  pl.BlockSpec((1,) + new_rows.shape[1:], m3),
                  pl.BlockSpec((1, GH, ncol), m3),
                  pl.BlockSpec(kng_col.shape, c2), pl.BlockSpec(eexp.shape, c2),
                  pl.BlockSpec((chunk, ncol), lambda b, c, pt: (c, 0)), pl.BlockSpec(bias_n.shape, c2),
                  pl.BlockSpec((1, chunk // SEL_BLOCK, 1, ncol), lambda b, c, pt: (b, c, 0, 0)),
                  pl.BlockSpec((1, 1, ncol), m3)],
        out_specs=pl.BlockSpec((1, ncol, HEAD_DIM), m3),
        scratch_shapes=[pltpu.VMEM((2, 2, chunk, KV_GROUPS, HEAD_DIM), F32), pltpu.SemaphoreType.DMA((2,)),
                        pltpu.VMEM((1, ncol), F32), pltpu.VMEM((1, ncol), F32), pltpu.VMEM((ncol, GH), F32)],
    )
    return pl.pallas_call(
        functools.partial(_sel_s_kernel, cp=cp, page_size=page_size, layer=layer, ts=ts),
        grid_spec=grid_spec,
        out_shape=jax.ShapeDtypeStruct((Bs, ncol, HEAD_DIM), F32),
        compiler_params=_params("arbitrary", "arbitrary"),
        name="sel_attn_sample",
    )(page_table, cache6, new_rows, qblk, kng_col, eexp, bias_p, bias_n, selp, seln)


def _win_s_call(state4, layer, new_rows, qblk, kng_col, eexp, bias_p, bias_n, ts, past):
    n_l, Bs, wb, _ = state4.shape
    ncol = qblk.shape[2]
    wout = min(WINDOW, wb + ts)
    m3 = lambda b: (b, 0, 0)
    c2 = lambda b: (0, 0)
    return pl.pallas_call(
        functools.partial(_win_s_kernel, ts=ts, kpos0=past - wb),
        grid=(Bs,),
        in_specs=[pl.BlockSpec((1, 1, wb, 2 * GH), lambda b: (layer, b, 0, 0)),
                  pl.BlockSpec((1,) + new_rows.shape[1:], m3),
                  pl.BlockSpec((1, GH, ncol), m3),
                  pl.BlockSpec(kng_col.shape, c2), pl.BlockSpec(eexp.shape, c2),
                  pl.BlockSpec(bias_p.shape, c2), pl.BlockSpec(bias_n.shape, c2)],
        out_specs=[pl.BlockSpec((1, ncol, HEAD_DIM), m3),
                   pl.BlockSpec((1, 1, wout, 2 * GH), lambda b: (0, b, 0, 0))],
        out_shape=[jax.ShapeDtypeStruct((Bs, ncol, HEAD_DIM), F32),
                   jax.ShapeDtypeStruct((1, Bs, wout, 2 * GH), F32)],
        compiler_params=_params("arbitrary"),
        name="win_attn_sample",
    )(state4, new_rows, qblk, kng_col, eexp, bias_p, bias_n)


def _nsa_prompt(x, mods, g, w_in_pad, w_out, qn_g, kn_g, pe, w1, w2, rel_bias, tm):
    B, T, D = x.shape
    N = B * T
    nq = C_HEADS * HEAD_DIM
    q, cmp_rows, sel_rows, win_rows, _, gates_t = _nsa_proj_call(x, mods[3], mods[4], g, w_in_pad, tm)
    tr = min(2048, N)
    assert N % tr == 0
    qn = _headnorm_call(q.reshape(N, nq), qn_g, nq, tr, SCALE).reshape(B, T, nq)
    seln = _headnorm_call(sel_rows.reshape(N, 2 * GH), kn_g, GH, tr).reshape(B, T, GH)
    winn = _headnorm_call(win_rows.reshape(N, 2 * GH), kn_g, GH, tr).reshape(B, T, GH)
    n_cmp = (T - CMP_BLOCK) // CMP_STRIDE + 1
    kvc = _compress_p_call(cmp_rows, pe, w1, w2, kn_g)
    o_cmp, selmask = _cmp_p_call(qn, kvc, rel_bias, gates_t, n_cmp)
    o_sel = _pattn_call(qn, seln, sel_rows, selmask, gates_t, rel_bias, "sel")
    o_win = _pattn_call(qn, winn, win_rows, None, gates_t, rel_bias, "win")
    x = _sum_proj_call(x, mods[5], o_cmp, o_sel, o_win, w_out, tm)
    wk = min(WINDOW, T)
    return x, cmp_rows, sel_rows, win_rows[:, T - wk:]


def _nsa_sample(x, mods, g, w_in_pad, w_out, qn_g, kn_g, pe, w1, w2, bias_tabs, cache_cmp3, cache_sel3,
                page_table, state_win4, layer, ts, bsz):
    R = ts * bsz
    nq = C_HEADS * HEAD_DIM
    past = page_table.shape[1] * cache_cmp3.shape[1]
    assert past % CMP_STRIDE == 0 and past % SEL_BLOCK == 0 and ts <= 8 and ts <= CMP_STRIDE
    q, cmp_rows, sel_rows, win_rows, gates, _ = _nsa_proj_call(x, mods[3], mods[4], g, w_in_pad, R)
    qn = _headnorm_call(q.reshape(R, nq), qn_g, nq, R, SCALE)
    q5 = jnp.transpose(qn.reshape(ts, bsz, KV_GROUPS, HPG, HEAD_DIM), (1, 2, 3, 0, 4))
    q2 = q5.reshape(bsz, KV_GROUPS, HPG * ts, HEAD_DIM)
    qd = jnp.transpose(q5.astype(F32).reshape(bsz, KV_GROUPS, HPG * ts, HEAD_DIM), (0, 1, 3, 2))
    qblk = (qd[:, :, :, None, :] * jnp.eye(KV_GROUPS, dtype=F32)[None, :, None, :, None]).reshape(bsz, GH, C_HEADS * ts)
    ncol = C_HEADS * ts
    tk = past + ts
    n_cmp = (tk - CMP_BLOCK) // CMP_STRIDE + 1
    assert n_cmp <= past // CMP_STRIDE - 1 + 1 and (n_cmp - 1) * CMP_STRIDE + CMP_BLOCK <= past
    nsb = -(-tk // SEL_BLOCK)
    kvc = _compress_s_call(cache_cmp3, page_table, layer, pe, w1, w2, kn_g)
    o_cmp, selw = _cmp_s_call(q2, kvc, bias_tabs["cmp"], n_cmp, nsb, ts, past)
    selt = jnp.transpose(selw, (0, 3, 1, 2))
    selt = jnp.broadcast_to(selt[:, :, :, None, :], selt.shape[:3] + (HPG, ts)).reshape(bsz, -1, 1, ncol)
    nblk = past // SEL_BLOCK
    assert nsb == nblk + 1
    selp, seln = selt[:, :nblk], selt[:, nblk]
    to_seq = lambda a: jnp.pad(jnp.transpose(a.reshape(ts, bsz, 2 * GH), (1, 0, 2)), ((0, 0), (0, 8 - ts), (0, 0)))
    kng_col = jnp.tile(kn_g, KV_GROUPS).reshape(GH, 1)
    eexp = np.zeros((GH, ncol), np.float32)
    for gi in range(KV_GROUPS):
        eexp[gi * HEAD_DIM:(gi + 1) * HEAD_DIM, gi * HPG * ts:(gi + 1) * HPG * ts] = 1.0
    eexp = jnp.asarray(eexp, BF16)
    o_sel = _sel_s_call(cache_sel3, page_table, layer, to_seq(sel_rows), qblk, kng_col, eexp, bias_tabs["sel"],
                        bias_tabs["new"], selp, seln, ts)
    o_win, new_win = _win_s_call(state_win4, layer, to_seq(win_rows), qblk, kng_col, eexp, bias_tabs["win"],
                                 bias_tabs["new"], ts, past)
    back = lambda o: jnp.transpose(o.reshape(bsz, KV_GROUPS, HPG, ts, HEAD_DIM), (3, 0, 1, 2, 4)).reshape(1, R, nq)
    x = _combine_call(x, mods[5], back(o_cmp), back(o_sel), back(o_win), gates, w_out, R)
    rows_out = lambda a: jnp.transpose(a.reshape(ts, bsz, 2, KV_GROUPS, HEAD_DIM), (1, 0, 2, 3, 4))
    return x, rows_out(cmp_rows), rows_out(sel_rows), new_win[0].reshape(bsz, -1, 2, KV_GROUPS, HEAD_DIM)


def kernel(x_prompt, x_sample, c_prompt, c_sample, cache_cmp_kv, cache_sel_kv, page_table, state_win_kv, state_conv, ada_w, ada_b, norm_g, ffn_w1, ffn_w2, even_w_in, even_w_out, gmlp_v_g, gmlp_ws, gmlp_bs, conv_w, conv_b, conv_ln_g, conv_ln_b, nsa_w_in, nsa_w_out, q_norm_g, k_norm_g, cmp_pe, cmp_w1, cmp_w2, rel_bias):
    B, T, D = x_prompt.shape
    Bs, Ts, _ = x_sample.shape
    depth = ada_w.shape[0]
    n_odd = nsa_w_in.shape[0]
    tm = min(512, T)
    Rs = Ts * Bs

    ada = _ada_call(jnp.concatenate([c_prompt, c_sample], axis=0), ada_w, ada_b)
    xp = x_prompt
    xs = jnp.transpose(x_sample, (1, 0, 2)).reshape(1, Rs, D)
    page_size = cache_cmp_kv.shape[1]
    cache_cmp3, cache_sel3 = cache_cmp_kv, cache_sel_kv
    state_win4 = state_win_kv.reshape(state_win_kv.shape[:3] + (2 * GH,))
    past = page_table.shape[1] * page_size
    bias_tabs = _sample_bias_tables(rel_bias, Ts, past, state_win_kv.shape[2], past // CMP_STRIDE)
    pad_cols = (-nsa_w_in.shape[2] + C_HEADS * HEAD_DIM + 6 * GH + LANES)

    cmp_p, cmp_s, sel_p, sel_s, win_p, win_s, conv_p, conv_s, v_s = ([] for _ in range(9))
    for l in range(depth):
        mp = [ada[l, :B, k * D:(k + 1) * D].reshape(B, 1, D) for k in range(9)]
        ms = [jnp.tile(ada[l, B:, k * D:(k + 1) * D], (Ts, 1)).reshape(1, Rs, D) for k in range(9)]
        w1 = ffn_w1[l].astype(BF16)
        w2 = ffn_w2[l].astype(BF16)
        xp = _ffn_call(xp, mp[0], mp[1], mp[2], norm_g[l, 0], w1[0], w2[0], tm)
        xs = _ffn_call(xs, ms[0], ms[1], ms[2], norm_g[l, 0], w1[0], w2[0], Rs)
        if l % 2 == 0:
            e = l // 2
            w_in = even_w_in[e].astype(BF16)
            w_out = even_w_out[e].astype(BF16)
            prm = (gmlp_v_g[e], gmlp_ws[e], gmlp_bs[e])
            cprm = (conv_w[e], conv_b[e], conv_ln_g[e], conv_ln_b[e])
            xp, cst_p = _even_call(xp, mp[3], mp[4], mp[5], norm_g[l, 1], w_in, w_out, *prm, *cprm, tm)
            xs, cst_s, vrow = _even_s_call(xs, ms[3], ms[4], ms[5], norm_g[l, 1], w_in, w_out, *prm,
                                           state_conv[e], *cprm, Ts, Bs)
            conv_p.append(cst_p)
            conv_s.append(cst_s)
            v_s.append(vrow)
        else:
            o = l // 2
            w_in_pad = jnp.pad(nsa_w_in[o], ((0, 0), (0, pad_cols))).astype(BF16)
            w_out = nsa_w_out[o].astype(BF16)
            prm = (norm_g[l, 1], w_in_pad, w_out, q_norm_g[o], k_norm_g[o], cmp_pe[o], cmp_w1[o], cmp_w2[o])
            xp, rc_p, rs_p, w_p = _nsa_prompt(xp, mp, *prm, rel_bias, tm)
            xs, rc_s, rs_s, w_s = _nsa_sample(xs, ms, *prm, bias_tabs, cache_cmp3, cache_sel3, page_table, state_win4,
                                              o, Ts, Bs)
            kv6 = lambda a: a.reshape(B, -1, 2, KV_GROUPS, HEAD_DIM)
            cmp_p.append(kv6(rc_p))
            sel_p.append(kv6(rs_p))
            win_p.append(kv6(w_p))
            cmp_s.append(rc_s)
            sel_s.append(rs_s)
            win_s.append(w_s)
        xp = _ffn_call(xp, mp[6], mp[7], mp[8], norm_g[l, 2], w1[1], w2[1], tm)
        xs = _ffn_call(xs, ms[6], ms[7], ms[8], norm_g[l, 2], w1[1], w2[1], Rs)
    y_sample = jnp.transpose(xs.reshape(Ts, Bs, D), (1, 0, 2))
    return (xp, y_sample, jnp.stack(cmp_p, axis=2), jnp.stack(cmp_s, axis=2), jnp.stack(sel_p, axis=2),
            jnp.stack(sel_s, axis=2), jnp.stack(win_p, axis=0), jnp.stack(win_s, axis=0),
            jnp.stack(conv_p, axis=0), jnp.stack(conv_s, axis=0), jnp.stack(v_s, axis=0))
```

```python
import functools
import math

import numpy as np
import jax
import jax.numpy as jnp
from jax import lax
from jax.experimental import pallas as pl
from jax.experimental.pallas import tpu as pltpu

F32 = jnp.float32
BF16 = jnp.bfloat16

A_GROUPS = 8
CHUNK = 128
CONV_W = 31
C_HEADS = 16
KV_GROUPS = 4
HEAD_DIM = 64
HPG = C_HEADS // KV_GROUPS
GH = KV_GROUPS * HEAD_DIM
CMP_BLOCK = 32
CMP_STRIDE = 16
SEL_BLOCK = 64
N_SEL = 16
WINDOW = 512
N_BUCKETS = 32
MAX_DIST = 128
SCALE = HEAD_DIM ** -0.5
EPS = 1e-6
NEG = -1e30
FORCE = 1e6

VMEM_LIMIT_BYTES = 60 * 2 ** 20
LANES = 128
TQ = 128
TK = 256
STACK = 4
HIST = 32


def _params(*sem):
    return pltpu.CompilerParams(dimension_semantics=sem, vmem_limit_bytes=VMEM_LIMIT_BYTES)


def _dot(a, b):
    return jnp.dot(a, b, preferred_element_type=F32)


def _dot_nt(a, b):
    return lax.dot_general(a, b, (((1,), (1,)), ((), ())), preferred_element_type=F32)


def _dot_tn(a, b):
    return lax.dot_general(a, b, (((0,), (0,)), ((), ())), preferred_element_type=F32)


def _split_bf16(x):
    hi = x.astype(BF16)
    lo = (x - hi.astype(F32)).astype(BF16)
    return hi, lo


def _sigmoid(x):
    return 1.0 / (1.0 + jnp.exp(-x))


def _silu(x):
    return x * _sigmoid(x)


def _gelu(x):
    return 0.5 * x * (1.0 + jnp.tanh(math.sqrt(2.0 / math.pi) * (x + 0.044715 * (x * x * x))))


def _modulate(x, g, shift, scale):
    y = x * lax.rsqrt(jnp.mean(x * x, axis=-1, keepdims=True) + EPS)
    return (y * g) * (1.0 + scale) + shift


def _bucket_np(dist):
    exact = N_BUCKETS // 2
    d = np.maximum(dist, 0)
    df = np.maximum(d, 1).astype(np.float32)
    large = exact + (np.log(df / np.float32(exact)) / np.float32(math.log(MAX_DIST / exact))
                     * np.float32(N_BUCKETS - exact)).astype(np.int32)
    return np.where(d < exact, d, np.minimum(large, N_BUCKETS - 1)).astype(np.int32)


def _bucket_thresholds():
    b = _bucket_np(np.arange(4 * MAX_DIST))
    assert (np.diff(b) >= 0).all() and b[MAX_DIST] == N_BUCKETS - 1
    return [int(np.argmax(b >= k)) for k in range(N_BUCKETS)]


_THR = _bucket_thresholds()


def _bias_lut(d, rb_ref, h):
    val = jnp.full(d.shape, rb_ref[0, h], F32)
    for k in range(1, N_BUCKETS):
        val = jnp.where(d >= _THR[k], rb_ref[k, h], val)
    return val


_SMEM_SPEC = pl.BlockSpec(memory_space=pltpu.SMEM)


def _full_spec(shape):
    n = len(shape)
    return pl.BlockSpec(shape, lambda *_: (0,) * n)


def _tok_spec(tm, width):
    return pl.BlockSpec((1, tm, width), lambda b, i: (b, i, 0))


def _mod_spec(mod, tm):
    if mod.shape[1] == 1:
        return pl.BlockSpec((1, 1, mod.shape[2]), lambda b, i: (b, 0, 0))
    return pl.BlockSpec((1, tm, mod.shape[2]), lambda b, i: (b, i, 0))


def _ada_kernel(c_ref, w_ref, b_ref, o_ref):
    c = c_ref[...]
    o_ref[0] = _dot(_silu(c).astype(BF16), w_ref[0].astype(BF16)) + b_ref[0]


def _ada_call(c_all, ada_w, ada_b):
    L, D, N = ada_w.shape
    M = c_all.shape[0]
    tn = 1024
    return pl.pallas_call(
        _ada_kernel,
        grid=(L, N // tn),
        in_specs=[pl.BlockSpec((M, D), lambda l, j: (0, 0)),
                  pl.BlockSpec((1, D, tn), lambda l, j: (l, 0, j)),
                  pl.BlockSpec((1, 1, tn), lambda l, j: (l, 0, j))],
        out_specs=pl.BlockSpec((1, M, tn), lambda l, j: (l, 0, j)),
        out_shape=jax.ShapeDtypeStruct((L, M, N), F32),
        compiler_params=_params("arbitrary", "arbitrary"),
        name="ada",
    )(c_all, ada_w, ada_b.reshape(L, 1, N))


def _ffn_kernel(x_ref, sh_ref, sc_ref, gt_ref, g_ref, w1_ref, w2_ref, o_ref, acc_ref, *, ff, tf):
    x = x_ref[0]
    h = _modulate(x, g_ref[...], sh_ref[0], sc_ref[0]).astype(BF16)
    for c in range(ff // tf):
        a = _dot(h, w1_ref[:, c * tf:(c + 1) * tf])
        b = _dot(h, w1_ref[:, ff + c * tf:ff + (c + 1) * tf])
        t = (_silu(a) * b).astype(BF16)
        part = _dot(t, w2_ref[c * tf:(c + 1) * tf, :])
        if c == 0:
            acc_ref[...] = part
        else:
            acc_ref[...] += part
    o_ref[0] = x + 0.5 * gt_ref[0] * acc_ref[...]


def _ffn_call(x3, sh, sc, gt, g, w1, w2, tm):
    NB, R, D = x3.shape
    ff = w2.shape[0]
    tf = 256
    return pl.pallas_call(
        functools.partial(_ffn_kernel, ff=ff, tf=tf),
        grid=(NB, R // tm),
        in_specs=[_tok_spec(tm, D), _mod_spec(sh, tm), _mod_spec(sc, tm), _mod_spec(gt, tm),
                  _full_spec((1, D)), _full_spec(w1.shape), _full_spec(w2.shape)],
        out_specs=_tok_spec(tm, D),
        out_shape=jax.ShapeDtypeStruct(x3.shape, F32),
        scratch_shapes=[pltpu.VMEM((tm, D), F32)],
        compiler_params=_params("arbitrary", "arbitrary"),
        name="ffn",
    )(x3, sh, sc, gt, g.reshape(1, D), w1, w2)


def _even_kernel(x_ref, sh_ref, sc_ref, gt_ref, g_ref, win_ref, wout_ref, vg_ref, ws_ref, bs_ref,
                 cw_ref, cb_ref, lg_ref, lb_ref, o_ref, cs_ref, ext_ref, sa_ref, shf_ref, *, tm, aw):
    @pl.when(pl.program_id(1) == 0)
    def _():
        ext_ref[0:HIST, :] = jnp.zeros((HIST, ext_ref.shape[1]), F32)

    x = x_ref[0]
    h = _modulate(x, g_ref[...], sh_ref[0], sc_ref[0]).astype(BF16)
    z = _dot(h, win_ref[...])
    u = _gelu(z[:, 0:aw])
    gv = _gelu(z[:, aw:2 * aw])
    v = gv * lax.rsqrt(jnp.mean(gv * gv, axis=-1, keepdims=True) + EPS) * vg_ref[...]
    vb = v.astype(BF16)
    row = lax.broadcasted_iota(jnp.int32, (CHUNK, CHUNK), 0)
    col = lax.broadcasted_iota(jnp.int32, (CHUNK, CHUNK), 1)
    wm = [jnp.where(row >= col, ws_ref[gi], 0.0).astype(BF16) for gi in range(A_GROUPS)]
    gw = aw // A_GROUPS
    first_half = col < gw
    for c in range(tm // CHUNK):
        for q in range(aw // LANES):
            vq = vb[c * CHUNK:(c + 1) * CHUNK, q * LANES:(q + 1) * LANES]
            s0 = _dot(wm[2 * q], vq)
            s1 = _dot(wm[2 * q + 1], vq)
            sa_ref[c * CHUNK:(c + 1) * CHUNK, q * LANES:(q + 1) * LANES] = (
                jnp.where(first_half, s0, s1) + bs_ref[:, q * LANES:(q + 1) * LANES])
    a_out = u * sa_ref[...]
    glu = z[:, 2 * aw:3 * aw] * _sigmoid(z[:, 3 * aw:4 * aw])
    ext_ref[HIST:HIST + tm, :] = glu
    off = HIST - (CONV_W - 1)
    conv = jnp.broadcast_to(cb_ref[...], (tm, ext_ref.shape[1]))
    for r in range(8):
        na = (CONV_W - 1 - r) // 8 + 1
        nr = tm + 8 * (na - 1)
        shf_ref[0:nr, :] = ext_ref[pl.ds(off + r, nr), :]
        for a in range(na):
            conv = conv + shf_ref[8 * a:8 * a + tm, :] * cw_ref[8 * a + r:8 * a + r + 1, :]
    cs_ref[0] = ext_ref[pl.ds(tm + off, CONV_W - 1), :]
    ext_ref[0:HIST, :] = ext_ref[tm:tm + HIST, :]
    mu = jnp.mean(conv, axis=-1, keepdims=True)
    cc = conv - mu
    var = jnp.mean(cc * cc, axis=-1, keepdims=True)
    b_out = _silu(cc * lax.rsqrt(var + EPS) * lg_ref[...] + lb_ref[...])
    out = _dot(a_out.astype(BF16), wout_ref[0:aw, :]) + _dot(b_out.astype(BF16), wout_ref[aw:, :])
    o_ref[0] = x + gt_ref[0] * out


def _even_call(x3, sh, sc, gt, g, w_in, w_out, v_g, ws, bs, cw, cb, ln_g, ln_b, tm):
    B, T, D = x3.shape
    aw = v_g.shape[0]
    bw = cw.shape[1]
    assert aw == bw and aw // A_GROUPS * 2 == LANES and T % tm == 0 and tm % CHUNK == 0
    bs_exp = jnp.repeat(bs.T, aw // A_GROUPS, axis=1)
    cw_pad = jnp.pad(cw, ((0, 1), (0, 0)))
    row = lambda a: a.reshape(1, -1)
    return pl.pallas_call(
        functools.partial(_even_kernel, tm=tm, aw=aw),
        grid=(B, T // tm),
        in_specs=[_tok_spec(tm, D), _mod_spec(sh, tm), _mod_spec(sc, tm), _mod_spec(gt, tm),
                  _full_spec((1, D)), _full_spec(w_in.shape), _full_spec(w_out.shape),
                  _full_spec((1, aw)), _full_spec(ws.shape), _full_spec(bs_exp.shape),
                  _full_spec(cw_pad.shape), _full_spec((1, bw)), _full_spec((1, bw)), _full_spec((1, bw))],
        out_specs=[_tok_spec(tm, D), pl.BlockSpec((1, CONV_W - 1, bw), lambda b, i: (b, 0, 0))],
        out_shape=[jax.ShapeDtypeStruct(x3.shape, F32), jax.ShapeDtypeStruct((B, CONV_W - 1, bw), F32)],
        scratch_shapes=[pltpu.VMEM((tm + HIST, bw), F32), pltpu.VMEM((tm, aw), F32),
                        pltpu.VMEM((tm + HIST, bw), F32)],
        compiler_params=_params("arbitrary", "arbitrary"),
        name="even_prompt",
    )(x3, sh, sc, gt, row(g), w_in, w_out, row(v_g), ws, bs_exp, cw_pad, row(cb), row(ln_g), row(ln_b))


def _even_s_kernel(x_ref, sh_ref, sc_ref, gt_ref, g_ref, win_ref, wout_ref, vg_ref, coef_ref, bsa_ref,
                   st_ref, cw_ref, cb_ref, lg_ref, lb_ref, o_ref, cs_ref, v_ref, *, ts, bsz, aw):
    x = x_ref[0]
    h = _modulate(x, g_ref[...], sh_ref[0], sc_ref[0]).astype(BF16)
    z = _dot(h, win_ref[...])
    u = _gelu(z[:, 0:aw])
    gv = _gelu(z[:, aw:2 * aw])
    v = gv * lax.rsqrt(jnp.mean(gv * gv, axis=-1, keepdims=True) + EPS) * vg_ref[...]
    glu = z[:, 2 * aw:3 * aw] * _sigmoid(z[:, 3 * aw:4 * aw])
    hist = CONV_W - 1
    sl = lambda a, t: a[t * bsz:(t + 1) * bsz]
    a_parts, b_parts = [], []
    for t in range(ts):
        v_ref[t] = sl(v, t)
        s = bsa_ref[t]
        for j in range(t + 1):
            s = s + coef_ref[t, j] * sl(v, j)
        a_parts.append(sl(u, t) * s)
        conv = cb_ref[...]
        for m in range(t, hist):
            conv = conv + st_ref[m] * cw_ref[m - t:m - t + 1, :]
        for j in range(t + 1):
            conv = conv + sl(glu, j) * cw_ref[hist - t + j:hist - t + j + 1, :]
        mu = jnp.mean(conv, axis=-1, keepdims=True)
        cc = conv - mu
        var = jnp.mean(cc * cc, axis=-1, keepdims=True)
        b_parts.append(_silu(cc * lax.rsqrt(var + EPS) * lg_ref[...] + lb_ref[...]))
    for i in range(hist):
        cs_ref[i] = st_ref[i + ts] if i + ts < hist else sl(glu, i + ts - hist)
    a_out = jnp.concatenate(a_parts, axis=0).astype(BF16)
    b_out = jnp.concatenate(b_parts, axis=0).astype(BF16)
    out = _dot(a_out, wout_ref[0:aw, :]) + _dot(b_out, wout_ref[aw:, :])
    o_ref[0] = x + gt_ref[0] * out


def _even_s_call(x3, sh, sc, gt, g, w_in, w_out, v_g, ws, bs, state, cw, cb, ln_g, ln_b, ts, bsz):
    _, R, D = x3.shape
    aw = v_g.shape[0]
    bw = cw.shape[1]
    gw = aw // A_GROUPS
    hist = CONV_W - 1
    assert ts <= CHUNK and ts <= hist
    coef = jnp.repeat(jnp.transpose(ws[:, :ts, :ts], (1, 2, 0)), gw, axis=2).reshape(ts, ts, 1, aw)
    bsa = jnp.repeat(bs[:, :ts].T, gw, axis=1).reshape(ts, 1, aw)
    st = jnp.transpose(state, (1, 0, 2))
    row = lambda a: a.reshape(1, -1)
    args = (x3, sh, sc, gt, row(g), w_in, w_out, row(v_g), coef, bsa, st, cw, row(cb), row(ln_g), row(ln_b))
    x_new, cs, v = pl.pallas_call(
        functools.partial(_even_s_kernel, ts=ts, bsz=bsz, aw=aw),
        grid=(1,),
        in_specs=[_full_spec(a.shape) for a in args],
        out_specs=[_full_spec(x3.shape), _full_spec((hist, bsz, bw)), _full_spec((ts, bsz, aw))],
        out_shape=[jax.ShapeDtypeStruct(x3.shape, F32), jax.ShapeDtypeStruct((hist, bsz, bw), F32),
                   jax.ShapeDtypeStruct((ts, bsz, aw), F32)],
        compiler_params=_params("arbitrary"),
        name="even_sample",
    )(*args)
    return x_new, jnp.transpose(cs, (1, 0, 2)), jnp.transpose(v, (1, 0, 2))


def _nsa_proj_kernel(x_ref, sh_ref, sc_ref, g_ref, w_ref, q_ref, cmp_ref, sel_ref, win_ref, gate_ref, gatet_ref):
    x = x_ref[0]
    h = _modulate(x, g_ref[...], sh_ref[0], sc_ref[0]).astype(BF16)
    z = _dot(h, w_ref[...])
    nq = C_HEADS * HEAD_DIM
    q_ref[0] = z[:, 0:nq]
    cmp_ref[0] = z[:, nq:nq + 2 * GH]
    sel_ref[0] = z[:, nq + 2 * GH:nq + 4 * GH]
    win_ref[0] = z[:, nq + 4 * GH:nq + 6 * GH]
    gate = _sigmoid(z[:, nq + 6 * GH:nq + 6 * GH + LANES])
    gate_ref[0] = gate
    gatet_ref[0] = gate.T


def _nsa_proj_call(x3, sh, sc, g, w_in_pad, tm):
    NB, R, D = x3.shape
    nq = C_HEADS * HEAD_DIM
    widths = (nq, 2 * GH, 2 * GH, 2 * GH, LANES)
    return pl.pallas_call(
        _nsa_proj_kernel,
        grid=(NB, R // tm),
        in_specs=[_tok_spec(tm, D), _mod_spec(sh, tm), _mod_spec(sc, tm), _full_spec((1, D)),
                  _full_spec(w_in_pad.shape)],
        out_specs=[_tok_spec(tm, w) for w in widths] + [pl.BlockSpec((1, LANES, tm), lambda b, i: (b, 0, i))],
        out_shape=[jax.ShapeDtypeStruct((NB, R, w), F32) for w in widths]
        + [jax.ShapeDtypeStruct((NB, LANES, R), F32)],
        compiler_params=_params("arbitrary", "arbitrary"),
        name="nsa_proj",
    )(x3, sh, sc, g.reshape(1, D), w_in_pad)


def _headnorm_kernel(x_ref, g_ref, o_ref, *, width, scale):
    lane = lax.broadcasted_iota(jnp.int32, (1, LANES), 1)
    lo = lane < HEAD_DIM
    for c in range(width // LANES):
        x = x_ref[:, c * LANES:(c + 1) * LANES]
        sq = x * x
        s_lo = jnp.sum(jnp.where(lo, sq, 0.0), axis=-1, keepdims=True)
        s_hi = jnp.sum(jnp.where(lo, 0.0, sq), axis=-1, keepdims=True)
        ss = jnp.where(lo, s_lo, s_hi)
        y = x * lax.rsqrt(ss * (1.0 / HEAD_DIM) + EPS) * g_ref[...]
        if scale != 1.0:
            y = y * scale
        o_ref[:, c * LANES:(c + 1) * LANES] = y.astype(o_ref.dtype)


def _headnorm_call(x2, g, width, tr, scale=1.0):
    N = x2.shape[0]
    g2 = jnp.tile(g, LANES // HEAD_DIM).reshape(1, LANES)
    return pl.pallas_call(
        functools.partial(_headnorm_kernel, width=width, scale=scale),
        grid=(N // tr,),
        in_specs=[pl.BlockSpec((tr, width), lambda i: (i, 0)), _full_spec((1, LANES))],
        out_specs=pl.BlockSpec((tr, width), lambda i: (i, 0)),
        out_shape=jax.ShapeDtypeStruct((N, width), BF16),
        compiler_params=_params("arbitrary"),
        name="headnorm",
    )(x2, g2)


def _compress_core(load_rows, n, pe_ref, w1_ref, w2_ref, kng_ref, o_ref, is_k):
    hid = w2_ref.shape[1]
    nstk = CMP_STRIDE // STACK
    accs = [None] * KV_GROUPS
    ctop = jnp.zeros((1, hid), F32)
    cbot = jnp.zeros((1, hid), F32)
    pe = pe_ref[0].astype(BF16)
    gpl = LANES // HEAD_DIM
    for l4 in range(nstk):
        xs = [[load_rows(l4 * STACK + i, s) for s in range(KV_GROUPS // gpl)] for i in range(STACK)]
        w = w1_ref[0, l4]
        r = _dot(pe, w)
        ctop = ctop + r[l4:l4 + 1, 0:hid]
        cbot = cbot + r[nstk + l4:nstk + l4 + 1, hid:2 * hid]
        for gi in range(KV_GROUPS):
            lo = (gi % gpl) * HEAD_DIM
            xcat = jnp.concatenate([x[gi // gpl][:, lo:lo + HEAD_DIM] for x in xs], axis=1).astype(BF16)
            part = _dot(xcat, w)
            accs[gi] = part if accs[gi] is None else accs[gi] + part
    for gi in range(KV_GROUPS):
        a = accs[gi][:, 0:hid] + ctop
        b = pltpu.roll(accs[gi][:, hid:2 * hid] + cbot, n - 1, 0)
        y = _dot(_gelu(a + b).astype(BF16), w2_ref[0])
        yn = y * lax.rsqrt(jnp.mean(y * y, axis=-1, keepdims=True) + EPS) * kng_ref[...]
        o_ref[0, 0, :, gi * HEAD_DIM:(gi + 1) * HEAD_DIM] = jnp.where(is_k, yn, y).astype(o_ref.dtype)


def _compress_p_kernel(x_ref, pe_ref, w1_ref, w2_ref, kng_ref, o_ref, *, n):
    kv = pl.program_id(1)
    slabs = 2 * GH // LANES
    load = lambda l, s: x_ref[0, pl.ds(l * slabs + kv * (slabs // 2) + s, n, stride=CMP_STRIDE * slabs), :]
    _compress_core(load, n, pe_ref, w1_ref, w2_ref, kng_ref, o_ref, kv == 0)


def _compress_weights(pe, w1, w2):
    hid = w1.shape[-1]
    w1r = w1.reshape(2, CMP_BLOCK, HEAD_DIM, hid)
    pair = jnp.concatenate([w1r[:, :CMP_STRIDE], w1r[:, CMP_STRIDE:]], axis=-1)
    w1c = pair.reshape(2, CMP_STRIDE // STACK, STACK * HEAD_DIM, 2 * hid).astype(BF16)
    return pe.reshape(2, 2 * CMP_STRIDE // STACK, STACK * HEAD_DIM), w1c, w2.astype(BF16)


def _compress_p_call(cmp_rows, pe, w1, w2, kn_g):
    B, T, _ = cmp_rows.shape
    n = T // CMP_STRIDE
    pe, w1r, w2r = _compress_weights(pe, w1, w2)
    hid = w1.shape[-1]
    slabs = 2 * GH // LANES
    cmp_rows = cmp_rows.reshape(B, T * slabs, LANES)
    return pl.pallas_call(
        functools.partial(_compress_p_kernel, n=n),
        grid=(B, 2),
        in_specs=[pl.BlockSpec((1, T * slabs, LANES), lambda b, kv: (b, 0, 0)),
                  pl.BlockSpec((1,) + pe.shape[1:], lambda b, kv: (kv, 0, 0)),
                  pl.BlockSpec((1,) + w1r.shape[1:], lambda b, kv: (kv, 0, 0, 0)),
                  pl.BlockSpec((1, hid, HEAD_DIM), lambda b, kv: (kv, 0, 0)),
                  _full_spec((1, HEAD_DIM))],
        out_specs=pl.BlockSpec((1, 1, n, GH), lambda b, kv: (b, kv, 0, 0)),
        out_shape=jax.ShapeDtypeStruct((B, 2, n, GH), BF16),
        compiler_params=_params("arbitrary", "arbitrary"),
        name="compress_prompt",
    )(cmp_rows, pe, w1r, w2r, kn_g.reshape(1, HEAD_DIM))


def _compress_s_kernel(pt_ref, cache_ref, pe_ref, w1_ref, w2_ref, kng_ref, o_ref, buf_ref, sem_ref,
                       *, n, n_pages, page_size, slab0):
    b = pl.program_id(0)
    kv = pl.program_id(1)
    step = b * 2 + kv
    n_steps = pl.num_programs(0) * 2
    nsl = GH // LANES

    def fetch(s, start):
        sb = s // 2
        slot = s % 2
        for p in range(n_pages):
            for sl in range(nsl):
                cp = pltpu.make_async_copy(
                    cache_ref.at[pt_ref[sb, p], :, slab0 + (s % 2) * nsl + sl, :],
                    buf_ref.at[slot, sl, pl.ds(p * page_size, page_size), :],
                    sem_ref.at[slot])
                cp.start() if start else cp.wait()

    @pl.when(step == 0)
    def _():
        fetch(step, True)

    @pl.when(step + 1 < n_steps)
    def _():
        fetch(step + 1, True)

    fetch(step, False)
    slot = step % 2
    load = lambda l, s: buf_ref[slot, s, pl.ds(l, n, stride=CMP_STRIDE), :]
    _compress_core(load, n, pe_ref, w1_ref, w2_ref, kng_ref, o_ref, kv == 0)


def _compress_s_call(cache3, page_table, layer, pe, w1, w2, kn_g):
    Bs, n_pages = page_table.shape
    page_size = cache3.shape[1]
    past = n_pages * page_size
    n = past // CMP_STRIDE
    pe, w1r, w2r = _compress_weights(pe, w1, w2)
    hid = w1.shape[-1]
    grid_spec = pltpu.PrefetchScalarGridSpec(
        num_scalar_prefetch=1,
        grid=(Bs, 2),
        in_specs=[pl.BlockSpec(memory_space=pl.ANY),
                  pl.BlockSpec((1,) + pe.shape[1:], lambda b, kv, pt: (kv, 0, 0)),
                  pl.BlockSpec((1,) + w1r.shape[1:], lambda b, kv, pt: (kv, 0, 0, 0)),
                  pl.BlockSpec((1, hid, HEAD_DIM), lambda b, kv, pt: (kv, 0, 0)),
                  pl.BlockSpec((1, HEAD_DIM), lambda b, kv, pt: (0, 0))],
        out_specs=pl.BlockSpec((1, 1, n, GH), lambda b, kv, pt: (b, kv, 0, 0)),
        scratch_shapes=[pltpu.VMEM((2, GH // LANES, past, LANES), F32), pltpu.SemaphoreType.DMA((2,))],
    )
    cache4 = cache3.reshape(cache3.shape[0], page_size, -1, LANES)
    return pl.pallas_call(
        functools.partial(_compress_s_kernel, n=n, n_pages=n_pages, page_size=page_size,
                          slab0=layer * 2 * GH // LANES),
        grid_spec=grid_spec,
        out_shape=jax.ShapeDtypeStruct((Bs, 2, n, GH), BF16),
        compiler_params=_params("arbitrary", "arbitrary"),
        name="compress_sample",
    )(page_table, cache4, pe, w1r, w2r, kn_g.reshape(1, HEAD_DIM))


def _select(imp, tpos, nsb):
    j = lax.broadcasted_iota(jnp.int32, imp.shape, 1)
    cur = tpos // SEL_BLOCK
    valid = (j * SEL_BLOCK <= tpos) & (j < nsb)
    forced = (j == 0) | (j == cur) | (j == cur - 1)
    impf = jnp.where(valid, jnp.where(forced, FORCE, imp), NEG)
    rank = jnp.zeros(imp.shape, jnp.int32)
    for jp in range(nsb):
        c = impf[:, jp:jp + 1]
        beats = (c > impf) | ((c == impf) & (jp < j))
        rank = rank + beats.astype(jnp.int32)
    return (rank < min(N_SEL, nsb)) & valid


def _select_t(imp, tpos, nsb):
    j = lax.broadcasted_iota(jnp.int32, imp.shape, 0)
    cur = tpos // SEL_BLOCK
    valid = (j * SEL_BLOCK <= tpos) & (j < nsb)
    forced = (j == 0) | (j == cur) | (j == cur - 1)
    impf = jnp.where(valid, jnp.where(forced, FORCE, imp), NEG)
    rank = jnp.zeros(imp.shape, jnp.int32)
    for jp in range(nsb):
        c = impf[jp:jp + 1, :]
        beats = (c > impf) | ((c == impf) & (jp < j))
        rank = rank + beats.astype(jnp.int32)
    return (rank < min(N_SEL, nsb)) & valid


def _overlap_np(n_cmp, nsb, rows, cols):
    cs = np.arange(n_cmp)[:, None] * CMP_STRIDE
    ss = np.arange(nsb)[None, :] * SEL_BLOCK
    ov = np.clip(np.minimum(cs + CMP_BLOCK, ss + SEL_BLOCK) - np.maximum(cs, ss), 0, None).astype(np.float32) / CMP_STRIDE
    out = np.zeros((rows, cols), np.float32)
    out[:n_cmp, :nsb] = ov
    return out


def _stacked_qt(qt, gi):
    return jnp.concatenate([qt[(gi * HPG + p) * HEAD_DIM:(gi * HPG + p + 1) * HEAD_DIM] for p in range(HPG)],
                           axis=1).astype(BF16)


def _cmp_p_kernel(rb_ref, q_ref, kv_ref, ovt_ref, gt_ref, o_ref, sel_ref, bias_ref, ot_ref, *, n_cmp, nsb):
    qb = pl.program_id(0)
    npad = kv_ref.shape[2]
    cols = HPG * TQ

    @pl.when(pl.program_id(1) == 0)
    def _():
        n = lax.broadcasted_iota(jnp.int32, (npad, TQ), 0)
        t = qb * TQ + lax.broadcasted_iota(jnp.int32, (npad, TQ), 1)
        d = t - (n * CMP_STRIDE + CMP_BLOCK - 1)
        for h in range(C_HEADS):
            bias_ref[:, h * TQ:(h + 1) * TQ] = _bias_lut(d, rb_ref, h) * LOG2E

    qt = q_ref[0].astype(F32).T
    n4 = lax.broadcasted_iota(jnp.int32, (npad, cols), 0)
    t4 = qb * TQ + (lax.broadcasted_iota(jnp.int32, (npad, cols), 1) & (TQ - 1))
    mask = ((t4 - (n4 * CMP_STRIDE + CMP_BLOCK - 1)) >= 0) & (n4 < n_cmp)
    kc = kv_ref[0, 0]
    vct = kv_ref[0, 1].astype(F32).T
    tpos = qb * TQ + lax.broadcasted_iota(jnp.int32, (nsb, TQ), 1)
    for gi in range(KV_GROUPS):
        s = _dot(kc[:, gi * HEAD_DIM:(gi + 1) * HEAD_DIM], _stacked_qt(qt, gi)) + bias_ref[:, gi * cols:(gi + 1) * cols]
        s = jnp.where(mask, s, NEG)
        m = jnp.max(s, axis=0, keepdims=True)
        e = jnp.where(mask, jnp.exp2(s - m), 0.0)
        prob = e * (1.0 / jnp.maximum(jnp.sum(e, axis=0, keepdims=True), 1e-30))
        o = _dot(vct[gi * HEAD_DIM:(gi + 1) * HEAD_DIM].astype(BF16), prob.astype(BF16))
        psum = prob[:, 0:TQ]
        for p in range(HPG):
            h = gi * HPG + p
            ot_ref[h * HEAD_DIM:(h + 1) * HEAD_DIM, :] = o[:, p * TQ:(p + 1) * TQ] * gt_ref[0, h:h + 1, :]
            if p:
                psum = psum + prob[:, p * TQ:(p + 1) * TQ]
        hi, lo = _split_bf16(psum)
        imp = _dot(ovt_ref[...], hi) + _dot(ovt_ref[...], lo)
        sel_ref[0, gi] = _select_t(imp, tpos, nsb).astype(F32)
    o_ref[0] = ot_ref[...].T


def _cmp_p_call(qn, kvc, rel_bias, gates_t, n_cmp):
    B, T, nq = qn.shape
    npad = kvc.shape[2]
    nsb = -(-T // SEL_BLOCK)
    assert nsb % 8 == 0
    ovt = jnp.asarray(_overlap_np(n_cmp, nsb, npad, nsb).T, BF16)
    return pl.pallas_call(
        functools.partial(_cmp_p_kernel, n_cmp=n_cmp, nsb=nsb),
        grid=(T // TQ, B),
        in_specs=[_SMEM_SPEC,
                  pl.BlockSpec((1, TQ, nq), lambda i, b: (b, i, 0)),
                  pl.BlockSpec((1, 2, npad, GH), lambda i, b: (b, 0, 0, 0)),
                  _full_spec(ovt.shape),
                  pl.BlockSpec((1, LANES, TQ), lambda i, b: (b, 0, i))],
        out_specs=[pl.BlockSpec((1, TQ, nq), lambda i, b: (b, i, 0)),
                   pl.BlockSpec((1, KV_GROUPS, nsb, TQ), lambda i, b: (b, 0, 0, i))],
        out_shape=[jax.ShapeDtypeStruct((B, T, nq), F32),
                   jax.ShapeDtypeStruct((B, KV_GROUPS, nsb, T), F32)],
        scratch_shapes=[pltpu.VMEM((npad, C_HEADS * TQ), F32), pltpu.VMEM((nq, TQ), F32)],
        compiler_params=_params("arbitrary", "arbitrary"),
        name="cmp_attn_prompt",
    )(rel_bias, qn, kvc, ovt, gates_t)


ONES_ROWS = 16


LOG2E = math.log2(math.e)


def _pattn_kernel(qb_ref, kb_ref, fl_ref, rb_ref, q_ref, k_ref, v_ref, *rest, mode, gate_row0, n_tiles):
    if mode == "sel":
        sel_ref, gt_ref, o_ref, qst_ref, m_ref, acc_ref, bias_ref, ot_ref = rest
    else:
        gt_ref, o_ref, qst_ref, m_ref, acc_ref, bias_ref, ot_ref = rest
    i = pl.program_id(1)
    qb = qb_ref[i]
    kb = kb_ref[i]
    cols = HPG * TQ

    @pl.when((pl.program_id(0) == 0) & (i == 0))
    def _():
        row = lax.broadcasted_iota(jnp.int32, (TK, TQ), 0)
        col = lax.broadcasted_iota(jnp.int32, (TK, TQ), 1)
        for oi in range(n_tiles):
            d = oi * TQ + col - row
            vis = d >= 0
            if mode == "win":
                vis = vis & (d < WINDOW)
            for h in range(C_HEADS):
                bias_ref[oi, :, h * TQ:(h + 1) * TQ] = jnp.where(vis, _bias_lut(d, rb_ref, h) * LOG2E, NEG)

    @pl.when((fl_ref[i] & 1) == 1)
    def _():
        qt = q_ref[0].astype(F32).T
        for gi in range(KV_GROUPS):
            qst_ref[gi] = _stacked_qt(qt, gi)
        m_ref[...] = jnp.full(m_ref.shape, NEG, F32)
        acc_ref[...] = jnp.zeros(acc_ref.shape, F32)

    off = qb * TQ - kb * TK
    oi = jnp.minimum(off // TQ, n_tiles - 1)
    k = k_ref[0]
    vt = v_ref[0].T
    ones = jnp.ones((ONES_ROWS, TK), BF16)
    ss, m_olds, m_news = [], [], []
    for gi in range(KV_GROUPS):
        s = _dot(k[:, gi * HEAD_DIM:(gi + 1) * HEAD_DIM], qst_ref[gi]) + bias_ref[oi, :, gi * cols:(gi + 1) * cols]
        if mode == "sel":
            sm = sel_ref[0, gi, 0]
            blk = jnp.concatenate([jnp.broadcast_to(sm[r:r + 1], (SEL_BLOCK, TQ)) for r in range(TK // SEL_BLOCK)], axis=0)
            s = jnp.where(jnp.concatenate([blk > 0.5] * HPG, axis=1), s, NEG)
        m_old = m_ref[gi]
        m_olds.append(m_old)
        m_news.append(jnp.maximum(m_old, jnp.max(s, axis=0, keepdims=True)))
        ss.append(s)
    es = [jnp.exp2(ss[gi] - m_news[gi]).astype(BF16) for gi in range(KV_GROUPS)]
    pvs = []
    for gi in range(KV_GROUPS):
        vext = jnp.concatenate([vt[gi * HEAD_DIM:(gi + 1) * HEAD_DIM].astype(BF16), ones], axis=0)
        pvs.append(_dot(vext, es[gi]))
    for gi in range(KV_GROUPS):
        alpha = jnp.exp2(m_olds[gi] - m_news[gi])
        acc_ref[gi] = alpha * acc_ref[gi] + pvs[gi]
        m_ref[gi] = m_news[gi]

    @pl.when((fl_ref[i] & 2) == 2)
    def _():
        for gi in range(KV_GROUPS):
            a = acc_ref[gi]
            o = a[0:HEAD_DIM] * (1.0 / jnp.maximum(a[HEAD_DIM:HEAD_DIM + 1], 1e-30))
            for p in range(HPG):
                h = gi * HPG + p
                ot_ref[h * HEAD_DIM:(h + 1) * HEAD_DIM, :] = (
                    o[:, p * TQ:(p + 1) * TQ] * gt_ref[0, gate_row0 + h:gate_row0 + h + 1, :])
        o_ref[0] = ot_ref[...].T


def _pair_tables(nq, mode):
    qbs, kbs, fls = [], [], []
    for qb in range(nq):
        hi = (qb * TQ + TQ - 1) // TK
        lo = 0 if mode == "sel" else max((qb * TQ - WINDOW + 1) // TK, 0)
        for kb in range(lo, hi + 1):
            qbs.append(qb)
            kbs.append(kb)
            fls.append((1 if kb == lo else 0) | (2 if kb == hi else 0))
    return tuple(np.array(a, np.int32) for a in (qbs, kbs, fls))


def _pattn_call(qn, kn, rows, selmask, gates_t, rel_bias, mode):
    B, T, nq_w = qn.shape
    assert T % TK == 0 and TK % TQ == 0
    qbs, kbs, fls = _pair_tables(T // TQ, mode)
    n_tiles = int(np.max(qbs * TQ - kbs * TK)) // TQ + 1
    if mode == "sel":
        n_tiles = min(n_tiles, (MAX_DIST + TK - 1) // TQ + 2)
        assert (n_tiles - 1) * TQ - (TK - 1) >= MAX_DIST or n_tiles == int(np.max(qbs * TQ - kbs * TK)) // TQ + 1
    in_specs = [_SMEM_SPEC,
                pl.BlockSpec((1, TQ, nq_w), lambda b, i, qb, kb, fl: (b, qb[i], 0)),
                pl.BlockSpec((1, TK, GH), lambda b, i, qb, kb, fl: (b, kb[i], 0)),
                pl.BlockSpec((1, TK, GH), lambda b, i, qb, kb, fl: (b, kb[i], 1))]
    args = [rel_bias, qn, kn, rows]
    if mode == "sel":
        nblk = TK // SEL_BLOCK
        in_specs.append(pl.BlockSpec((1, KV_GROUPS, 1, nblk, TQ), lambda b, i, qb, kb, fl: (b, 0, kb[i], 0, qb[i])))
        args.append(selmask.reshape(B, KV_GROUPS, T // TK, nblk, T))
    in_specs.append(pl.BlockSpec((1, LANES, TQ), lambda b, i, qb, kb, fl: (b, 0, qb[i])))
    args.append(gates_t)
    cols = HPG * TQ
    grid_spec = pltpu.PrefetchScalarGridSpec(
        num_scalar_prefetch=3,
        grid=(B, int(qbs.shape[0])),
        in_specs=in_specs,
        out_specs=pl.BlockSpec((1, TQ, nq_w), lambda b, i, qb, kb, fl: (b, qb[i], 0)),
        scratch_shapes=[pltpu.VMEM((KV_GROUPS, HEAD_DIM, cols), BF16),
                        pltpu.VMEM((KV_GROUPS, 1, cols), F32),
                        pltpu.VMEM((KV_GROUPS, HEAD_DIM + ONES_ROWS, cols), F32),
                        pltpu.VMEM((n_tiles, TK, C_HEADS * TQ), F32),
                        pltpu.VMEM((nq_w, TQ), F32)],
    )
    return pl.pallas_call(
        functools.partial(_pattn_kernel, mode=mode, gate_row0=C_HEADS * (1 if mode == "sel" else 2), n_tiles=n_tiles),
        grid_spec=grid_spec,
        out_shape=jax.ShapeDtypeStruct((B, T, nq_w), F32),
        compiler_params=_params("arbitrary", "arbitrary"),
        name="attn_prompt_" + mode,
    )(jnp.asarray(qbs), jnp.asarray(kbs), jnp.asarray(fls), *args)


def _combine_kernel(x_ref, gt_ref, oc_ref, os_ref, ow_ref, gate_ref, w_ref, o_ref, mix_ref):
    gate = gate_ref[0]
    for h in range(C_HEADS):
        hs = slice(h * HEAD_DIM, (h + 1) * HEAD_DIM)
        mix_ref[:, hs] = (gate[:, h:h + 1] * oc_ref[0, :, hs]
                          + gate[:, C_HEADS + h:C_HEADS + h + 1] * os_ref[0, :, hs]
                          + gate[:, 2 * C_HEADS + h:2 * C_HEADS + h + 1] * ow_ref[0, :, hs]).astype(BF16)
    o_ref[0] = x_ref[0] + gt_ref[0] * _dot(mix_ref[...], w_ref[...])


def _combine_call(x3, gt, oc, os_, ow, gates, w_out, tm):
    NB, R, D = x3.shape
    nq = C_HEADS * HEAD_DIM
    return pl.pallas_call(
        _combine_kernel,
        grid=(NB, R // tm),
        in_specs=[_tok_spec(tm, D), _mod_spec(gt, tm), _tok_spec(tm, nq), _tok_spec(tm, nq), _tok_spec(tm, nq),
                  _tok_spec(tm, LANES), _full_spec(w_out.shape)],
        out_specs=_tok_spec(tm, D),
        out_shape=jax.ShapeDtypeStruct(x3.shape, F32),
        scratch_shapes=[pltpu.VMEM((tm, nq), BF16)],
        compiler_params=_params("arbitrary", "arbitrary"),
        name="nsa_combine",
    )(x3, gt, oc, os_, ow, gates, w_out)


def _sum_proj_kernel(x_ref, gt_ref, oc_ref, os_ref, ow_ref, w_ref, o_ref):
    mix = (oc_ref[0] + os_ref[0] + ow_ref[0]).astype(BF16)
    o_ref[0] = x_ref[0] + gt_ref[0] * _dot(mix, w_ref[...])


def _sum_proj_call(x3, gt, oc, os_, ow, w_out, tm):
    NB, R, D = x3.shape
    nq = C_HEADS * HEAD_DIM
    return pl.pallas_call(
        _sum_proj_kernel,
        grid=(NB, R // tm),
        in_specs=[_tok_spec(tm, D), _mod_spec(gt, tm), _tok_spec(tm, nq), _tok_spec(tm, nq), _tok_spec(tm, nq),
                  _full_spec(w_out.shape)],
        out_specs=_tok_spec(tm, D),
        out_shape=jax.ShapeDtypeStruct(x3.shape, F32),
        compiler_params=_params("arbitrary", "arbitrary"),
        name="nsa_out_proj",
    )(x3, gt, oc, os_, ow, w_out)


def _lut_kernel(rb_ref, d_ref, h_ref, o_ref):
    d = d_ref[...]
    hh = h_ref[...]
    out = jnp.zeros(d.shape, F32)
    for h in range(C_HEADS):
        out = jnp.where(hh == h, _bias_lut(d, rb_ref, h), out)
    o_ref[...] = out


def _lut_call(rel_bias, dist, head):
    dist = np.ascontiguousarray(np.broadcast_to(dist, head.shape)).astype(np.int32)
    head = np.ascontiguousarray(head).astype(np.int32)
    return pl.pallas_call(
        _lut_kernel,
        grid=(1,),
        in_specs=[_SMEM_SPEC, _full_spec(dist.shape), _full_spec(head.shape)],
        out_specs=_full_spec(dist.shape),
        out_shape=jax.ShapeDtypeStruct(dist.shape, F32),
        compiler_params=_params("arbitrary"),
        name="bias_lut",
    )(rel_bias, jnp.asarray(dist), jnp.asarray(head))


def _cmp_s_kernel(q_ref, kv_ref, bias_ref, ov_ref, o_ref, sel_ref, *, n_cmp, nsb, ts, past):
    npad = kv_ref.shape[2]
    rows = HPG * ts
    t = past + lax.broadcasted_iota(jnp.int32, (rows, npad), 0) % ts
    nidx = lax.broadcasted_iota(jnp.int32, (rows, npad), 1)
    mask = ((t - (nidx * CMP_STRIDE + CMP_BLOCK - 1)) >= 0) & (nidx < n_cmp)
    tpos = past + lax.broadcasted_iota(jnp.int32, (ts, 1), 0)
    for gi in range(KV_GROUPS):
        kc = kv_ref[0, 0, :, gi * HEAD_DIM:(gi + 1) * HEAD_DIM]
        vc = kv_ref[0, 1, :, gi * HEAD_DIM:(gi + 1) * HEAD_DIM]
        s = _dot_nt(q_ref[0, gi], kc) + bias_ref[gi]
        s = jnp.where(mask, s, NEG)
        m = jnp.max(s, axis=-1, keepdims=True)
        e = jnp.where(mask, jnp.exp(s - m), 0.0)
        prob = e / jnp.maximum(jnp.sum(e, axis=-1, keepdims=True), 1e-30)
        o_ref[0, gi] = _dot(prob.astype(BF16), vc)
        hi, lo = _split_bf16(prob)
        imp16 = _dot(hi, ov_ref[...]) + _dot(lo, ov_ref[...])
        imp = imp16[0:ts]
        for p in range(1, HPG):
            imp = imp + imp16[p * ts:(p + 1) * ts]
        sel_ref[0, gi] = _select(imp, tpos, nsb).astype(F32)


def _cmp_s_bias(rel_bias, npad, ts, past):
    rows = HPG * ts
    r = np.arange(KV_GROUPS * rows)
    dist = (past + r % ts)[:, None] - (np.arange(npad)[None, :] * CMP_STRIDE + CMP_BLOCK - 1)
    head = np.broadcast_to((r // ts)[:, None], dist.shape)
    return _lut_call(rel_bias, dist, head).reshape(KV_GROUPS, rows, npad)


def _cmp_s_call(q2, kvc, bias, n_cmp, nsb, ts, past):
    Bs = q2.shape[0]
    npad = kvc.shape[2]
    rows = HPG * ts
    lpad = -(-nsb // LANES) * LANES
    ov = jnp.asarray(_overlap_np(n_cmp, nsb, npad, lpad), BF16)
    return pl.pallas_call(
        functools.partial(_cmp_s_kernel, n_cmp=n_cmp, nsb=nsb, ts=ts, past=past),
        grid=(Bs,),
        in_specs=[pl.BlockSpec((1, KV_GROUPS, rows, HEAD_DIM), lambda b: (b, 0, 0, 0)),
                  pl.BlockSpec((1, 2, npad, GH), lambda b: (b, 0, 0, 0)),
                  _full_spec(bias.shape), _full_spec(ov.shape)],
        out_specs=[pl.BlockSpec((1, KV_GROUPS, rows, HEAD_DIM), lambda b: (b, 0, 0, 0)),
                   pl.BlockSpec((1, KV_GROUPS, ts, lpad), lambda b: (b, 0, 0, 0))],
        out_shape=[jax.ShapeDtypeStruct((Bs, KV_GROUPS, rows, HEAD_DIM), F32),
                   jax.ShapeDtypeStruct((Bs, KV_GROUPS, ts, lpad), F32)],
        compiler_params=_params("arbitrary"),
        name="cmp_attn_sample",
    )(q2, kvc, bias, ov)


def _decode_core(kp, vp, kn, vn, qblk, kng_col, eexp, bias_p, bias_n, mask_p, mask_n):
    qb = (qblk * kng_col).astype(BF16)

    def logits(k, bias):
        hi, lo = _split_bf16(k * k)
        ss = _dot(hi, eexp) + _dot(lo, eexp)
        r = lax.rsqrt(ss * (1.0 / HEAD_DIM) + EPS)
        return _dot(k.astype(BF16), qb) * r + bias

    lp = jnp.where(mask_p, logits(kp, bias_p), NEG)
    ln = jnp.where(mask_n, logits(kn, bias_n), NEG)
    m = jnp.maximum(jnp.max(lp, axis=0, keepdims=True), jnp.max(ln, axis=0, keepdims=True))
    ep = jnp.where(mask_p, jnp.exp(lp - m), 0.0)
    en = jnp.where(mask_n, jnp.exp(ln - m), 0.0)
    denom = jnp.sum(ep, axis=0, keepdims=True) + jnp.sum(en, axis=0, keepdims=True)
    inv = 1.0 / jnp.maximum(denom, 1e-30)
    of = _dot_tn((ep * inv).astype(BF16), vp.astype(BF16)) + _dot_tn((en * inv).astype(BF16), vn.astype(BF16))
    ncol = of.shape[0]
    per = ncol // KV_GROUPS
    rg = lax.broadcasted_iota(jnp.int32, (ncol, HEAD_DIM), 0) // per
    o = jnp.zeros((ncol, HEAD_DIM), F32)
    for gi in range(KV_GROUPS):
        o = o + jnp.where(rg == gi, of[:, gi * HEAD_DIM:(gi + 1) * HEAD_DIM], 0.0)
    return o


def _new_key_mask(ts, ncol, rows):
    jn = lax.broadcasted_iota(jnp.int32, (rows, ncol), 0)
    tn = lax.broadcasted_iota(jnp.int32, (rows, ncol), 1) % ts
    return (jn <= tn) & (jn < ts)


def _page_copy(cache_ref, buf_ref, sem_ref, page, slot, p, col, width, page_size):
    return pltpu.make_async_copy(
        cache_ref.at[page, :, pl.ds(col, width)],
        buf_ref.at[slot, pl.ds(p * page_size, page_size), :],
        sem_ref.at[slot])


def _sel_s_kernel(pt_ref, cache_ref, new_ref, q_ref, kng_ref, eexp_ref, bp_ref, bn_ref, selp_ref, seln_ref,
                  o_ref, buf_ref, sem_ref, *, n_pages, page_size, col0, ts):
    b = pl.program_id(0)
    nb = pl.num_programs(0)
    past = n_pages * page_size
    ncol = q_ref.shape[2]

    def fetch(sb, start):
        slot = sb % 2
        for p in range(n_pages):
            cp = _page_copy(cache_ref, buf_ref, sem_ref, pt_ref[sb, p], slot, p, col0, 2 * GH, page_size)
            cp.start() if start else cp.wait()

    @pl.when(b == 0)
    def _():
        fetch(b, True)

    @pl.when(b + 1 < nb)
    def _():
        fetch(b + 1, True)

    fetch(b, False)
    slot = b % 2
    kp = buf_ref[slot, :, 0:GH]
    vp = buf_ref[slot, :, GH:2 * GH]
    nblk = past // SEL_BLOCK
    mask_p = jnp.broadcast_to(selp_ref[0], (nblk, SEL_BLOCK, ncol)).reshape(past, ncol) > 0.5
    mask_n = _new_key_mask(ts, ncol, new_ref.shape[1]) & (seln_ref[0] > 0.5)
    o_ref[0] = _decode_core(kp, vp, new_ref[0, :, 0:GH], new_ref[0, :, GH:2 * GH], q_ref[0], kng_ref[...],
                            eexp_ref[...], bp_ref[...], bn_ref[...], mask_p, mask_n)


def _win_s_kernel(st_ref, new_ref, q_ref, kng_ref, eexp_ref, bp_ref, bn_ref, o_ref, nw_ref, *, ts, kpos0):
    wb = st_ref.shape[2]
    ncol = q_ref.shape[2]
    jp = lax.broadcasted_iota(jnp.int32, (wb, ncol), 0)
    tp = lax.broadcasted_iota(jnp.int32, (wb, ncol), 1) % ts
    dist = wb + tp - jp
    mask_p = (dist >= 0) & (dist < WINDOW) & (kpos0 + jp >= 0)
    mask_n = _new_key_mask(ts, ncol, new_ref.shape[1])
    st = st_ref[0, 0]
    o_ref[0] = _decode_core(st[:, 0:GH], st[:, GH:2 * GH], new_ref[0, :, 0:GH], new_ref[0, :, GH:2 * GH],
                            q_ref[0], kng_ref[...], eexp_ref[...], bp_ref[...], bn_ref[...], mask_p, mask_n)
    wout = nw_ref.shape[2]
    keep = wout - ts
    nw_ref[0, 0, 0:keep, :] = st_ref[0, 0, pl.ds(wb - keep, keep), :]
    nw_ref[0, 0, keep:wout, :] = new_ref[0, 0:ts, :]


def _decode_bias(rel_bias, ts, key_dist):
    ncol = C_HEADS * ts
    dist = key_dist[:, np.arange(ncol) % ts]
    head = np.broadcast_to((np.arange(ncol) // ts)[None, :], dist.shape)
    return _lut_call(rel_bias, dist, head)


def _sample_bias_tables(rel_bias, ts, past, wb, npad):
    tq = np.arange(ts)[None, :]
    tail = 2 * MAX_DIST
    assert past >= tail and tail - ts >= MAX_DIST
    sel_tail = _decode_bias(rel_bias, ts, tail + tq - np.arange(tail)[:, None])
    sel_past = jnp.concatenate([jnp.broadcast_to(sel_tail[0:1], (past - tail, sel_tail.shape[1])), sel_tail], axis=0)
    return dict(
        new=_decode_bias(rel_bias, ts, tq - np.arange(8)[:, None]),
        sel=sel_past,
        win=_decode_bias(rel_bias, ts, wb + tq - np.arange(wb)[:, None]),
        cmp=_cmp_s_bias(rel_bias, npad, ts, past))


def _sel_s_call(cache3, page_table, layer, new_rows, qblk, kng_col, eexp, bias_p, bias_n, selp, seln, ts):
    Bs, n_pages = page_table.shape
    page_size = cache3.shape[1]
    past = n_pages * page_size
    ncol = qblk.shape[2]
    nblk = past // SEL_BLOCK
    m3 = lambda b, pt: (b, 0, 0)
    c2 = lambda b, pt: (0, 0)
    grid_spec = pltpu.PrefetchScalarGridSpec(
        num_scalar_prefetch=1,
        grid=(Bs,),
        in_specs=[pl.BlockSpec(memory_space=pl.ANY),
                  pl.BlockSpec((1,) + new_rows.shape[1:], m3),
                  pl.BlockSpec((1, GH, ncol), m3),
                  pl.BlockSpec(kng_col.shape, c2), pl.BlockSpec(eexp.shape, c2),
                  pl.BlockSpec(bias_p.shape, c2), pl.BlockSpec(bias_n.shape, c2),
                  pl.BlockSpec((1, nblk, 1, ncol), lambda b, pt: (b, 0, 0, 0)),
                  pl.BlockSpec((1, 1, ncol), m3)],
        out_specs=pl.BlockSpec((1, ncol, HEAD_DIM), m3),
        scratch_shapes=[pltpu.VMEM((2, past, 2 * GH), F32), pltpu.SemaphoreType.DMA((2,))],
    )
    return pl.pallas_call(
        functools.partial(_sel_s_kernel, n_pages=n_pages, page_size=page_size, col0=layer * 2 * GH, ts=ts),
        grid_spec=grid_spec,
        out_shape=jax.ShapeDtypeStruct((Bs, ncol, HEAD_DIM), F32),
        compiler_params=_params("arbitrary"),
        name="sel_attn_sample",
    )(page_table, cache3, new_rows, qblk, kng_col, eexp, bias_p, bias_n, selp, seln)


def _win_s_call(state4, layer, new_rows, qblk, kng_col, eexp, bias_p, bias_n, ts, past):
    n_l, Bs, wb, _ = state4.shape
    ncol = qblk.shape[2]
    wout = min(WINDOW, wb + ts)
    m3 = lambda b: (b, 0, 0)
    c2 = lambda b: (0, 0)
    return pl.pallas_call(
        functools.partial(_win_s_kernel, ts=ts, kpos0=past - wb),
        grid=(Bs,),
        in_specs=[pl.BlockSpec((1, 1, wb, 2 * GH), lambda b: (layer, b, 0, 0)),
                  pl.BlockSpec((1,) + new_rows.shape[1:], m3),
                  pl.BlockSpec((1, GH, ncol), m3),
                  pl.BlockSpec(kng_col.shape, c2), pl.BlockSpec(eexp.shape, c2),
                  pl.BlockSpec(bias_p.shape, c2), pl.BlockSpec(bias_n.shape, c2)],
        out_specs=[pl.BlockSpec((1, ncol, HEAD_DIM), m3),
                   pl.BlockSpec((1, 1, wout, 2 * GH), lambda b: (0, b, 0, 0))],
        out_shape=[jax.ShapeDtypeStruct((Bs, ncol, HEAD_DIM), F32),
                   jax.ShapeDtypeStruct((1, Bs, wout, 2 * GH), F32)],
        compiler_params=_params("arbitrary"),
        name="win_attn_sample",
    )(state4, new_rows, qblk, kng_col, eexp, bias_p, bias_n)


def _nsa_prompt(x, mods, g, w_in_pad, w_out, qn_g, kn_g, pe, w1, w2, rel_bias, tm):
    B, T, D = x.shape
    N = B * T
    nq = C_HEADS * HEAD_DIM
    q, cmp_rows, sel_rows, win_rows, _, gates_t = _nsa_proj_call(x, mods[3], mods[4], g, w_in_pad, tm)
    tr = min(2048, N)
    assert N % tr == 0
    qn = _headnorm_call(q.reshape(N, nq), qn_g, nq, tr, SCALE * LOG2E).reshape(B, T, nq)
    seln = _headnorm_call(sel_rows.reshape(N, 2 * GH), kn_g, GH, tr).reshape(B, T, GH)
    winn = _headnorm_call(win_rows.reshape(N, 2 * GH), kn_g, GH, tr).reshape(B, T, GH)
    n_cmp = (T - CMP_BLOCK) // CMP_STRIDE + 1
    kvc = _compress_p_call(cmp_rows, pe, w1, w2, kn_g)
    o_cmp, selmask = _cmp_p_call(qn, kvc, rel_bias, gates_t, n_cmp)
    o_sel = _pattn_call(qn, seln, sel_rows, selmask, gates_t, rel_bias, "sel")
    o_win = _pattn_call(qn, winn, win_rows, None, gates_t, rel_bias, "win")
    x = _sum_proj_call(x, mods[5], o_cmp, o_sel, o_win, w_out, tm)
    wk = min(WINDOW, T)
    return x, cmp_rows, sel_rows, win_rows[:, T - wk:]


def _nsa_sample(x, mods, g, w_in_pad, w_out, qn_g, kn_g, pe, w1, w2, bias_tabs, cache_cmp3, cache_sel3,
                page_table, state_win4, layer, ts, bsz):
    R = ts * bsz
    nq = C_HEADS * HEAD_DIM
    past = page_table.shape[1] * cache_cmp3.shape[1]
    assert past % CMP_STRIDE == 0 and past % SEL_BLOCK == 0 and ts <= 8 and ts <= CMP_STRIDE
    q, cmp_rows, sel_rows, win_rows, gates, _ = _nsa_proj_call(x, mods[3], mods[4], g, w_in_pad, R)
    qn = _headnorm_call(q.reshape(R, nq), qn_g, nq, R, SCALE)
    q5 = jnp.transpose(qn.reshape(ts, bsz, KV_GROUPS, HPG, HEAD_DIM), (1, 2, 3, 0, 4))
    q2 = q5.reshape(bsz, KV_GROUPS, HPG * ts, HEAD_DIM)
    qd = jnp.transpose(q5.astype(F32).reshape(bsz, KV_GROUPS, HPG * ts, HEAD_DIM), (0, 1, 3, 2))
    qblk = (qd[:, :, :, None, :] * jnp.eye(KV_GROUPS, dtype=F32)[None, :, None, :, None]).reshape(bsz, GH, C_HEADS * ts)
    ncol = C_HEADS * ts
    tk = past + ts
    n_cmp = (tk - CMP_BLOCK) // CMP_STRIDE + 1
    assert n_cmp <= past // CMP_STRIDE - 1 + 1 and (n_cmp - 1) * CMP_STRIDE + CMP_BLOCK <= past
    nsb = -(-tk // SEL_BLOCK)
    kvc = _compress_s_call(cache_cmp3, page_table, layer, pe, w1, w2, kn_g)
    o_cmp, selw = _cmp_s_call(q2, kvc, bias_tabs["cmp"], n_cmp, nsb, ts, past)
    selt = jnp.transpose(selw, (0, 3, 1, 2))
    selt = jnp.broadcast_to(selt[:, :, :, None, :], selt.shape[:3] + (HPG, ts)).reshape(bsz, -1, 1, ncol)
    nblk = past // SEL_BLOCK
    assert nsb == nblk + 1
    selp, seln = selt[:, :nblk], selt[:, nblk]
    to_seq = lambda a: jnp.pad(jnp.transpose(a.reshape(ts, bsz, 2 * GH), (1, 0, 2)), ((0, 0), (0, 8 - ts), (0, 0)))
    kng_col = jnp.tile(kn_g, KV_GROUPS).reshape(GH, 1)
    eexp = np.zeros((GH, ncol), np.float32)
    for gi in range(KV_GROUPS):
        eexp[gi * HEAD_DIM:(gi + 1) * HEAD_DIM, gi * HPG * ts:(gi + 1) * HPG * ts] = 1.0
    eexp = jnp.asarray(eexp, BF16)
    o_sel = _sel_s_call(cache_sel3, page_table, layer, to_seq(sel_rows), qblk, kng_col, eexp, bias_tabs["sel"],
                        bias_tabs["new"], selp, seln, ts)
    o_win, new_win = _win_s_call(state_win4, layer, to_seq(win_rows), qblk, kng_col, eexp, bias_tabs["win"],
                                 bias_tabs["new"], ts, past)
    back = lambda o: jnp.transpose(o.reshape(bsz, KV_GROUPS, HPG, ts, HEAD_DIM), (3, 0, 1, 2, 4)).reshape(1, R, nq)
    x = _combine_call(x, mods[5], back(o_cmp), back(o_sel), back(o_win), gates, w_out, R)
    rows_out = lambda a: jnp.transpose(a.reshape(ts, bsz, 2, KV_GROUPS, HEAD_DIM), (1, 0, 2, 3, 4))
    return x, rows_out(cmp_rows), rows_out(sel_rows), new_win[0].reshape(bsz, -1, 2, KV_GROUPS, HEAD_DIM)


def kernel(x_prompt, x_sample, c_prompt, c_sample, cache_cmp_kv, cache_sel_kv, page_table, state_win_kv, state_conv, ada_w, ada_b, norm_g, ffn_w1, ffn_w2, even_w_in, even_w_out, gmlp_v_g, gmlp_ws, gmlp_bs, conv_w, conv_b, conv_ln_g, conv_ln_b, nsa_w_in, nsa_w_out, q_norm_g, k_norm_g, cmp_pe, cmp_w1, cmp_w2, rel_bias):
    B, T, D = x_prompt.shape
    Bs, Ts, _ = x_sample.shape
    depth = ada_w.shape[0]
    n_odd = nsa_w_in.shape[0]
    tm = min(512, T)
    Rs = Ts * Bs

    ada = _ada_call(jnp.concatenate([c_prompt, c_sample], axis=0), ada_w, ada_b)
    xp = x_prompt
    xs = jnp.transpose(x_sample, (1, 0, 2)).reshape(1, Rs, D)
    n_pool, page_size = cache_cmp_kv.shape[:2]
    cache_cmp3 = cache_cmp_kv.reshape(n_pool, page_size, -1)
    cache_sel3 = cache_sel_kv.reshape(n_pool, page_size, -1)
    state_win4 = state_win_kv.reshape(state_win_kv.shape[:3] + (2 * GH,))
    past = page_table.shape[1] * page_size
    bias_tabs = _sample_bias_tables(rel_bias, Ts, past, state_win_kv.shape[2], past // CMP_STRIDE)
    pad_cols = (-nsa_w_in.shape[2] + C_HEADS * HEAD_DIM + 6 * GH + LANES)

    cmp_p, cmp_s, sel_p, sel_s, win_p, win_s, conv_p, conv_s, v_s = ([] for _ in range(9))
    for l in range(depth):
        mp = [ada[l, :B, k * D:(k + 1) * D].reshape(B, 1, D) for k in range(9)]
        ms = [jnp.tile(ada[l, B:, k * D:(k + 1) * D], (Ts, 1)).reshape(1, Rs, D) for k in range(9)]
        w1 = ffn_w1[l].astype(BF16)
        w2 = ffn_w2[l].astype(BF16)
        xp = _ffn_call(xp, mp[0], mp[1], mp[2], norm_g[l, 0], w1[0], w2[0], tm)
        xs = _ffn_call(xs, ms[0], ms[1], ms[2], norm_g[l, 0], w1[0], w2[0], Rs)
        if l % 2 == 0:
            e = l // 2
            w_in = even_w_in[e].astype(BF16)
            w_out = even_w_out[e].astype(BF16)
            prm = (gmlp_v_g[e], gmlp_ws[e], gmlp_bs[e])
            cprm = (conv_w[e], conv_b[e], conv_ln_g[e], conv_ln_b[e])
            xp, cst_p = _even_call(xp, mp[3], mp[4], mp[5], norm_g[l, 1], w_in, w_out, *prm, *cprm, tm)
            xs, cst_s, vrow = _even_s_call(xs, ms[3], ms[4], ms[5], norm_g[l, 1], w_in, w_out, *prm,
                                           state_conv[e], *cprm, Ts, Bs)
            conv_p.append(cst_p)
            conv_s.append(cst_s)
            v_s.append(vrow)
        else:
            o = l // 2
            w_in_pad = jnp.pad(nsa_w_in[o], ((0, 0), (0, pad_cols))).astype(BF16)
            w_out = nsa_w_out[o].astype(BF16)
            prm = (norm_g[l, 1], w_in_pad, w_out, q_norm_g[o], k_norm_g[o], cmp_pe[o], cmp_w1[o], cmp_w2[o])
            xp, rc_p, rs_p, w_p = _nsa_prompt(xp, mp, *prm, rel_bias, tm)
            xs, rc_s, rs_s, w_s = _nsa_sample(xs, ms, *prm, bias_tabs, cache_cmp3, cache_sel3, page_table, state_win4,
                                              o, Ts, Bs)
            kv6 = lambda a: a.reshape(B, -1, 2, KV_GROUPS, HEAD_DIM)
            cmp_p.append(kv6(rc_p))
            sel_p.append(kv6(rs_p))
            win_p.append(kv6(w_p))
            cmp_s.append(rc_s)
            sel_s.append(rs_s)
            win_s.append(w_s)
        xp = _ffn_call(xp, mp[6], mp[7], mp[8], norm_g[l, 2], w1[1], w2[1], tm)
        xs = _ffn_call(xs, ms[6], ms[7], ms[8], norm_g[l, 2], w1[1], w2[1], Rs)
    y_sample = jnp.transpose(xs.reshape(Ts, Bs, D), (1, 0, 2))
    return (xp, y_sample, jnp.stack(cmp_p, axis=2), jnp.stack(cmp_s, axis=2), jnp.stack(sel_p, axis=2),
            jnp.stack(sel_s, axis=2), jnp.stack(win_p, axis=0), jnp.stack(win_s, axis=0),
            jnp.stack(conv_p, axis=0), jnp.stack(conv_s, axis=0), jnp.stack(v_s, axis=0))
```

```python
import functools
import math

import numpy as np
import jax
import jax.numpy as jnp
from jax import lax
from jax.experimental import pallas as pl
from jax.experimental.pallas import tpu as pltpu

F32 = jnp.float32
BF16 = jnp.bfloat16

A_GROUPS = 8
CHUNK = 128
CONV_W = 31
C_HEADS = 16
KV_GROUPS = 4
HEAD_DIM = 64
HPG = C_HEADS // KV_GROUPS
GH = KV_GROUPS * HEAD_DIM
CMP_BLOCK = 32
CMP_STRIDE = 16
SEL_BLOCK = 64
N_SEL = 16
WINDOW = 512
N_BUCKETS = 32
MAX_DIST = 128
SCALE = HEAD_DIM ** -0.5
EPS = 1e-6
NEG = -1e30
FORCE = 1e6

VMEM_LIMIT_BYTES = 60 * 2 ** 20
LANES = 128
TQ = 256
TK = 256
STACK = 4
HIST = 32


def _params(*sem):
    return pltpu.CompilerParams(dimension_semantics=sem, vmem_limit_bytes=VMEM_LIMIT_BYTES)


def _dot(a, b):
    return jnp.dot(a, b, preferred_element_type=F32)


def _dot_nt(a, b):
    return lax.dot_general(a, b, (((1,), (1,)), ((), ())), preferred_element_type=F32)


def _dot_tn(a, b):
    return lax.dot_general(a, b, (((0,), (0,)), ((), ())), preferred_element_type=F32)


def _split_bf16(x):
    hi = x.astype(BF16)
    lo = (x - hi.astype(F32)).astype(BF16)
    return hi, lo


def _sigmoid(x):
    return 1.0 / (1.0 + jnp.exp(-x))


def _silu(x):
    return x * _sigmoid(x)


def _gelu(x):
    return 0.5 * x * (1.0 + jnp.tanh(math.sqrt(2.0 / math.pi) * (x + 0.044715 * (x * x * x))))


def _modulate(x, g, shift, scale):
    y = x * lax.rsqrt(jnp.mean(x * x, axis=-1, keepdims=True) + EPS)
    return (y * g) * (1.0 + scale) + shift


def _bucket_np(dist):
    exact = N_BUCKETS // 2
    d = np.maximum(dist, 0)
    df = np.maximum(d, 1).astype(np.float32)
    large = exact + (np.log(df / np.float32(exact)) / np.float32(math.log(MAX_DIST / exact))
                     * np.float32(N_BUCKETS - exact)).astype(np.int32)
    return np.where(d < exact, d, np.minimum(large, N_BUCKETS - 1)).astype(np.int32)


def _bucket_thresholds():
    b = _bucket_np(np.arange(4 * MAX_DIST))
    assert (np.diff(b) >= 0).all() and b[MAX_DIST] == N_BUCKETS - 1
    return [int(np.argmax(b >= k)) for k in range(N_BUCKETS)]


_THR = _bucket_thresholds()


def _bias_lut(d, rb_ref, h):
    val = jnp.full(d.shape, rb_ref[0, h], F32)
    for k in range(1, N_BUCKETS):
        val = jnp.where(d >= _THR[k], rb_ref[k, h], val)
    return val


_SMEM_SPEC = pl.BlockSpec(memory_space=pltpu.SMEM)


def _full_spec(shape):
    n = len(shape)
    return pl.BlockSpec(shape, lambda *_: (0,) * n)


def _tok_spec(tm, width):
    return pl.BlockSpec((1, tm, width), lambda b, i: (b, i, 0))


def _mod_spec(mod, tm):
    if mod.shape[1] == 1:
        return pl.BlockSpec((1, 1, mod.shape[2]), lambda b, i: (b, 0, 0))
    return pl.BlockSpec((1, tm, mod.shape[2]), lambda b, i: (b, i, 0))


def _ada_kernel(c_ref, w_ref, b_ref, o_ref):
    c = c_ref[...]
    o_ref[0] = _dot(_silu(c).astype(BF16), w_ref[0].astype(BF16)) + b_ref[0]


def _ada_call(c_all, ada_w, ada_b):
    L, D, N = ada_w.shape
    M = c_all.shape[0]
    tn = 1024
    return pl.pallas_call(
        _ada_kernel,
        grid=(L, N // tn),
        in_specs=[pl.BlockSpec((M, D), lambda l, j: (0, 0)),
                  pl.BlockSpec((1, D, tn), lambda l, j: (l, 0, j)),
                  pl.BlockSpec((1, 1, tn), lambda l, j: (l, 0, j))],
        out_specs=pl.BlockSpec((1, M, tn), lambda l, j: (l, 0, j)),
        out_shape=jax.ShapeDtypeStruct((L, M, N), F32),
        compiler_params=_params("arbitrary", "arbitrary"),
        name="ada",
    )(c_all, ada_w, ada_b.reshape(L, 1, N))


def _ffn_kernel(x_ref, sh_ref, sc_ref, gt_ref, g_ref, w1_ref, w2_ref, o_ref, acc_ref, *, ff, tf):
    x = x_ref[0]
    h = _modulate(x, g_ref[...], sh_ref[0], sc_ref[0]).astype(BF16)
    for c in range(ff // tf):
        a = _dot(h, w1_ref[:, c * tf:(c + 1) * tf])
        b = _dot(h, w1_ref[:, ff + c * tf:ff + (c + 1) * tf])
        t = (_silu(a) * b).astype(BF16)
        part = _dot(t, w2_ref[c * tf:(c + 1) * tf, :])
        if c == 0:
            acc_ref[...] = part
        else:
            acc_ref[...] += part
    o_ref[0] = x + 0.5 * gt_ref[0] * acc_ref[...]


def _ffn_call(x3, sh, sc, gt, g, w1, w2, tm):
    NB, R, D = x3.shape
    ff = w2.shape[0]
    tf = 256
    return pl.pallas_call(
        functools.partial(_ffn_kernel, ff=ff, tf=tf),
        grid=(NB, R // tm),
        in_specs=[_tok_spec(tm, D), _mod_spec(sh, tm), _mod_spec(sc, tm), _mod_spec(gt, tm),
                  _full_spec((1, D)), _full_spec(w1.shape), _full_spec(w2.shape)],
        out_specs=_tok_spec(tm, D),
        out_shape=jax.ShapeDtypeStruct(x3.shape, F32),
        scratch_shapes=[pltpu.VMEM((tm, D), F32)],
        compiler_params=_params("arbitrary", "arbitrary"),
        name="ffn",
    )(x3, sh, sc, gt, g.reshape(1, D), w1, w2)


def _even_kernel(x_ref, sh_ref, sc_ref, gt_ref, g_ref, win_ref, wout_ref, vg_ref, ws_ref, bs_ref,
                 cw_ref, cb_ref, lg_ref, lb_ref, o_ref, cs_ref, ext_ref, sa_ref, shf_ref, *, tm, aw):
    @pl.when(pl.program_id(1) == 0)
    def _():
        ext_ref[0:HIST, :] = jnp.zeros((HIST, ext_ref.shape[1]), F32)

    x = x_ref[0]
    h = _modulate(x, g_ref[...], sh_ref[0], sc_ref[0]).astype(BF16)
    z = _dot(h, win_ref[...])
    u = _gelu(z[:, 0:aw])
    gv = _gelu(z[:, aw:2 * aw])
    v = gv * lax.rsqrt(jnp.mean(gv * gv, axis=-1, keepdims=True) + EPS) * vg_ref[...]
    vb = v.astype(BF16)
    row = lax.broadcasted_iota(jnp.int32, (CHUNK, CHUNK), 0)
    col = lax.broadcasted_iota(jnp.int32, (CHUNK, CHUNK), 1)
    wm = [jnp.where(row >= col, ws_ref[gi], 0.0).astype(BF16) for gi in range(A_GROUPS)]
    gw = aw // A_GROUPS
    first_half = col < gw
    for c in range(tm // CHUNK):
        for q in range(aw // LANES):
            vq = vb[c * CHUNK:(c + 1) * CHUNK, q * LANES:(q + 1) * LANES]
            s0 = _dot(wm[2 * q], vq)
            s1 = _dot(wm[2 * q + 1], vq)
            sa_ref[c * CHUNK:(c + 1) * CHUNK, q * LANES:(q + 1) * LANES] = (
                jnp.where(first_half, s0, s1) + bs_ref[:, q * LANES:(q + 1) * LANES])
    a_out = u * sa_ref[...]
    glu = z[:, 2 * aw:3 * aw] * _sigmoid(z[:, 3 * aw:4 * aw])
    ext_ref[HIST:HIST + tm, :] = glu
    off = HIST - (CONV_W - 1)
    conv = jnp.broadcast_to(cb_ref[...], (tm, ext_ref.shape[1]))
    for r in range(8):
        na = (CONV_W - 1 - r) // 8 + 1
        nr = tm + 8 * (na - 1)
        shf_ref[0:nr, :] = ext_ref[pl.ds(off + r, nr), :]
        for a in range(na):
            conv = conv + shf_ref[8 * a:8 * a + tm, :] * cw_ref[8 * a + r:8 * a + r + 1, :]
    cs_ref[0] = ext_ref[pl.ds(tm + off, CONV_W - 1), :]
    ext_ref[0:HIST, :] = ext_ref[tm:tm + HIST, :]
    mu = jnp.mean(conv, axis=-1, keepdims=True)
    cc = conv - mu
    var = jnp.mean(cc * cc, axis=-1, keepdims=True)
    b_out = _silu(cc * lax.rsqrt(var + EPS) * lg_ref[...] + lb_ref[...])
    out = _dot(a_out.astype(BF16), wout_ref[0:aw, :]) + _dot(b_out.astype(BF16), wout_ref[aw:, :])
    o_ref[0] = x + gt_ref[0] * out


def _even_call(x3, sh, sc, gt, g, w_in, w_out, v_g, ws, bs, cw, cb, ln_g, ln_b, tm):
    B, T, D = x3.shape
    aw = v_g.shape[0]
    bw = cw.shape[1]
    assert aw == bw and aw // A_GROUPS * 2 == LANES and T % tm == 0 and tm % CHUNK == 0
    bs_exp = jnp.repeat(bs.T, aw // A_GROUPS, axis=1)
    cw_pad = jnp.pad(cw, ((0, 1), (0, 0)))
    row = lambda a: a.reshape(1, -1)
    return pl.pallas_call(
        functools.partial(_even_kernel, tm=tm, aw=aw),
        grid=(B, T // tm),
        in_specs=[_tok_spec(tm, D), _mod_spec(sh, tm), _mod_spec(sc, tm), _mod_spec(gt, tm),
                  _full_spec((1, D)), _full_spec(w_in.shape), _full_spec(w_out.shape),
                  _full_spec((1, aw)), _full_spec(ws.shape), _full_spec(bs_exp.shape),
                  _full_spec(cw_pad.shape), _full_spec((1, bw)), _full_spec((1, bw)), _full_spec((1, bw))],
        out_specs=[_tok_spec(tm, D), pl.BlockSpec((1, CONV_W - 1, bw), lambda b, i: (b, 0, 0))],
        out_shape=[jax.ShapeDtypeStruct(x3.shape, F32), jax.ShapeDtypeStruct((B, CONV_W - 1, bw), F32)],
        scratch_shapes=[pltpu.VMEM((tm + HIST, bw), F32), pltpu.VMEM((tm, aw), F32),
                        pltpu.VMEM((tm + HIST, bw), F32)],
        compiler_params=_params("arbitrary", "arbitrary"),
        name="even_prompt",
    )(x3, sh, sc, gt, row(g), w_in, w_out, row(v_g), ws, bs_exp, cw_pad, row(cb), row(ln_g), row(ln_b))


def _even_s_kernel(x_ref, sh_ref, sc_ref, gt_ref, g_ref, win_ref, wout_ref, vg_ref, coef_ref, bsa_ref,
                   st_ref, cw_ref, cb_ref, lg_ref, lb_ref, o_ref, cs_ref, v_ref, *, ts, bsz, aw):
    x = x_ref[0]
    h = _modulate(x, g_ref[...], sh_ref[0], sc_ref[0]).astype(BF16)
    z = _dot(h, win_ref[...])
    u = _gelu(z[:, 0:aw])
    gv = _gelu(z[:, aw:2 * aw])
    v = gv * lax.rsqrt(jnp.mean(gv * gv, axis=-1, keepdims=True) + EPS) * vg_ref[...]
    glu = z[:, 2 * aw:3 * aw] * _sigmoid(z[:, 3 * aw:4 * aw])
    hist = CONV_W - 1
    sl = lambda a, t: a[t * bsz:(t + 1) * bsz]
    a_parts, b_parts = [], []
    for t in range(ts):
        v_ref[t] = sl(v, t)
        s = bsa_ref[t]
        for j in range(t + 1):
            s = s + coef_ref[t, j] * sl(v, j)
        a_parts.append(sl(u, t) * s)
        conv = cb_ref[...]
        for m in range(t, hist):
            conv = conv + st_ref[m] * cw_ref[m - t:m - t + 1, :]
        for j in range(t + 1):
            conv = conv + sl(glu, j) * cw_ref[hist - t + j:hist - t + j + 1, :]
        mu = jnp.mean(conv, axis=-1, keepdims=True)
        cc = conv - mu
        var = jnp.mean(cc * cc, axis=-1, keepdims=True)
        b_parts.append(_silu(cc * lax.rsqrt(var + EPS) * lg_ref[...] + lb_ref[...]))
    for i in range(hist):
        cs_ref[i] = st_ref[i + ts] if i + ts < hist else sl(glu, i + ts - hist)
    a_out = jnp.concatenate(a_parts, axis=0).astype(BF16)
    b_out = jnp.concatenate(b_parts, axis=0).astype(BF16)
    out = _dot(a_out, wout_ref[0:aw, :]) + _dot(b_out, wout_ref[aw:, :])
    o_ref[0] = x + gt_ref[0] * out


def _even_s_call(x3, sh, sc, gt, g, w_in, w_out, v_g, ws, bs, state, cw, cb, ln_g, ln_b, ts, bsz):
    _, R, D = x3.shape
    aw = v_g.shape[0]
    bw = cw.shape[1]
    gw = aw // A_GROUPS
    hist = CONV_W - 1
    assert ts <= CHUNK and ts <= hist
    coef = jnp.repeat(jnp.transpose(ws[:, :ts, :ts], (1, 2, 0)), gw, axis=2).reshape(ts, ts, 1, aw)
    bsa = jnp.repeat(bs[:, :ts].T, gw, axis=1).reshape(ts, 1, aw)
    st = jnp.transpose(state, (1, 0, 2))
    row = lambda a: a.reshape(1, -1)
    args = (x3, sh, sc, gt, row(g), w_in, w_out, row(v_g), coef, bsa, st, cw, row(cb), row(ln_g), row(ln_b))
    x_new, cs, v = pl.pallas_call(
        functools.partial(_even_s_kernel, ts=ts, bsz=bsz, aw=aw),
        grid=(1,),
        in_specs=[_full_spec(a.shape) for a in args],
        out_specs=[_full_spec(x3.shape), _full_spec((hist, bsz, bw)), _full_spec((ts, bsz, aw))],
        out_shape=[jax.ShapeDtypeStruct(x3.shape, F32), jax.ShapeDtypeStruct((hist, bsz, bw), F32),
                   jax.ShapeDtypeStruct((ts, bsz, aw), F32)],
        compiler_params=_params("arbitrary"),
        name="even_sample",
    )(*args)
    return x_new, jnp.transpose(cs, (1, 0, 2)), jnp.transpose(v, (1, 0, 2))


def _head_rmsnorm(x, g2):
    lo = lax.broadcasted_iota(jnp.int32, (1, LANES), 1) < HEAD_DIM
    sq = x * x
    s_lo = jnp.sum(jnp.where(lo, sq, 0.0), axis=-1, keepdims=True)
    s_hi = jnp.sum(jnp.where(lo, 0.0, sq), axis=-1, keepdims=True)
    return x * lax.rsqrt(jnp.where(lo, s_lo, s_hi) * (1.0 / HEAD_DIM) + EPS) * g2


def _nsa_proj_kernel(x_ref, sh_ref, sc_ref, g_ref, w_ref, qg_ref, kg_ref, qn_ref, cmp_ref, sel_ref, win_ref,
                     seln_ref, winn_ref, gate_ref, gatet_ref, *, qscale):
    x = x_ref[0]
    h = _modulate(x, g_ref[...], sh_ref[0], sc_ref[0]).astype(BF16)
    z = _dot(h, w_ref[...])
    nq = C_HEADS * HEAD_DIM
    for c in range(nq // LANES):
        qn_ref[0, :, c * LANES:(c + 1) * LANES] = (
            _head_rmsnorm(z[:, c * LANES:(c + 1) * LANES], qg_ref[...]) * qscale).astype(BF16)
    cmp_ref[0] = z[:, nq:nq + 2 * GH]
    sel_ref[0] = z[:, nq + 2 * GH:nq + 4 * GH]
    win_ref[0] = z[:, nq + 4 * GH:nq + 6 * GH]
    for c in range(GH // LANES):
        sk = nq + 2 * GH + c * LANES
        wk = nq + 4 * GH + c * LANES
        seln_ref[0, :, c * LANES:(c + 1) * LANES] = _head_rmsnorm(z[:, sk:sk + LANES], kg_ref[...]).astype(BF16)
        winn_ref[0, :, c * LANES:(c + 1) * LANES] = _head_rmsnorm(z[:, wk:wk + LANES], kg_ref[...]).astype(BF16)
    gate = _sigmoid(z[:, nq + 6 * GH:nq + 6 * GH + LANES])
    gate_ref[0] = gate
    gatet_ref[0] = gate.T


def _nsa_proj_call(x3, sh, sc, g, w_in_pad, qn_g, kn_g, qscale, tm):
    NB, R, D = x3.shape
    nq = C_HEADS * HEAD_DIM
    outs = ((nq, BF16), (2 * GH, F32), (2 * GH, F32), (2 * GH, F32), (GH, BF16), (GH, BF16), (LANES, F32))
    tile2 = lambda a: jnp.tile(a, LANES // HEAD_DIM).reshape(1, LANES)
    return pl.pallas_call(
        functools.partial(_nsa_proj_kernel, qscale=qscale),
        grid=(NB, R // tm),
        in_specs=[_tok_spec(tm, D), _mod_spec(sh, tm), _mod_spec(sc, tm), _full_spec((1, D)),
                  _full_spec(w_in_pad.shape), _full_spec((1, LANES)), _full_spec((1, LANES))],
        out_specs=[_tok_spec(tm, w) for w, _ in outs] + [pl.BlockSpec((1, LANES, tm), lambda b, i: (b, 0, i))],
        out_shape=[jax.ShapeDtypeStruct((NB, R, w), dt) for w, dt in outs]
        + [jax.ShapeDtypeStruct((NB, LANES, R), F32)],
        compiler_params=_params("arbitrary", "arbitrary"),
        name="nsa_proj",
    )(x3, sh, sc, g.reshape(1, D), w_in_pad, tile2(qn_g), tile2(kn_g))


def _compress_core(load_rows, n, pe_ref, w1_ref, w2_ref, kng_ref, o_ref, is_k):
    hid = w2_ref.shape[1]
    nstk = CMP_STRIDE // STACK
    accs = [None] * KV_GROUPS
    ctop = jnp.zeros((1, hid), F32)
    cbot = jnp.zeros((1, hid), F32)
    pe = pe_ref[0].astype(BF16)
    gpl = LANES // HEAD_DIM
    for l4 in range(nstk):
        xs = [[load_rows(l4 * STACK + i, s) for s in range(KV_GROUPS // gpl)] for i in range(STACK)]
        w = w1_ref[0, l4]
        r = _dot(pe, w)
        ctop = ctop + r[l4:l4 + 1, 0:hid]
        cbot = cbot + r[nstk + l4:nstk + l4 + 1, hid:2 * hid]
        for gi in range(KV_GROUPS):
            lo = (gi % gpl) * HEAD_DIM
            xcat = jnp.concatenate([x[gi // gpl][:, lo:lo + HEAD_DIM] for x in xs], axis=1).astype(BF16)
            part = _dot(xcat, w)
            accs[gi] = part if accs[gi] is None else accs[gi] + part
    for gi in range(KV_GROUPS):
        a = accs[gi][:, 0:hid] + ctop
        b = pltpu.roll(accs[gi][:, hid:2 * hid] + cbot, n - 1, 0)
        y = _dot(_gelu(a + b).astype(BF16), w2_ref[0])
        yn = y * lax.rsqrt(jnp.mean(y * y, axis=-1, keepdims=True) + EPS) * kng_ref[...]
        o_ref[0, 0, :, gi * HEAD_DIM:(gi + 1) * HEAD_DIM] = jnp.where(is_k, yn, y).astype(o_ref.dtype)


def _compress_p_kernel(x_ref, pe_ref, w1_ref, w2_ref, kng_ref, o_ref, *, n):
    kv = pl.program_id(1)
    slabs = 2 * GH // LANES
    load = lambda l, s: x_ref[0, pl.ds(l * slabs + kv * (slabs // 2) + s, n, stride=CMP_STRIDE * slabs), :]
    _compress_core(load, n, pe_ref, w1_ref, w2_ref, kng_ref, o_ref, kv == 0)


def _compress_weights(pe, w1, w2):
    hid = w1.shape[-1]
    w1r = w1.reshape(2, CMP_BLOCK, HEAD_DIM, hid)
    pair = jnp.concatenate([w1r[:, :CMP_STRIDE], w1r[:, CMP_STRIDE:]], axis=-1)
    w1c = pair.reshape(2, CMP_STRIDE // STACK, STACK * HEAD_DIM, 2 * hid).astype(BF16)
    return pe.reshape(2, 2 * CMP_STRIDE // STACK, STACK * HEAD_DIM), w1c, w2.astype(BF16)


def _compress_p_call(cmp_rows, pe, w1, w2, kn_g):
    B, T, _ = cmp_rows.shape
    n = T // CMP_STRIDE
    pe, w1r, w2r = _compress_weights(pe, w1, w2)
    hid = w1.shape[-1]
    slabs = 2 * GH // LANES
    cmp_rows = cmp_rows.reshape(B, T * slabs, LANES)
    return pl.pallas_call(
        functools.partial(_compress_p_kernel, n=n),
        grid=(B, 2),
        in_specs=[pl.BlockSpec((1, T * slabs, LANES), lambda b, kv: (b, 0, 0)),
                  pl.BlockSpec((1,) + pe.shape[1:], lambda b, kv: (kv, 0, 0)),
                  pl.BlockSpec((1,) + w1r.shape[1:], lambda b, kv: (kv, 0, 0, 0)),
                  pl.BlockSpec((1, hid, HEAD_DIM), lambda b, kv: (kv, 0, 0)),
                  _full_spec((1, HEAD_DIM))],
        out_specs=pl.BlockSpec((1, 1, n, GH), lambda b, kv: (b, kv, 0, 0)),
        out_shape=jax.ShapeDtypeStruct((B, 2, n, GH), BF16),
        compiler_params=_params("arbitrary", "arbitrary"),
        name="compress_prompt",
    )(cmp_rows, pe, w1r, w2r, kn_g.reshape(1, HEAD_DIM))


def _compress_s_kernel(pt_ref, cache_ref, pe_ref, w1_ref, w2_ref, kng_ref, o_ref, buf_ref, sem_ref,
                       *, n, n_pages, page_size, layer):
    b = pl.program_id(0)
    kv = pl.program_id(1)
    step = b * 2 + kv
    n_steps = pl.num_programs(0) * 2
    cpp = page_size // CMP_STRIDE

    def fetch(s, start):
        sb = s // 2
        slot = s % 2
        for p in range(n_pages):
            cp = pltpu.make_async_copy(
                cache_ref.at[pt_ref[sb, p], layer * 2 + s % 2],
                buf_ref.at[slot, :, pl.ds(p * cpp, cpp), :],
                sem_ref.at[slot])
            cp.start() if start else cp.wait()

    @pl.when(step == 0)
    def _():
        fetch(step, True)

    @pl.when(step + 1 < n_steps)
    def _():
        fetch(step + 1, True)

    fetch(step, False)
    slot = step % 2
    load = lambda l, s: buf_ref[slot, l, :, s * LANES:(s + 1) * LANES]
    _compress_core(load, n, pe_ref, w1_ref, w2_ref, kng_ref, o_ref, kv == 0)


def _chunk_major_cache(cache):
    n_pool, page_size = cache.shape[:2]
    c = cache.reshape(n_pool, page_size // CMP_STRIDE, CMP_STRIDE, -1, GH)
    return jnp.transpose(c, (0, 3, 2, 1, 4))


def _compress_s_call(cache_cm, page_table, layer, pe, w1, w2, kn_g):
    Bs, n_pages = page_table.shape
    page_size = cache_cm.shape[2] * cache_cm.shape[3]
    past = n_pages * page_size
    n = past // CMP_STRIDE
    pe, w1r, w2r = _compress_weights(pe, w1, w2)
    hid = w1.shape[-1]
    grid_spec = pltpu.PrefetchScalarGridSpec(
        num_scalar_prefetch=1,
        grid=(Bs, 2),
        in_specs=[pl.BlockSpec(memory_space=pl.ANY),
                  pl.BlockSpec((1,) + pe.shape[1:], lambda b, kv, pt: (kv, 0, 0)),
                  pl.BlockSpec((1,) + w1r.shape[1:], lambda b, kv, pt: (kv, 0, 0, 0)),
                  pl.BlockSpec((1, hid, HEAD_DIM), lambda b, kv, pt: (kv, 0, 0)),
                  pl.BlockSpec((1, HEAD_DIM), lambda b, kv, pt: (0, 0))],
        out_specs=pl.BlockSpec((1, 1, n, GH), lambda b, kv, pt: (b, kv, 0, 0)),
        scratch_shapes=[pltpu.VMEM((2, CMP_STRIDE, n, GH), F32), pltpu.SemaphoreType.DMA((2,))],
    )
    return pl.pallas_call(
        functools.partial(_compress_s_kernel, n=n, n_pages=n_pages, page_size=page_size, layer=layer),
        grid_spec=grid_spec,
        out_shape=jax.ShapeDtypeStruct((Bs, 2, n, GH), BF16),
        compiler_params=_params("arbitrary", "arbitrary"),
        name="compress_sample",
    )(page_table, cache_cm, pe, w1r, w2r, kn_g.reshape(1, HEAD_DIM))


def _select(imp, tpos, nsb):
    j = lax.broadcasted_iota(jnp.int32, imp.shape, 1)
    cur = tpos // SEL_BLOCK
    valid = (j * SEL_BLOCK <= tpos) & (j < nsb)
    forced = (j == 0) | (j == cur) | (j == cur - 1)
    impf = jnp.where(valid, jnp.where(forced, FORCE, imp), NEG)
    rank = jnp.zeros(imp.shape, jnp.int32)
    for jp in range(nsb):
        c = impf[:, jp:jp + 1]
        beats = (c > impf) | ((c == impf) & (jp < j))
        rank = rank + beats.astype(jnp.int32)
    return (rank < min(N_SEL, nsb)) & valid


def _select_t(imp, tpos, nsb):
    j = lax.broadcasted_iota(jnp.int32, imp.shape, 0)
    cur = tpos // SEL_BLOCK
    valid = (j * SEL_BLOCK <= tpos) & (j < nsb)
    forced = (j == 0) | (j == cur) | (j == cur - 1)
    impf = jnp.where(valid, jnp.where(forced, FORCE, imp), NEG)
    rank = jnp.zeros(imp.shape, jnp.int32)
    for jp in range(nsb):
        c = impf[jp:jp + 1, :]
        beats = (c > impf) | ((c == impf) & (jp < j))
        rank = rank + beats.astype(jnp.int32)
    return (rank < min(N_SEL, nsb)) & valid


def _overlap_np(n_cmp, nsb, rows, cols):
    cs = np.arange(n_cmp)[:, None] * CMP_STRIDE
    ss = np.arange(nsb)[None, :] * SEL_BLOCK
    ov = np.clip(np.minimum(cs + CMP_BLOCK, ss + SEL_BLOCK) - np.maximum(cs, ss), 0, None).astype(np.float32) / CMP_STRIDE
    out = np.zeros((rows, cols), np.float32)
    out[:n_cmp, :nsb] = ov
    return out


def _stacked_qt(qt, gi):
    return jnp.concatenate([qt[(gi * HPG + p) * HEAD_DIM:(gi * HPG + p + 1) * HEAD_DIM] for p in range(HPG)],
                           axis=1).astype(BF16)


def _cmp_p_kernel(rb_ref, q_ref, kv_ref, ovt_ref, gt_ref, o_ref, sel_ref, bias_ref, ot_ref, *, n_cmp, nsb):
    qb = pl.program_id(0)
    npad = kv_ref.shape[2]
    cols = HPG * TQ

    @pl.when(pl.program_id(1) == 0)
    def _():
        n = lax.broadcasted_iota(jnp.int32, (npad, TQ), 0)
        t = qb * TQ + lax.broadcasted_iota(jnp.int32, (npad, TQ), 1)
        d = t - (n * CMP_STRIDE + CMP_BLOCK - 1)
        for h in range(C_HEADS):
            bias_ref[:, h * TQ:(h + 1) * TQ] = _bias_lut(d, rb_ref, h) * LOG2E

    qt = q_ref[0].astype(F32).T
    n4 = lax.broadcasted_iota(jnp.int32, (npad, cols), 0)
    t4 = qb * TQ + (lax.broadcasted_iota(jnp.int32, (npad, cols), 1) & (TQ - 1))
    mask = ((t4 - (n4 * CMP_STRIDE + CMP_BLOCK - 1)) >= 0) & (n4 < n_cmp)
    kc = kv_ref[0, 0]
    vct = kv_ref[0, 1].astype(F32).T
    tpos = qb * TQ + lax.broadcasted_iota(jnp.int32, (nsb, TQ), 1)
    for gi in range(KV_GROUPS):
        s = _dot(kc[:, gi * HEAD_DIM:(gi + 1) * HEAD_DIM], _stacked_qt(qt, gi)) + bias_ref[:, gi * cols:(gi + 1) * cols]
        s = jnp.where(mask, s, NEG)
        m = jnp.max(s, axis=0, keepdims=True)
        e = jnp.where(mask, jnp.exp2(s - m), 0.0)
        prob = e * (1.0 / jnp.maximum(jnp.sum(e, axis=0, keepdims=True), 1e-30))
        o = _dot(vct[gi * HEAD_DIM:(gi + 1) * HEAD_DIM].astype(BF16), prob.astype(BF16))
        psum = prob[:, 0:TQ]
        for p in range(HPG):
            h = gi * HPG + p
            ot_ref[h * HEAD_DIM:(h + 1) * HEAD_DIM, :] = o[:, p * TQ:(p + 1) * TQ] * gt_ref[0, h:h + 1, :]
            if p:
                psum = psum + prob[:, p * TQ:(p + 1) * TQ]
        hi, lo = _split_bf16(psum)
        imp = _dot(ovt_ref[...], hi) + _dot(ovt_ref[...], lo)
        sel_ref[0, gi] = _select_t(imp, tpos, nsb).astype(F32)
    o_ref[0] = ot_ref[...].T


def _cmp_p_call(qn, kvc, rel_bias, gates_t, n_cmp):
    B, T, nq = qn.shape
    npad = kvc.shape[2]
    nsb = -(-T // SEL_BLOCK)
    assert nsb % 8 == 0
    ovt = jnp.asarray(_overlap_np(n_cmp, nsb, npad, nsb).T, BF16)
    return pl.pallas_call(
        functools.partial(_cmp_p_kernel, n_cmp=n_cmp, nsb=nsb),
        grid=(T // TQ, B),
        in_specs=[_SMEM_SPEC,
                  pl.BlockSpec((1, TQ, nq), lambda i, b: (b, i, 0)),
                  pl.BlockSpec((1, 2, npad, GH), lambda i, b: (b, 0, 0, 0)),
                  _full_spec(ovt.shape),
                  pl.BlockSpec((1, LANES, TQ), lambda i, b: (b, 0, i))],
        out_specs=[pl.BlockSpec((1, TQ, nq), lambda i, b: (b, i, 0)),
                   pl.BlockSpec((1, KV_GROUPS, nsb, TQ), lambda i, b: (b, 0, 0, i))],
        out_shape=[jax.ShapeDtypeStruct((B, T, nq), F32),
                   jax.ShapeDtypeStruct((B, KV_GROUPS, nsb, T), F32)],
        scratch_shapes=[pltpu.VMEM((npad, C_HEADS * TQ), F32), pltpu.VMEM((nq, TQ), F32)],
        compiler_params=_params("arbitrary", "arbitrary"),
        name="cmp_attn_prompt",
    )(rel_bias, qn, kvc, ovt, gates_t)


ONES_ROWS = 16


LOG2E = math.log2(math.e)


def _pattn_kernel(qb_ref, kb_ref, fl_ref, rb_ref, q_ref, k_ref, v_ref, *rest, mode, gate_row0, n_tiles):
    if mode == "sel":
        sel_ref, gt_ref, o_ref, qst_ref, m_ref, acc_ref, bias_ref, ot_ref = rest
    else:
        gt_ref, o_ref, qst_ref, m_ref, acc_ref, bias_ref, ot_ref = rest
    i = pl.program_id(1)
    qb = qb_ref[i]
    kb = kb_ref[i]
    cols = HPG * TQ

    @pl.when((pl.program_id(0) == 0) & (i == 0))
    def _():
        row = lax.broadcasted_iota(jnp.int32, (TK, TQ), 0)
        col = lax.broadcasted_iota(jnp.int32, (TK, TQ), 1)
        for oi in range(n_tiles):
            d = oi * TQ + col - row
            vis = d >= 0
            if mode == "win":
                vis = vis & (d < WINDOW)
            for h in range(C_HEADS):
                bias_ref[oi, :, h * TQ:(h + 1) * TQ] = jnp.where(vis, _bias_lut(d, rb_ref, h) * LOG2E, NEG)

    @pl.when((fl_ref[i] & 1) == 1)
    def _():
        qt = q_ref[0].astype(F32).T
        for gi in range(KV_GROUPS):
            qst_ref[gi] = _stacked_qt(qt, gi)
        m_ref[...] = jnp.full(m_ref.shape, NEG, F32)
        acc_ref[...] = jnp.zeros(acc_ref.shape, F32)

    off = qb * TQ - kb * TK
    oi = jnp.minimum(off // TQ, n_tiles - 1)
    k = k_ref[0]
    vt = v_ref[0].T
    ones = jnp.ones((ONES_ROWS, TK), BF16)
    ss, m_olds, m_news = [], [], []
    for gi in range(KV_GROUPS):
        s = _dot(k[:, gi * HEAD_DIM:(gi + 1) * HEAD_DIM], qst_ref[gi]) + bias_ref[oi, :, gi * cols:(gi + 1) * cols]
        if mode == "sel":
            sm = sel_ref[0, gi, 0]
            blk = jnp.concatenate([jnp.broadcast_to(sm[r:r + 1], (SEL_BLOCK, TQ)) for r in range(TK // SEL_BLOCK)], axis=0)
            s = jnp.where(jnp.concatenate([blk > 0.5] * HPG, axis=1), s, NEG)
        m_old = m_ref[gi]
        m_olds.append(m_old)
        m_news.append(jnp.maximum(m_old, jnp.max(s, axis=0, keepdims=True)))
        ss.append(s)
    es = [jnp.exp2(ss[gi] - m_news[gi]).astype(BF16) for gi in range(KV_GROUPS)]
    pvs = []
    for gi in range(KV_GROUPS):
        vext = jnp.concatenate([vt[gi * HEAD_DIM:(gi + 1) * HEAD_DIM].astype(BF16), ones], axis=0)
        pvs.append(_dot(vext, es[gi]))
    for gi in range(KV_GROUPS):
        alpha = jnp.exp2(m_olds[gi] - m_news[gi])
        acc_ref[gi] = alpha * acc_ref[gi] + pvs[gi]
        m_ref[gi] = m_news[gi]

    @pl.when((fl_ref[i] & 2) == 2)
    def _():
        for gi in range(KV_GROUPS):
            a = acc_ref[gi]
            o = a[0:HEAD_DIM] * (1.0 / jnp.maximum(a[HEAD_DIM:HEAD_DIM + 1], 1e-30))
            for p in range(HPG):
                h = gi * HPG + p
                ot_ref[h * HEAD_DIM:(h + 1) * HEAD_DIM, :] = (
                    o[:, p * TQ:(p + 1) * TQ] * gt_ref[0, gate_row0 + h:gate_row0 + h + 1, :])
        o_ref[0] = ot_ref[...].T


def _pair_tables(nq, mode):
    qbs, kbs, fls = [], [], []
    for qb in range(nq):
        hi = (qb * TQ + TQ - 1) // TK
        lo = 0 if mode == "sel" else max((qb * TQ - WINDOW + 1) // TK, 0)
        for kb in range(lo, hi + 1):
            qbs.append(qb)
            kbs.append(kb)
            fls.append((1 if kb == lo else 0) | (2 if kb == hi else 0))
    return tuple(np.array(a, np.int32) for a in (qbs, kbs, fls))


def _pattn_call(qn, kn, rows, selmask, gates_t, rel_bias, mode):
    B, T, nq_w = qn.shape
    assert T % TK == 0 and TK % TQ == 0
    qbs, kbs, fls = _pair_tables(T // TQ, mode)
    n_tiles = int(np.max(qbs * TQ - kbs * TK)) // TQ + 1
    if mode == "sel":
        n_tiles = min(n_tiles, (MAX_DIST + TK - 1) // TQ + 2)
        assert (n_tiles - 1) * TQ - (TK - 1) >= MAX_DIST or n_tiles == int(np.max(qbs * TQ - kbs * TK)) // TQ + 1
    in_specs = [_SMEM_SPEC,
                pl.BlockSpec((1, TQ, nq_w), lambda b, i, qb, kb, fl: (b, qb[i], 0)),
                pl.BlockSpec((1, TK, GH), lambda b, i, qb, kb, fl: (b, kb[i], 0)),
                pl.BlockSpec((1, TK, GH), lambda b, i, qb, kb, fl: (b, kb[i], 1))]
    args = [rel_bias, qn, kn, rows]
    if mode == "sel":
        nblk = TK // SEL_BLOCK
        in_specs.append(pl.BlockSpec((1, KV_GROUPS, 1, nblk, TQ), lambda b, i, qb, kb, fl: (b, 0, kb[i], 0, qb[i])))
        args.append(selmask.reshape(B, KV_GROUPS, T // TK, nblk, T))
    in_specs.append(pl.BlockSpec((1, LANES, TQ), lambda b, i, qb, kb, fl: (b, 0, qb[i])))
    args.append(gates_t)
    cols = HPG * TQ
    grid_spec = pltpu.PrefetchScalarGridSpec(
        num_scalar_prefetch=3,
        grid=(B, int(qbs.shape[0])),
        in_specs=in_specs,
        out_specs=pl.BlockSpec((1, TQ, nq_w), lambda b, i, qb, kb, fl: (b, qb[i], 0)),
        scratch_shapes=[pltpu.VMEM((KV_GROUPS, HEAD_DIM, cols), BF16),
                        pltpu.VMEM((KV_GROUPS, 1, cols), F32),
                        pltpu.VMEM((KV_GROUPS, HEAD_DIM + ONES_ROWS, cols), F32),
                        pltpu.VMEM((n_tiles, TK, C_HEADS * TQ), F32),
                        pltpu.VMEM((nq_w, TQ), F32)],
    )
    return pl.pallas_call(
        functools.partial(_pattn_kernel, mode=mode, gate_row0=C_HEADS * (1 if mode == "sel" else 2), n_tiles=n_tiles),
        grid_spec=grid_spec,
        out_shape=jax.ShapeDtypeStruct((B, T, nq_w), F32),
        compiler_params=_params("arbitrary", "arbitrary"),
        name="attn_prompt_" + mode,
    )(jnp.asarray(qbs), jnp.asarray(kbs), jnp.asarray(fls), *args)


def _combine_kernel(x_ref, gt_ref, oc_ref, os_ref, ow_ref, gate_ref, w_ref, o_ref, mix_ref):
    gate = gate_ref[0]
    for h in range(C_HEADS):
        hs = slice(h * HEAD_DIM, (h + 1) * HEAD_DIM)
        mix_ref[:, hs] = (gate[:, h:h + 1] * oc_ref[0, :, hs]
                          + gate[:, C_HEADS + h:C_HEADS + h + 1] * os_ref[0, :, hs]
                          + gate[:, 2 * C_HEADS + h:2 * C_HEADS + h + 1] * ow_ref[0, :, hs]).astype(BF16)
    o_ref[0] = x_ref[0] + gt_ref[0] * _dot(mix_ref[...], w_ref[...])


def _combine_call(x3, gt, oc, os_, ow, gates, w_out, tm):
    NB, R, D = x3.shape
    nq = C_HEADS * HEAD_DIM
    return pl.pallas_call(
        _combine_kernel,
        grid=(NB, R // tm),
        in_specs=[_tok_spec(tm, D), _mod_spec(gt, tm), _tok_spec(tm, nq), _tok_spec(tm, nq), _tok_spec(tm, nq),
                  _tok_spec(tm, LANES), _full_spec(w_out.shape)],
        out_specs=_tok_spec(tm, D),
        out_shape=jax.ShapeDtypeStruct(x3.shape, F32),
        scratch_shapes=[pltpu.VMEM((tm, nq), BF16)],
        compiler_params=_params("arbitrary", "arbitrary"),
        name="nsa_combine",
    )(x3, gt, oc, os_, ow, gates, w_out)


def _sum_proj_kernel(x_ref, gt_ref, oc_ref, os_ref, ow_ref, w_ref, o_ref):
    mix = (oc_ref[0] + os_ref[0] + ow_ref[0]).astype(BF16)
    o_ref[0] = x_ref[0] + gt_ref[0] * _dot(mix, w_ref[...])


def _sum_proj_call(x3, gt, oc, os_, ow, w_out, tm):
    NB, R, D = x3.shape
    nq = C_HEADS * HEAD_DIM
    return pl.pallas_call(
        _sum_proj_kernel,
        grid=(NB, R // tm),
        in_specs=[_tok_spec(tm, D), _mod_spec(gt, tm), _tok_spec(tm, nq), _tok_spec(tm, nq), _tok_spec(tm, nq),
                  _full_spec(w_out.shape)],
        out_specs=_tok_spec(tm, D),
        out_shape=jax.ShapeDtypeStruct(x3.shape, F32),
        compiler_params=_params("arbitrary", "arbitrary"),
        name="nsa_out_proj",
    )(x3, gt, oc, os_, ow, w_out)


def _lut_kernel(rb_ref, d_ref, h_ref, o_ref):
    d = d_ref[...]
    hh = h_ref[...]
    out = jnp.zeros(d.shape, F32)
    for h in range(C_HEADS):
        out = jnp.where(hh == h, _bias_lut(d, rb_ref, h), out)
    o_ref[...] = out


def _lut_call(rel_bias, dist, head):
    dist = np.ascontiguousarray(np.broadcast_to(dist, head.shape)).astype(np.int32)
    head = np.ascontiguousarray(head).astype(np.int32)
    return pl.pallas_call(
        _lut_kernel,
        grid=(1,),
        in_specs=[_SMEM_SPEC, _full_spec(dist.shape), _full_spec(head.shape)],
        out_specs=_full_spec(dist.shape),
        out_shape=jax.ShapeDtypeStruct(dist.shape, F32),
        compiler_params=_params("arbitrary"),
        name="bias_lut",
    )(rel_bias, jnp.asarray(dist), jnp.asarray(head))


def _cmp_s_kernel(q_ref, kv_ref, bias_ref, ov_ref, o_ref, sel_ref, *, n_cmp, nsb, ts, past):
    npad = kv_ref.shape[2]
    rows = HPG * ts
    t = past + lax.broadcasted_iota(jnp.int32, (rows, npad), 0) % ts
    nidx = lax.broadcasted_iota(jnp.int32, (rows, npad), 1)
    mask = ((t - (nidx * CMP_STRIDE + CMP_BLOCK - 1)) >= 0) & (nidx < n_cmp)
    tpos = past + lax.broadcasted_iota(jnp.int32, (ts, 1), 0)
    for gi in range(KV_GROUPS):
        kc = kv_ref[0, 0, :, gi * HEAD_DIM:(gi + 1) * HEAD_DIM]
        vc = kv_ref[0, 1, :, gi * HEAD_DIM:(gi + 1) * HEAD_DIM]
        s = _dot_nt(q_ref[0, gi], kc) + bias_ref[gi]
        s = jnp.where(mask, s, NEG)
        m = jnp.max(s, axis=-1, keepdims=True)
        e = jnp.where(mask, jnp.exp(s - m), 0.0)
        prob = e / jnp.maximum(jnp.sum(e, axis=-1, keepdims=True), 1e-30)
        o_ref[0, gi] = _dot(prob.astype(BF16), vc)
        hi, lo = _split_bf16(prob)
        imp16 = _dot(hi, ov_ref[...]) + _dot(lo, ov_ref[...])
        imp = imp16[0:ts]
        for p in range(1, HPG):
            imp = imp + imp16[p * ts:(p + 1) * ts]
        sel_ref[0, gi] = _select(imp, tpos, nsb).astype(F32)


def _cmp_s_bias(rel_bias, npad, ts, past):
    rows = HPG * ts
    r = np.arange(KV_GROUPS * rows)
    dist = (past + r % ts)[:, None] - (np.arange(npad)[None, :] * CMP_STRIDE + CMP_BLOCK - 1)
    head = np.broadcast_to((r // ts)[:, None], dist.shape)
    return _lut_call(rel_bias, dist, head).reshape(KV_GROUPS, rows, npad)


def _cmp_s_call(q2, kvc, bias, n_cmp, nsb, ts, past):
    Bs = q2.shape[0]
    npad = kvc.shape[2]
    rows = HPG * ts
    lpad = -(-nsb // LANES) * LANES
    ov = jnp.asarray(_overlap_np(n_cmp, nsb, npad, lpad), BF16)
    return pl.pallas_call(
        functools.partial(_cmp_s_kernel, n_cmp=n_cmp, nsb=nsb, ts=ts, past=past),
        grid=(Bs,),
        in_specs=[pl.BlockSpec((1, KV_GROUPS, rows, HEAD_DIM), lambda b: (b, 0, 0, 0)),
                  pl.BlockSpec((1, 2, npad, GH), lambda b: (b, 0, 0, 0)),
                  _full_spec(bias.shape), _full_spec(ov.shape)],
        out_specs=[pl.BlockSpec((1, KV_GROUPS, rows, HEAD_DIM), lambda b: (b, 0, 0, 0)),
                   pl.BlockSpec((1, KV_GROUPS, ts, lpad), lambda b: (b, 0, 0, 0))],
        out_shape=[jax.ShapeDtypeStruct((Bs, KV_GROUPS, rows, HEAD_DIM), F32),
                   jax.ShapeDtypeStruct((Bs, KV_GROUPS, ts, lpad), F32)],
        compiler_params=_params("arbitrary"),
        name="cmp_attn_sample",
    )(q2, kvc, bias, ov)


def _decode_core(kp, vp, kn, vn, qblk, kng_col, eexp, bias_p, bias_n, mask_p, mask_n):
    qb = (qblk * kng_col).astype(BF16)

    def logits(k, bias):
        hi, lo = _split_bf16(k * k)
        ss = _dot(hi, eexp) + _dot(lo, eexp)
        r = lax.rsqrt(ss * (1.0 / HEAD_DIM) + EPS)
        return _dot(k.astype(BF16), qb) * r + bias

    lp = jnp.where(mask_p, logits(kp, bias_p), NEG)
    ln = jnp.where(mask_n, logits(kn, bias_n), NEG)
    m = jnp.maximum(jnp.max(lp, axis=0, keepdims=True), jnp.max(ln, axis=0, keepdims=True))
    ep = jnp.where(mask_p, jnp.exp(lp - m), 0.0)
    en = jnp.where(mask_n, jnp.exp(ln - m), 0.0)
    denom = jnp.sum(ep, axis=0, keepdims=True) + jnp.sum(en, axis=0, keepdims=True)
    inv = 1.0 / jnp.maximum(denom, 1e-30)
    of = _dot_tn((ep * inv).astype(BF16), vp.astype(BF16)) + _dot_tn((en * inv).astype(BF16), vn.astype(BF16))
    ncol = of.shape[0]
    per = ncol // KV_GROUPS
    rg = lax.broadcasted_iota(jnp.int32, (ncol, HEAD_DIM), 0) // per
    o = jnp.zeros((ncol, HEAD_DIM), F32)
    for gi in range(KV_GROUPS):
        o = o + jnp.where(rg == gi, of[:, gi * HEAD_DIM:(gi + 1) * HEAD_DIM], 0.0)
    return o


def _new_key_mask(ts, ncol, rows):
    jn = lax.broadcasted_iota(jnp.int32, (rows, ncol), 0)
    tn = lax.broadcasted_iota(jnp.int32, (rows, ncol), 1) % ts
    return (jn <= tn) & (jn < ts)


def _page_copy(cache_ref, buf_ref, sem_ref, page, slot, p, col, width, page_size):
    return pltpu.make_async_copy(
        cache_ref.at[page, :, pl.ds(col, width)],
        buf_ref.at[slot, pl.ds(p * page_size, page_size), :],
        sem_ref.at[slot])


def _sel_s_kernel(pt_ref, cache_ref, new_ref, q_ref, kng_ref, eexp_ref, bp_ref, bn_ref, selp_ref, seln_ref,
                  o_ref, buf_ref, sem_ref, *, n_pages, page_size, col0, ts):
    b = pl.program_id(0)
    nb = pl.num_programs(0)
    past = n_pages * page_size
    ncol = q_ref.shape[2]

    def fetch(sb, start):
        slot = sb % 2
        for p in range(n_pages):
            cp = _page_copy(cache_ref, buf_ref, sem_ref, pt_ref[sb, p], slot, p, col0, 2 * GH, page_size)
            cp.start() if start else cp.wait()

    @pl.when(b == 0)
    def _():
        fetch(b, True)

    @pl.when(b + 1 < nb)
    def _():
        fetch(b + 1, True)

    fetch(b, False)
    slot = b % 2
    kp = buf_ref[slot, :, 0:GH]
    vp = buf_ref[slot, :, GH:2 * GH]
    nblk = past // SEL_BLOCK
    mask_p = jnp.broadcast_to(selp_ref[0], (nblk, SEL_BLOCK, ncol)).reshape(past, ncol) > 0.5
    mask_n = _new_key_mask(ts, ncol, new_ref.shape[1]) & (seln_ref[0] > 0.5)
    o_ref[0] = _decode_core(kp, vp, new_ref[0, :, 0:GH], new_ref[0, :, GH:2 * GH], q_ref[0], kng_ref[...],
                            eexp_ref[...], bp_ref[...], bn_ref[...], mask_p, mask_n)


def _win_s_kernel(st_ref, new_ref, q_ref, kng_ref, eexp_ref, bp_ref, bn_ref, o_ref, nw_ref, *, ts, kpos0):
    wb = st_ref.shape[2]
    ncol = q_ref.shape[2]
    jp = lax.broadcasted_iota(jnp.int32, (wb, ncol), 0)
    tp = lax.broadcasted_iota(jnp.int32, (wb, ncol), 1) % ts
    dist = wb + tp - jp
    mask_p = (dist >= 0) & (dist < WINDOW) & (kpos0 + jp >= 0)
    mask_n = _new_key_mask(ts, ncol, new_ref.shape[1])
    st = st_ref[0, 0]
    o_ref[0] = _decode_core(st[:, 0:GH], st[:, GH:2 * GH], new_ref[0, :, 0:GH], new_ref[0, :, GH:2 * GH],
                            q_ref[0], kng_ref[...], eexp_ref[...], bp_ref[...], bn_ref[...], mask_p, mask_n)
    wout = nw_ref.shape[2]
    keep = wout - ts
    nw_ref[0, 0, 0:keep, :] = st_ref[0, 0, pl.ds(wb - keep, keep), :]
    nw_ref[0, 0, keep:wout, :] = new_ref[0, 0:ts, :]


def _decode_bias(rel_bias, ts, key_dist):
    ncol = C_HEADS * ts
    dist = key_dist[:, np.arange(ncol) % ts]
    head = np.broadcast_to((np.arange(ncol) // ts)[None, :], dist.shape)
    return _lut_call(rel_bias, dist, head)


def _sample_bias_tables(rel_bias, ts, past, wb, npad):
    tq = np.arange(ts)[None, :]
    tail = 2 * MAX_DIST
    assert past >= tail and tail - ts >= MAX_DIST
    sel_tail = _decode_bias(rel_bias, ts, tail + tq - np.arange(tail)[:, None])
    sel_past = jnp.concatenate([jnp.broadcast_to(sel_tail[0:1], (past - tail, sel_tail.shape[1])), sel_tail], axis=0)
    return dict(
        new=_decode_bias(rel_bias, ts, tq - np.arange(8)[:, None]),
        sel=sel_past,
        win=_decode_bias(rel_bias, ts, wb + tq - np.arange(wb)[:, None]),
        cmp=_cmp_s_bias(rel_bias, npad, ts, past))


def _sel_s_call(cache3, page_table, layer, new_rows, qblk, kng_col, eexp, bias_p, bias_n, selp, seln, ts):
    Bs, n_pages = page_table.shape
    page_size = cache3.shape[1]
    past = n_pages * page_size
    ncol = qblk.shape[2]
    nblk = past // SEL_BLOCK
    m3 = lambda b, pt: (b, 0, 0)
    c2 = lambda b, pt: (0, 0)
    grid_spec = pltpu.PrefetchScalarGridSpec(
        num_scalar_prefetch=1,
        grid=(Bs,),
        in_specs=[pl.BlockSpec(memory_space=pl.ANY),
                  pl.BlockSpec((1,) + new_rows.shape[1:], m3),
                  pl.BlockSpec((1, GH, ncol), m3),
                  pl.BlockSpec(kng_col.shape, c2), pl.BlockSpec(eexp.shape, c2),
                  pl.BlockSpec(bias_p.shape, c2), pl.BlockSpec(bias_n.shape, c2),
                  pl.BlockSpec((1, nblk, 1, ncol), lambda b, pt: (b, 0, 0, 0)),
                  pl.BlockSpec((1, 1, ncol), m3)],
        out_specs=pl.BlockSpec((1, ncol, HEAD_DIM), m3),
        scratch_shapes=[pltpu.VMEM((2, past, 2 * GH), F32), pltpu.SemaphoreType.DMA((2,))],
    )
    return pl.pallas_call(
        functools.partial(_sel_s_kernel, n_pages=n_pages, page_size=page_size, col0=layer * 2 * GH, ts=ts),
        grid_spec=grid_spec,
        out_shape=jax.ShapeDtypeStruct((Bs, ncol, HEAD_DIM), F32),
        compiler_params=_params("arbitrary"),
        name="sel_attn_sample",
    )(page_table, cache3, new_rows, qblk, kng_col, eexp, bias_p, bias_n, selp, seln)


def _win_s_call(state4, layer, new_rows, qblk, kng_col, eexp, bias_p, bias_n, ts, past):
    n_l, Bs, wb, _ = state4.shape
    ncol = qblk.shape[2]
    wout = min(WINDOW, wb + ts)
    m3 = lambda b: (b, 0, 0)
    c2 = lambda b: (0, 0)
    return pl.pallas_call(
        functools.partial(_win_s_kernel, ts=ts, kpos0=past - wb),
        grid=(Bs,),
        in_specs=[pl.BlockSpec((1, 1, wb, 2 * GH), lambda b: (layer, b, 0, 0)),
                  pl.BlockSpec((1,) + new_rows.shape[1:], m3),
                  pl.BlockSpec((1, GH, ncol), m3),
                  pl.BlockSpec(kng_col.shape, c2), pl.BlockSpec(eexp.shape, c2),
                  pl.BlockSpec(bias_p.shape, c2), pl.BlockSpec(bias_n.shape, c2)],
        out_specs=[pl.BlockSpec((1, ncol, HEAD_DIM), m3),
                   pl.BlockSpec((1, 1, wout, 2 * GH), lambda b: (0, b, 0, 0))],
        out_shape=[jax.ShapeDtypeStruct((Bs, ncol, HEAD_DIM), F32),
                   jax.ShapeDtypeStruct((1, Bs, wout, 2 * GH), F32)],
        compiler_params=_params("arbitrary"),
        name="win_attn_sample",
    )(state4, new_rows, qblk, kng_col, eexp, bias_p, bias_n)


def _nsa_prompt(x, mods, g, w_in_pad, w_out, qn_g, kn_g, pe, w1, w2, rel_bias, tm):
    B, T, D = x.shape
    N = B * T
    nq = C_HEADS * HEAD_DIM
    qn, cmp_rows, sel_rows, win_rows, seln, winn, _, gates_t = _nsa_proj_call(
        x, mods[3], mods[4], g, w_in_pad, qn_g, kn_g, SCALE * LOG2E, tm)
    n_cmp = (T - CMP_BLOCK) // CMP_STRIDE + 1
    kvc = _compress_p_call(cmp_rows, pe, w1, w2, kn_g)
    o_cmp, selmask = _cmp_p_call(qn, kvc, rel_bias, gates_t, n_cmp)
    o_sel = _pattn_call(qn, seln, sel_rows, selmask, gates_t, rel_bias, "sel")
    o_win = _pattn_call(qn, winn, win_rows, None, gates_t, rel_bias, "win")
    x = _sum_proj_call(x, mods[5], o_cmp, o_sel, o_win, w_out, tm)
    wk = min(WINDOW, T)
    return x, cmp_rows, sel_rows, win_rows[:, T - wk:]


def _nsa_sample(x, mods, g, w_in_pad, w_out, qn_g, kn_g, pe, w1, w2, bias_tabs, cache_cmp3, cache_sel3,
                page_table, state_win4, layer, ts, bsz):
    R = ts * bsz
    nq = C_HEADS * HEAD_DIM
    past = page_table.shape[1] * cache_sel3.shape[1]
    assert past % CMP_STRIDE == 0 and past % SEL_BLOCK == 0 and ts <= 8 and ts <= CMP_STRIDE
    qn, cmp_rows, sel_rows, win_rows, _, _, gates, _ = _nsa_proj_call(
        x, mods[3], mods[4], g, w_in_pad, qn_g, kn_g, SCALE, R)
    q5 = jnp.transpose(qn.reshape(ts, bsz, KV_GROUPS, HPG, HEAD_DIM), (1, 2, 3, 0, 4))
    q2 = q5.reshape(bsz, KV_GROUPS, HPG * ts, HEAD_DIM)
    qd = jnp.transpose(q5.astype(F32).reshape(bsz, KV_GROUPS, HPG * ts, HEAD_DIM), (0, 1, 3, 2))
    qblk = (qd[:, :, :, None, :] * jnp.eye(KV_GROUPS, dtype=F32)[None, :, None, :, None]).reshape(bsz, GH, C_HEADS * ts)
    ncol = C_HEADS * ts
    tk = past + ts
    n_cmp = (tk - CMP_BLOCK) // CMP_STRIDE + 1
    assert n_cmp <= past // CMP_STRIDE - 1 + 1 and (n_cmp - 1) * CMP_STRIDE + CMP_BLOCK <= past
    nsb = -(-tk // SEL_BLOCK)
    kvc = _compress_s_call(cache_cmp3, page_table, layer, pe, w1, w2, kn_g)
    o_cmp, selw = _cmp_s_call(q2, kvc, bias_tabs["cmp"], n_cmp, nsb, ts, past)
    selt = jnp.transpose(selw, (0, 3, 1, 2))
    selt = jnp.broadcast_to(selt[:, :, :, None, :], selt.shape[:3] + (HPG, ts)).reshape(bsz, -1, 1, ncol)
    nblk = past // SEL_BLOCK
    assert nsb == nblk + 1
    selp, seln = selt[:, :nblk], selt[:, nblk]
    to_seq = lambda a: jnp.pad(jnp.transpose(a.reshape(ts, bsz, 2 * GH), (1, 0, 2)), ((0, 0), (0, 8 - ts), (0, 0)))
    kng_col = jnp.tile(kn_g, KV_GROUPS).reshape(GH, 1)
    eexp = np.zeros((GH, ncol), np.float32)
    for gi in range(KV_GROUPS):
        eexp[gi * HEAD_DIM:(gi + 1) * HEAD_DIM, gi * HPG * ts:(gi + 1) * HPG * ts] = 1.0
    eexp = jnp.asarray(eexp, BF16)
    o_sel = _sel_s_call(cache_sel3, page_table, layer, to_seq(sel_rows), qblk, kng_col, eexp, bias_tabs["sel"],
                        bias_tabs["new"], selp, seln, ts)
    o_win, new_win = _win_s_call(state_win4, layer, to_seq(win_rows), qblk, kng_col, eexp, bias_tabs["win"],
                                 bias_tabs["new"], ts, past)
    back = lambda o: jnp.transpose(o.reshape(bsz, KV_GROUPS, HPG, ts, HEAD_DIM), (3, 0, 1, 2, 4)).reshape(1, R, nq)
    x = _combine_call(x, mods[5], back(o_cmp), back(o_sel), back(o_win), gates, w_out, R)
    rows_out = lambda a: jnp.transpose(a.reshape(ts, bsz, 2, KV_GROUPS, HEAD_DIM), (1, 0, 2, 3, 4))
    return x, rows_out(cmp_rows), rows_out(sel_rows), new_win[0].reshape(bsz, -1, 2, KV_GROUPS, HEAD_DIM)


def kernel(x_prompt, x_sample, c_prompt, c_sample, cache_cmp_kv, cache_sel_kv, page_table, state_win_kv, state_conv, ada_w, ada_b, norm_g, ffn_w1, ffn_w2, even_w_in, even_w_out, gmlp_v_g, gmlp_ws, gmlp_bs, conv_w, conv_b, conv_ln_g, conv_ln_b, nsa_w_in, nsa_w_out, q_norm_g, k_norm_g, cmp_pe, cmp_w1, cmp_w2, rel_bias):
    B, T, D = x_prompt.shape
    Bs, Ts, _ = x_sample.shape
    depth = ada_w.shape[0]
    n_odd = nsa_w_in.shape[0]
    tm = min(512, T)
    Rs = Ts * Bs

    ada = _ada_call(jnp.concatenate([c_prompt, c_sample], axis=0), ada_w, ada_b)
    xp = x_prompt
    xs = jnp.transpose(x_sample, (1, 0, 2)).reshape(1, Rs, D)
    n_pool, page_size = cache_cmp_kv.shape[:2]
    assert page_size % CMP_STRIDE == 0
    cache_cmp3 = _chunk_major_cache(cache_cmp_kv)
    cache_sel3 = cache_sel_kv.reshape(n_pool, page_size, -1)
    state_win4 = state_win_kv.reshape(state_win_kv.shape[:3] + (2 * GH,))
    past = page_table.shape[1] * page_size
    bias_tabs = _sample_bias_tables(rel_bias, Ts, past, state_win_kv.shape[2], past // CMP_STRIDE)
    pad_cols = (-nsa_w_in.shape[2] + C_HEADS * HEAD_DIM + 6 * GH + LANES)

    cmp_p, cmp_s, sel_p, sel_s, win_p, win_s, conv_p, conv_s, v_s = ([] for _ in range(9))
    for l in range(depth):
        mp = [ada[l, :B, k * D:(k + 1) * D].reshape(B, 1, D) for k in range(9)]
        ms = [jnp.tile(ada[l, B:, k * D:(k + 1) * D], (Ts, 1)).reshape(1, Rs, D) for k in range(9)]
        w1 = ffn_w1[l].astype(BF16)
        w2 = ffn_w2[l].astype(BF16)
        xp = _ffn_call(xp, mp[0], mp[1], mp[2], norm_g[l, 0], w1[0], w2[0], tm)
        xs = _ffn_call(xs, ms[0], ms[1], ms[2], norm_g[l, 0], w1[0], w2[0], Rs)
        if l % 2 == 0:
            e = l // 2
            w_in = even_w_in[e].astype(BF16)
            w_out = even_w_out[e].astype(BF16)
            prm = (gmlp_v_g[e], gmlp_ws[e], gmlp_bs[e])
            cprm = (conv_w[e], conv_b[e], conv_ln_g[e], conv_ln_b[e])
            xp, cst_p = _even_call(xp, mp[3], mp[4], mp[5], norm_g[l, 1], w_in, w_out, *prm, *cprm, tm)
            xs, cst_s, vrow = _even_s_call(xs, ms[3], ms[4], ms[5], norm_g[l, 1], w_in, w_out, *prm,
                                           state_conv[e], *cprm, Ts, Bs)
            conv_p.append(cst_p)
            conv_s.append(cst_s)
            v_s.append(vrow)
        else:
            o = l // 2
            w_in_pad = jnp.pad(nsa_w_in[o], ((0, 0), (0, pad_cols))).astype(BF16)
            w_out = nsa_w_out[o].astype(BF16)
            prm = (norm_g[l, 1], w_in_pad, w_out, q_norm_g[o], k_norm_g[o], cmp_pe[o], cmp_w1[o], cmp_w2[o])
            xp, rc_p, rs_p, w_p = _nsa_prompt(xp, mp, *prm, rel_bias, tm)
            xs, rc_s, rs_s, w_s = _nsa_sample(xs, ms, *prm, bias_tabs, cache_cmp3, cache_sel3, page_table, state_win4,
                                              o, Ts, Bs)
            kv6 = lambda a: a.reshape(B, -1, 2, KV_GROUPS, HEAD_DIM)
            cmp_p.append(kv6(rc_p))
            sel_p.append(kv6(rs_p))
            win_p.append(kv6(w_p))
            cmp_s.append(rc_s)
            sel_s.append(rs_s)
            win_s.append(w_s)
        xp = _ffn_call(xp, mp[6], mp[7], mp[8], norm_g[l, 2], w1[1], w2[1], tm)
        xs = _ffn_call(xs, ms[6], ms[7], ms[8], norm_g[l, 2], w1[1], w2[1], Rs)
    y_sample = jnp.transpose(xs.reshape(Ts, Bs, D), (1, 0, 2))
    return (xp, y_sample, jnp.stack(cmp_p, axis=2), jnp.stack(cmp_s, axis=2), jnp.stack(sel_p, axis=2),
            jnp.stack(sel_s, axis=2), jnp.stack(win_p, axis=0), jnp.stack(win_s, axis=0),
            jnp.stack(conv_p, axis=0), jnp.stack(conv_s, axis=0), jnp.stack(v_s, axis=0))
```

```python
import functools
import math

import numpy as np
import jax
import jax.numpy as jnp
from jax import lax
from jax.experimental import pallas as pl
from jax.experimental.pallas import tpu as pltpu

F32 = jnp.float32
BF16 = jnp.bfloat16

A_GROUPS = 8
CHUNK = 128
CONV_W = 31
C_HEADS = 16
KV_GROUPS = 4
HEAD_DIM = 64
HPG = C_HEADS // KV_GROUPS
GH = KV_GROUPS * HEAD_DIM
CMP_BLOCK = 32
CMP_STRIDE = 16
SEL_BLOCK = 64
N_SEL = 16
WINDOW = 512
N_BUCKETS = 32
MAX_DIST = 128
SCALE = HEAD_DIM ** -0.5
EPS = 1e-6
NEG = -1e30
FORCE = 1e6

VMEM_LIMIT_BYTES = 60 * 2 ** 20
LANES = 128
SUBLANES = 8
MXU_DIM = 256
TQ = 256
TK = 256
STACK = 4
HIST = 32


def _params(*sem):
    return pltpu.CompilerParams(dimension_semantics=sem, vmem_limit_bytes=VMEM_LIMIT_BYTES)


def _dot(a, b):
    return jnp.dot(a, b, preferred_element_type=F32)


def _dot_nt(a, b):
    return lax.dot_general(a, b, (((1,), (1,)), ((), ())), preferred_element_type=F32)


def _dot_tn(a, b):
    return lax.dot_general(a, b, (((0,), (0,)), ((), ())), preferred_element_type=F32)


def _split_bf16(x):
    hi = x.astype(BF16)
    lo = (x - hi.astype(F32)).astype(BF16)
    return hi, lo


def _sigmoid(x):
    return 1.0 / (1.0 + jnp.exp(-x))


def _silu(x):
    return x * _sigmoid(x)


def _gelu(x):
    return 0.5 * x * (1.0 + jnp.tanh(math.sqrt(2.0 / math.pi) * (x + 0.044715 * (x * x * x))))


def _modulate(x, g, shift, scale):
    y = x * lax.rsqrt(jnp.mean(x * x, axis=-1, keepdims=True) + EPS)
    return (y * g) * (1.0 + scale) + shift


def _bucket_np(dist):
    exact = N_BUCKETS // 2
    d = np.maximum(dist, 0)
    df = np.maximum(d, 1).astype(np.float32)
    large = exact + (np.log(df / np.float32(exact)) / np.float32(math.log(MAX_DIST / exact))
                     * np.float32(N_BUCKETS - exact)).astype(np.int32)
    return np.where(d < exact, d, np.minimum(large, N_BUCKETS - 1)).astype(np.int32)


def _bucket_thresholds():
    b = _bucket_np(np.arange(4 * MAX_DIST))
    assert (np.diff(b) >= 0).all() and b[MAX_DIST] == N_BUCKETS - 1
    return [int(np.argmax(b >= k)) for k in range(N_BUCKETS)]


_THR = _bucket_thresholds()


def _bias_lut(d, rb_ref, h):
    val = jnp.full(d.shape, rb_ref[0, h], F32)
    for k in range(1, N_BUCKETS):
        val = jnp.where(d >= _THR[k], rb_ref[k, h], val)
    return val


_SMEM_SPEC = pl.BlockSpec(memory_space=pltpu.SMEM)


def _full_spec(shape):
    n = len(shape)
    return pl.BlockSpec(shape, lambda *_: (0,) * n)


def _tok_spec(tm, width):
    return pl.BlockSpec((1, tm, width), lambda b, i: (b, i, 0))


def _mod_spec(mod, tm):
    if mod.shape[1] == 1:
        return pl.BlockSpec((1, 1, mod.shape[2]), lambda b, i: (b, 0, 0))
    return pl.BlockSpec((1, tm, mod.shape[2]), lambda b, i: (b, i, 0))


def _ada_kernel(c_ref, w_ref, b_ref, o_ref):
    c = c_ref[...]
    o_ref[0] = _dot(_silu(c).astype(BF16), w_ref[0].astype(BF16)) + b_ref[0]


def _ada_call(c_all, ada_w, ada_b):
    L, D, N = ada_w.shape
    M = c_all.shape[0]
    tn = 1024
    return pl.pallas_call(
        _ada_kernel,
        grid=(L, N // tn),
        in_specs=[pl.BlockSpec((M, D), lambda l, j: (0, 0)),
                  pl.BlockSpec((1, D, tn), lambda l, j: (l, 0, j)),
                  pl.BlockSpec((1, 1, tn), lambda l, j: (l, 0, j))],
        out_specs=pl.BlockSpec((1, M, tn), lambda l, j: (l, 0, j)),
        out_shape=jax.ShapeDtypeStruct((L, M, N), F32),
        compiler_params=_params("arbitrary", "arbitrary"),
        name="ada",
    )(c_all, ada_w, ada_b.reshape(L, 1, N))


def _ffn_kernel(x_ref, sh_ref, sc_ref, gt_ref, g_ref, w1_ref, w2_ref, o_ref, acc_ref, *, ff, tf):
    x = x_ref[0]
    h = _modulate(x, g_ref[...], sh_ref[0], sc_ref[0]).astype(BF16)
    for c in range(ff // tf):
        a = _dot(h, w1_ref[:, c * tf:(c + 1) * tf])
        b = _dot(h, w1_ref[:, ff + c * tf:ff + (c + 1) * tf])
        t = (_silu(a) * b).astype(BF16)
        part = _dot(t, w2_ref[c * tf:(c + 1) * tf, :])
        if c == 0:
            acc_ref[...] = part
        else:
            acc_ref[...] += part
    o_ref[0] = x + 0.5 * gt_ref[0] * acc_ref[...]


def _ffn_call(x3, sh, sc, gt, g, w1, w2, tm):
    NB, R, D = x3.shape
    ff = w2.shape[0]
    tf = MXU_DIM
    return pl.pallas_call(
        functools.partial(_ffn_kernel, ff=ff, tf=tf),
        grid=(NB, R // tm),
        in_specs=[_tok_spec(tm, D), _mod_spec(sh, tm), _mod_spec(sc, tm), _mod_spec(gt, tm),
                  _full_spec((1, D)), _full_spec(w1.shape), _full_spec(w2.shape)],
        out_specs=_tok_spec(tm, D),
        out_shape=jax.ShapeDtypeStruct(x3.shape, F32),
        scratch_shapes=[pltpu.VMEM((tm, D), F32)],
        compiler_params=_params("arbitrary", "arbitrary"),
        name="ffn",
    )(x3, sh, sc, gt, g.reshape(1, D), w1, w2)


def _even_kernel(x_ref, sh_ref, sc_ref, gt_ref, g_ref, win_ref, wout_ref, vg_ref, ws_ref, bs_ref,
                 cw_ref, cb_ref, lg_ref, lb_ref, o_ref, cs_ref, ext_ref, sa_ref, shf_ref, *, tm, aw):
    @pl.when(pl.program_id(1) == 0)
    def _():
        ext_ref[0:HIST, :] = jnp.zeros((HIST, ext_ref.shape[1]), F32)

    x = x_ref[0]
    h = _modulate(x, g_ref[...], sh_ref[0], sc_ref[0]).astype(BF16)
    z = _dot(h, win_ref[...])
    u = _gelu(z[:, 0:aw])
    gv = _gelu(z[:, aw:2 * aw])
    v = gv * lax.rsqrt(jnp.mean(gv * gv, axis=-1, keepdims=True) + EPS) * vg_ref[...]
    vb = v.astype(BF16)
    row = lax.broadcasted_iota(jnp.int32, (CHUNK, CHUNK), 0)
    col = lax.broadcasted_iota(jnp.int32, (CHUNK, CHUNK), 1)
    wm = [jnp.where(row >= col, ws_ref[gi], 0.0).astype(BF16) for gi in range(A_GROUPS)]
    gw = aw // A_GROUPS
    first_half = col < gw
    for c in range(tm // CHUNK):
        for q in range(aw // LANES):
            vq = vb[c * CHUNK:(c + 1) * CHUNK, q * LANES:(q + 1) * LANES]
            s0 = _dot(wm[2 * q], vq)
            s1 = _dot(wm[2 * q + 1], vq)
            sa_ref[c * CHUNK:(c + 1) * CHUNK, q * LANES:(q + 1) * LANES] = (
                jnp.where(first_half, s0, s1) + bs_ref[:, q * LANES:(q + 1) * LANES])
    a_out = u * sa_ref[...]
    glu = z[:, 2 * aw:3 * aw] * _sigmoid(z[:, 3 * aw:4 * aw])
    ext_ref[HIST:HIST + tm, :] = glu
    off = HIST - (CONV_W - 1)
    conv = jnp.broadcast_to(cb_ref[...], (tm, ext_ref.shape[1]))
    for r in range(SUBLANES):
        na = (CONV_W - 1 - r) // SUBLANES + 1
        nr = tm + SUBLANES * (na - 1)
        shf_ref[0:nr, :] = ext_ref[pl.ds(off + r, nr), :]
        for a in range(na):
            k = SUBLANES * a + r
            conv = conv + shf_ref[SUBLANES * a:SUBLANES * a + tm, :] * cw_ref[k:k + 1, :]
    cs_ref[0] = ext_ref[pl.ds(tm + off, CONV_W - 1), :]
    ext_ref[0:HIST, :] = ext_ref[tm:tm + HIST, :]
    mu = jnp.mean(conv, axis=-1, keepdims=True)
    cc = conv - mu
    var = jnp.mean(cc * cc, axis=-1, keepdims=True)
    b_out = _silu(cc * lax.rsqrt(var + EPS) * lg_ref[...] + lb_ref[...])
    out = _dot(a_out.astype(BF16), wout_ref[0:aw, :]) + _dot(b_out.astype(BF16), wout_ref[aw:, :])
    o_ref[0] = x + gt_ref[0] * out


def _even_call(x3, sh, sc, gt, g, w_in, w_out, v_g, ws, bs, cw, cb, ln_g, ln_b, tm):
    B, T, D = x3.shape
    aw = v_g.shape[0]
    bw = cw.shape[1]
    assert aw == bw and aw // A_GROUPS * 2 == LANES and T % tm == 0 and tm % CHUNK == 0
    bs_exp = jnp.repeat(bs.T, aw // A_GROUPS, axis=1)
    cw_pad = jnp.pad(cw, ((0, 1), (0, 0)))
    row = lambda a: a.reshape(1, -1)
    return pl.pallas_call(
        functools.partial(_even_kernel, tm=tm, aw=aw),
        grid=(B, T // tm),
        in_specs=[_tok_spec(tm, D), _mod_spec(sh, tm), _mod_spec(sc, tm), _mod_spec(gt, tm),
                  _full_spec((1, D)), _full_spec(w_in.shape), _full_spec(w_out.shape),
                  _full_spec((1, aw)), _full_spec(ws.shape), _full_spec(bs_exp.shape),
                  _full_spec(cw_pad.shape), _full_spec((1, bw)), _full_spec((1, bw)), _full_spec((1, bw))],
        out_specs=[_tok_spec(tm, D), pl.BlockSpec((1, CONV_W - 1, bw), lambda b, i: (b, 0, 0))],
        out_shape=[jax.ShapeDtypeStruct(x3.shape, F32), jax.ShapeDtypeStruct((B, CONV_W - 1, bw), F32)],
        scratch_shapes=[pltpu.VMEM((tm + HIST, bw), F32), pltpu.VMEM((tm, aw), F32),
                        pltpu.VMEM((tm + HIST, bw), F32)],
        compiler_params=_params("arbitrary", "arbitrary"),
        name="even_prompt",
    )(x3, sh, sc, gt, row(g), w_in, w_out, row(v_g), ws, bs_exp, cw_pad, row(cb), row(ln_g), row(ln_b))


def _even_s_kernel(x_ref, sh_ref, sc_ref, gt_ref, g_ref, win_ref, wout_ref, vg_ref, coef_ref, bsa_ref,
                   st_ref, cw_ref, cb_ref, lg_ref, lb_ref, o_ref, cs_ref, v_ref, *, ts, bsz, aw):
    x = x_ref[0]
    h = _modulate(x, g_ref[...], sh_ref[0], sc_ref[0]).astype(BF16)
    z = _dot(h, win_ref[...])
    u = _gelu(z[:, 0:aw])
    gv = _gelu(z[:, aw:2 * aw])
    v = gv * lax.rsqrt(jnp.mean(gv * gv, axis=-1, keepdims=True) + EPS) * vg_ref[...]
    glu = z[:, 2 * aw:3 * aw] * _sigmoid(z[:, 3 * aw:4 * aw])
    hist = CONV_W - 1
    sl = lambda a, t: a[t * bsz:(t + 1) * bsz]
    a_parts, b_parts = [], []
    for t in range(ts):
        v_ref[t] = sl(v, t)
        s = bsa_ref[t]
        for j in range(t + 1):
            s = s + coef_ref[t, j] * sl(v, j)
        a_parts.append(sl(u, t) * s)
        conv = cb_ref[...]
        for m in range(t, hist):
            conv = conv + st_ref[m] * cw_ref[m - t:m - t + 1, :]
        for j in range(t + 1):
            conv = conv + sl(glu, j) * cw_ref[hist - t + j:hist - t + j + 1, :]
        mu = jnp.mean(conv, axis=-1, keepdims=True)
        cc = conv - mu
        var = jnp.mean(cc * cc, axis=-1, keepdims=True)
        b_parts.append(_silu(cc * lax.rsqrt(var + EPS) * lg_ref[...] + lb_ref[...]))
    for i in range(hist):
        cs_ref[i] = st_ref[i + ts] if i + ts < hist else sl(glu, i + ts - hist)
    a_out = jnp.concatenate(a_parts, axis=0).astype(BF16)
    b_out = jnp.concatenate(b_parts, axis=0).astype(BF16)
    out = _dot(a_out, wout_ref[0:aw, :]) + _dot(b_out, wout_ref[aw:, :])
    o_ref[0] = x + gt_ref[0] * out


def _even_s_call(x3, sh, sc, gt, g, w_in, w_out, v_g, ws, bs, state, cw, cb, ln_g, ln_b, ts, bsz):
    _, R, D = x3.shape
    aw = v_g.shape[0]
    bw = cw.shape[1]
    gw = aw // A_GROUPS
    hist = CONV_W - 1
    assert ts <= CHUNK and ts <= hist
    coef = jnp.repeat(jnp.transpose(ws[:, :ts, :ts], (1, 2, 0)), gw, axis=2).reshape(ts, ts, 1, aw)
    bsa = jnp.repeat(bs[:, :ts].T, gw, axis=1).reshape(ts, 1, aw)
    st = jnp.transpose(state, (1, 0, 2))
    row = lambda a: a.reshape(1, -1)
    args = (x3, sh, sc, gt, row(g), w_in, w_out, row(v_g), coef, bsa, st, cw, row(cb), row(ln_g), row(ln_b))
    x_new, cs, v = pl.pallas_call(
        functools.partial(_even_s_kernel, ts=ts, bsz=bsz, aw=aw),
        grid=(1,),
        in_specs=[_full_spec(a.shape) for a in args],
        out_specs=[_full_spec(x3.shape), _full_spec((hist, bsz, bw)), _full_spec((ts, bsz, aw))],
        out_shape=[jax.ShapeDtypeStruct(x3.shape, F32), jax.ShapeDtypeStruct((hist, bsz, bw), F32),
                   jax.ShapeDtypeStruct((ts, bsz, aw), F32)],
        compiler_params=_params("arbitrary"),
        name="even_sample",
    )(*args)
    return x_new, jnp.transpose(cs, (1, 0, 2)), jnp.transpose(v, (1, 0, 2))


def _head_rmsnorm(x, g2):
    lo = lax.broadcasted_iota(jnp.int32, (1, LANES), 1) < HEAD_DIM
    sq = x * x
    s_lo = jnp.sum(jnp.where(lo, sq, 0.0), axis=-1, keepdims=True)
    s_hi = jnp.sum(jnp.where(lo, 0.0, sq), axis=-1, keepdims=True)
    return x * lax.rsqrt(jnp.where(lo, s_lo, s_hi) * (1.0 / HEAD_DIM) + EPS) * g2


def _nsa_proj_kernel(x_ref, sh_ref, sc_ref, g_ref, w_ref, qg_ref, kg_ref, qn_ref, cmp_ref, sel_ref, win_ref,
                     seln_ref, winn_ref, gate_ref, gatet_ref, *, qscale):
    x = x_ref[0]
    h = _modulate(x, g_ref[...], sh_ref[0], sc_ref[0]).astype(BF16)
    z = _dot(h, w_ref[...])
    nq = C_HEADS * HEAD_DIM
    for c in range(nq // LANES):
        qn_ref[0, :, c * LANES:(c + 1) * LANES] = (
            _head_rmsnorm(z[:, c * LANES:(c + 1) * LANES], qg_ref[...]) * qscale).astype(BF16)
    cmp_ref[0] = z[:, nq:nq + 2 * GH]
    sel_ref[0] = z[:, nq + 2 * GH:nq + 4 * GH]
    win_ref[0] = z[:, nq + 4 * GH:nq + 6 * GH]
    for c in range(GH // LANES):
        sk = nq + 2 * GH + c * LANES
        wk = nq + 4 * GH + c * LANES
        seln_ref[0, :, c * LANES:(c + 1) * LANES] = _head_rmsnorm(z[:, sk:sk + LANES], kg_ref[...]).astype(BF16)
        winn_ref[0, :, c * LANES:(c + 1) * LANES] = _head_rmsnorm(z[:, wk:wk + LANES], kg_ref[...]).astype(BF16)
    gate = _sigmoid(z[:, nq + 6 * GH:nq + 6 * GH + LANES])
    gate_ref[0] = gate
    gatet_ref[0] = gate.T


def _nsa_proj_call(x3, sh, sc, g, w_in_pad, qn_g, kn_g, qscale, tm):
    NB, R, D = x3.shape
    nq = C_HEADS * HEAD_DIM
    outs = ((nq, BF16), (2 * GH, F32), (2 * GH, F32), (2 * GH, F32), (GH, BF16), (GH, BF16), (LANES, F32))
    tile2 = lambda a: jnp.tile(a, LANES // HEAD_DIM).reshape(1, LANES)
    return pl.pallas_call(
        functools.partial(_nsa_proj_kernel, qscale=qscale),
        grid=(NB, R // tm),
        in_specs=[_tok_spec(tm, D), _mod_spec(sh, tm), _mod_spec(sc, tm), _full_spec((1, D)),
                  _full_spec(w_in_pad.shape), _full_spec((1, LANES)), _full_spec((1, LANES))],
        out_specs=[_tok_spec(tm, w) for w, _ in outs] + [pl.BlockSpec((1, LANES, tm), lambda b, i: (b, 0, i))],
        out_shape=[jax.ShapeDtypeStruct((NB, R, w), dt) for w, dt in outs]
        + [jax.ShapeDtypeStruct((NB, LANES, R), F32)],
        compiler_params=_params("arbitrary", "arbitrary"),
        name="nsa_proj",
    )(x3, sh, sc, g.reshape(1, D), w_in_pad, tile2(qn_g), tile2(kn_g))


def _compress_core(load_rows, n, pe_ref, w1_ref, w2_ref, kng_ref, o_ref, is_k):
    hid = w2_ref.shape[1]
    nstk = CMP_STRIDE // STACK
    accs = [None] * KV_GROUPS
    ctop = jnp.zeros((1, hid), F32)
    cbot = jnp.zeros((1, hid), F32)
    pe = pe_ref[0].astype(BF16)
    gpl = LANES // HEAD_DIM
    for l4 in range(nstk):
        xs = [[load_rows(l4 * STACK + i, s) for s in range(KV_GROUPS // gpl)] for i in range(STACK)]
        w = w1_ref[0, l4]
        r = _dot(pe, w)
        ctop = ctop + r[l4:l4 + 1, 0:hid]
        cbot = cbot + r[nstk + l4:nstk + l4 + 1, hid:2 * hid]
        for gi in range(KV_GROUPS):
            lo = (gi % gpl) * HEAD_DIM
            xcat = jnp.concatenate([x[gi // gpl][:, lo:lo + HEAD_DIM] for x in xs], axis=1).astype(BF16)
            part = _dot(xcat, w)
            accs[gi] = part if accs[gi] is None else accs[gi] + part
    for gi in range(KV_GROUPS):
        a = accs[gi][:, 0:hid] + ctop
        b = pltpu.roll(accs[gi][:, hid:2 * hid] + cbot, n - 1, 0)
        y = _dot(_gelu(a + b).astype(BF16), w2_ref[0])
        yn = y * lax.rsqrt(jnp.mean(y * y, axis=-1, keepdims=True) + EPS) * kng_ref[...]
        o_ref[0, 0, :, gi * HEAD_DIM:(gi + 1) * HEAD_DIM] = jnp.where(is_k, yn, y).astype(o_ref.dtype)


def _compress_p_kernel(x_ref, pe_ref, w1_ref, w2_ref, kng_ref, o_ref, *, n):
    kv = pl.program_id(1)
    slabs = 2 * GH // LANES
    load = lambda l, s: x_ref[0, pl.ds(l * slabs + kv * (slabs // 2) + s, n, stride=CMP_STRIDE * slabs), :]
    _compress_core(load, n, pe_ref, w1_ref, w2_ref, kng_ref, o_ref, kv == 0)


def _compress_weights(pe, w1, w2):
    hid = w1.shape[-1]
    w1r = w1.reshape(2, CMP_BLOCK, HEAD_DIM, hid)
    pair = jnp.concatenate([w1r[:, :CMP_STRIDE], w1r[:, CMP_STRIDE:]], axis=-1)
    w1c = pair.reshape(2, CMP_STRIDE // STACK, STACK * HEAD_DIM, 2 * hid).astype(BF16)
    return pe.reshape(2, 2 * CMP_STRIDE // STACK, STACK * HEAD_DIM), w1c, w2.astype(BF16)


def _compress_p_call(cmp_rows, pe, w1, w2, kn_g):
    B, T, _ = cmp_rows.shape
    n = T // CMP_STRIDE
    pe, w1r, w2r = _compress_weights(pe, w1, w2)
    hid = w1.shape[-1]
    slabs = 2 * GH // LANES
    cmp_rows = cmp_rows.reshape(B, T * slabs, LANES)
    return pl.pallas_call(
        functools.partial(_compress_p_kernel, n=n),
        grid=(B, 2),
        in_specs=[pl.BlockSpec((1, T * slabs, LANES), lambda b, kv: (b, 0, 0)),
                  pl.BlockSpec((1,) + pe.shape[1:], lambda b, kv: (kv, 0, 0)),
                  pl.BlockSpec((1,) + w1r.shape[1:], lambda b, kv: (kv, 0, 0, 0)),
                  pl.BlockSpec((1, hid, HEAD_DIM), lambda b, kv: (kv, 0, 0)),
                  _full_spec((1, HEAD_DIM))],
        out_specs=pl.BlockSpec((1, 1, n, GH), lambda b, kv: (b, kv, 0, 0)),
        out_shape=jax.ShapeDtypeStruct((B, 2, n, GH), BF16),
        compiler_params=_params("arbitrary", "arbitrary"),
        name="compress_prompt",
    )(cmp_rows, pe, w1r, w2r, kn_g.reshape(1, HEAD_DIM))


def _compress_s_kernel(pt_ref, cache_ref, pe_ref, w1_ref, w2_ref, kng_ref, o_ref, buf_ref, sem_ref,
                       *, n, n_pages, page_size, layer):
    b = pl.program_id(0)
    kv = pl.program_id(1)
    step = b * 2 + kv
    n_steps = pl.num_programs(0) * 2
    cpp = page_size // CMP_STRIDE

    def fetch(s, start):
        sb = s // 2
        slot = s % 2
        for p in range(n_pages):
            cp = pltpu.make_async_copy(
                cache_ref.at[pt_ref[sb, p], layer * 2 + s % 2],
                buf_ref.at[slot, :, pl.ds(p * cpp, cpp), :],
                sem_ref.at[slot])
            cp.start() if start else cp.wait()

    @pl.when(step == 0)
    def _():
        fetch(step, True)

    @pl.when(step + 1 < n_steps)
    def _():
        fetch(step + 1, True)

    fetch(step, False)
    slot = step % 2
    load = lambda l, s: buf_ref[slot, l, :, s * LANES:(s + 1) * LANES]
    _compress_core(load, n, pe_ref, w1_ref, w2_ref, kng_ref, o_ref, kv == 0)


def _chunk_major_cache(cache):
    n_pool, page_size = cache.shape[:2]
    c = cache.reshape(n_pool, page_size // CMP_STRIDE, CMP_STRIDE, -1, GH)
    return jnp.transpose(c, (0, 3, 2, 1, 4))


def _compress_s_call(cache_cm, page_table, layer, pe, w1, w2, kn_g):
    Bs, n_pages = page_table.shape
    page_size = cache_cm.shape[2] * cache_cm.shape[3]
    past = n_pages * page_size
    n = past // CMP_STRIDE
    pe, w1r, w2r = _compress_weights(pe, w1, w2)
    hid = w1.shape[-1]
    grid_spec = pltpu.PrefetchScalarGridSpec(
        num_scalar_prefetch=1,
        grid=(Bs, 2),
        in_specs=[pl.BlockSpec(memory_space=pl.ANY),
                  pl.BlockSpec((1,) + pe.shape[1:], lambda b, kv, pt: (kv, 0, 0)),
                  pl.BlockSpec((1,) + w1r.shape[1:], lambda b, kv, pt: (kv, 0, 0, 0)),
                  pl.BlockSpec((1, hid, HEAD_DIM), lambda b, kv, pt: (kv, 0, 0)),
                  pl.BlockSpec((1, HEAD_DIM), lambda b, kv, pt: (0, 0))],
        out_specs=pl.BlockSpec((1, 1, n, GH), lambda b, kv, pt: (b, kv, 0, 0)),
        scratch_shapes=[pltpu.VMEM((2, CMP_STRIDE, n, GH), F32), pltpu.SemaphoreType.DMA((2,))],
    )
    return pl.pallas_call(
        functools.partial(_compress_s_kernel, n=n, n_pages=n_pages, page_size=page_size, layer=layer),
        grid_spec=grid_spec,
        out_shape=jax.ShapeDtypeStruct((Bs, 2, n, GH), BF16),
        compiler_params=_params("arbitrary", "arbitrary"),
        name="compress_sample",
    )(page_table, cache_cm, pe, w1r, w2r, kn_g.reshape(1, HEAD_DIM))


def _select(imp, tpos, nsb):
    j = lax.broadcasted_iota(jnp.int32, imp.shape, 1)
    cur = tpos // SEL_BLOCK
    valid = (j * SEL_BLOCK <= tpos) & (j < nsb)
    forced = (j == 0) | (j == cur) | (j == cur - 1)
    impf = jnp.where(valid, jnp.where(forced, FORCE, imp), NEG)
    rank = jnp.zeros(imp.shape, jnp.int32)
    for jp in range(nsb):
        c = impf[:, jp:jp + 1]
        beats = (c > impf) | ((c == impf) & (jp < j))
        rank = rank + beats.astype(jnp.int32)
    return (rank < min(N_SEL, nsb)) & valid


def _select_t(imp, tpos, nsb):
    j = lax.broadcasted_iota(jnp.int32, imp.shape, 0)
    cur = tpos // SEL_BLOCK
    valid = (j * SEL_BLOCK <= tpos) & (j < nsb)
    forced = (j == 0) | (j == cur) | (j == cur - 1)
    impf = jnp.where(valid, jnp.where(forced, FORCE, imp), NEG)
    rank = jnp.zeros(imp.shape, jnp.int32)
    for jp in range(nsb):
        c = impf[jp:jp + 1, :]
        beats = (c > impf) | ((c == impf) & (jp < j))
        rank = rank + beats.astype(jnp.int32)
    return (rank < min(N_SEL, nsb)) & valid


def _overlap_np(n_cmp, nsb, rows, cols):
    cs = np.arange(n_cmp)[:, None] * CMP_STRIDE
    ss = np.arange(nsb)[None, :] * SEL_BLOCK
    ov = np.clip(np.minimum(cs + CMP_BLOCK, ss + SEL_BLOCK) - np.maximum(cs, ss), 0, None).astype(np.float32) / CMP_STRIDE
    out = np.zeros((rows, cols), np.float32)
    out[:n_cmp, :nsb] = ov
    return out


def _stacked_qt(qt, gi):
    return jnp.concatenate([qt[(gi * HPG + p) * HEAD_DIM:(gi * HPG + p + 1) * HEAD_DIM] for p in range(HPG)],
                           axis=1).astype(BF16)


def _cmp_p_kernel(rb_ref, q_ref, kv_ref, ovt_ref, gt_ref, o_ref, sel_ref, bias_ref, ot_ref, *, n_cmp, nsb):
    qb = pl.program_id(0)
    npad = kv_ref.shape[2]
    cols = HPG * TQ

    @pl.when(pl.program_id(1) == 0)
    def _():
        n = lax.broadcasted_iota(jnp.int32, (npad, TQ), 0)
        t = qb * TQ + lax.broadcasted_iota(jnp.int32, (npad, TQ), 1)
        d = t - (n * CMP_STRIDE + CMP_BLOCK - 1)
        for h in range(C_HEADS):
            bias_ref[:, h * TQ:(h + 1) * TQ] = _bias_lut(d, rb_ref, h) * LOG2E

    qt = q_ref[0].astype(F32).T
    n4 = lax.broadcasted_iota(jnp.int32, (npad, cols), 0)
    t4 = qb * TQ + (lax.broadcasted_iota(jnp.int32, (npad, cols), 1) & (TQ - 1))
    mask = ((t4 - (n4 * CMP_STRIDE + CMP_BLOCK - 1)) >= 0) & (n4 < n_cmp)
    kc = kv_ref[0, 0]
    vct = kv_ref[0, 1].astype(F32).T
    tpos = qb * TQ + lax.broadcasted_iota(jnp.int32, (nsb, TQ), 1)
    for gi in range(KV_GROUPS):
        s = _dot(kc[:, gi * HEAD_DIM:(gi + 1) * HEAD_DIM], _stacked_qt(qt, gi)) + bias_ref[:, gi * cols:(gi + 1) * cols]
        s = jnp.where(mask, s, NEG)
        m = jnp.max(s, axis=0, keepdims=True)
        e = jnp.where(mask, jnp.exp2(s - m), 0.0)
        prob = e * (1.0 / jnp.maximum(jnp.sum(e, axis=0, keepdims=True), 1e-30))
        o = _dot(vct[gi * HEAD_DIM:(gi + 1) * HEAD_DIM].astype(BF16), prob.astype(BF16))
        psum = prob[:, 0:TQ]
        for p in range(HPG):
            h = gi * HPG + p
            ot_ref[h * HEAD_DIM:(h + 1) * HEAD_DIM, :] = o[:, p * TQ:(p + 1) * TQ] * gt_ref[0, h:h + 1, :]
            if p:
                psum = psum + prob[:, p * TQ:(p + 1) * TQ]
        hi, lo = _split_bf16(psum)
        imp = _dot(ovt_ref[...], hi) + _dot(ovt_ref[...], lo)
        sel_ref[0, gi] = _select_t(imp, tpos, nsb).astype(F32)
    o_ref[0] = ot_ref[...].T.astype(o_ref.dtype)


def _cmp_p_call(qn, kvc, rel_bias, gates_t, n_cmp):
    B, T, nq = qn.shape
    npad = kvc.shape[2]
    nsb = -(-T // SEL_BLOCK)
    assert nsb % SUBLANES == 0
    ovt = jnp.asarray(_overlap_np(n_cmp, nsb, npad, nsb).T, BF16)
    return pl.pallas_call(
        functools.partial(_cmp_p_kernel, n_cmp=n_cmp, nsb=nsb),
        grid=(T // TQ, B),
        in_specs=[_SMEM_SPEC,
                  pl.BlockSpec((1, TQ, nq), lambda i, b: (b, i, 0)),
                  pl.BlockSpec((1, 2, npad, GH), lambda i, b: (b, 0, 0, 0)),
                  _full_spec(ovt.shape),
                  pl.BlockSpec((1, LANES, TQ), lambda i, b: (b, 0, i))],
        out_specs=[pl.BlockSpec((1, TQ, nq), lambda i, b: (b, i, 0)),
                   pl.BlockSpec((1, KV_GROUPS, nsb, TQ), lambda i, b: (b, 0, 0, i))],
        out_shape=[jax.ShapeDtypeStruct((B, T, nq), BF16),
                   jax.ShapeDtypeStruct((B, KV_GROUPS, nsb, T), F32)],
        scratch_shapes=[pltpu.VMEM((npad, C_HEADS * TQ), F32), pltpu.VMEM((nq, TQ), F32)],
        compiler_params=_params("arbitrary", "arbitrary"),
        name="cmp_attn_prompt",
    )(rel_bias, qn, kvc, ovt, gates_t)


ONES_ROWS = 16


LOG2E = math.log2(math.e)


def _pattn_kernel(qb_ref, kb_ref, fl_ref, rb_ref, q_ref, k_ref, v_ref, *rest, mode, gate_row0, n_tiles):
    if mode == "sel":
        sel_ref, gt_ref, o_ref, qst_ref, m_ref, acc_ref, bias_ref, ot_ref = rest
    else:
        gt_ref, o_ref, qst_ref, m_ref, acc_ref, bias_ref, ot_ref = rest
    i = pl.program_id(1)
    qb = qb_ref[i]
    kb = kb_ref[i]
    cols = HPG * TQ

    @pl.when((pl.program_id(0) == 0) & (i == 0))
    def _():
        row = lax.broadcasted_iota(jnp.int32, (TK, TQ), 0)
        col = lax.broadcasted_iota(jnp.int32, (TK, TQ), 1)
        for oi in range(n_tiles):
            d = oi * TQ + col - row
            vis = d >= 0
            if mode == "win":
                vis = vis & (d < WINDOW)
            for h in range(C_HEADS):
                bias_ref[oi, :, h * TQ:(h + 1) * TQ] = jnp.where(vis, _bias_lut(d, rb_ref, h) * LOG2E, NEG)

    @pl.when((fl_ref[i] & 1) == 1)
    def _():
        qt = q_ref[0].astype(F32).T
        for gi in range(KV_GROUPS):
            qst_ref[gi] = _stacked_qt(qt, gi)
        m_ref[...] = jnp.full(m_ref.shape, NEG, F32)
        acc_ref[...] = jnp.zeros(acc_ref.shape, F32)

    off = qb * TQ - kb * TK
    oi = jnp.minimum(off // TQ, n_tiles - 1)
    k = k_ref[0]
    vt = v_ref[0].T
    ones = jnp.ones((ONES_ROWS, TK), BF16)
    ss, m_olds, m_news = [], [], []
    for gi in range(KV_GROUPS):
        s = _dot(k[:, gi * HEAD_DIM:(gi + 1) * HEAD_DIM], qst_ref[gi]) + bias_ref[oi, :, gi * cols:(gi + 1) * cols]
        if mode == "sel":
            sm = sel_ref[0, gi, 0]
            blk = jnp.concatenate([jnp.broadcast_to(sm[r:r + 1], (SEL_BLOCK, TQ)) for r in range(TK // SEL_BLOCK)], axis=0)
            s = jnp.where(jnp.concatenate([blk > 0.5] * HPG, axis=1), s, NEG)
        m_old = m_ref[gi]
        m_olds.append(m_old)
        m_news.append(jnp.maximum(m_old, jnp.max(s, axis=0, keepdims=True)))
        ss.append(s)
    es = [jnp.exp2(ss[gi] - m_news[gi]).astype(BF16) for gi in range(KV_GROUPS)]
    pvs = []
    for gi in range(KV_GROUPS):
        vext = jnp.concatenate([vt[gi * HEAD_DIM:(gi + 1) * HEAD_DIM].astype(BF16), ones], axis=0)
        pvs.append(_dot(vext, es[gi]))
    for gi in range(KV_GROUPS):
        alpha = jnp.exp2(m_olds[gi] - m_news[gi])
        acc_ref[gi] = alpha * acc_ref[gi] + pvs[gi]
        m_ref[gi] = m_news[gi]

    @pl.when((fl_ref[i] & 2) == 2)
    def _():
        for gi in range(KV_GROUPS):
            a = acc_ref[gi]
            o = a[0:HEAD_DIM] * (1.0 / jnp.maximum(a[HEAD_DIM:HEAD_DIM + 1], 1e-30))
            for p in range(HPG):
                h = gi * HPG + p
                ot_ref[h * HEAD_DIM:(h + 1) * HEAD_DIM, :] = (
                    o[:, p * TQ:(p + 1) * TQ] * gt_ref[0, gate_row0 + h:gate_row0 + h + 1, :])
        o_ref[0] = ot_ref[...].T.astype(o_ref.dtype)


def _pair_tables(nq, mode):
    qbs, kbs, fls = [], [], []
    for qb in range(nq):
        hi = (qb * TQ + TQ - 1) // TK
        lo = 0 if mode == "sel" else max((qb * TQ - WINDOW + 1) // TK, 0)
        for kb in range(lo, hi + 1):
            qbs.append(qb)
            kbs.append(kb)
            fls.append((1 if kb == lo else 0) | (2 if kb == hi else 0))
    return tuple(np.array(a, np.int32) for a in (qbs, kbs, fls))


def _pattn_call(qn, kn, rows, selmask, gates_t, rel_bias, mode):
    B, T, nq_w = qn.shape
    assert T % TK == 0 and TK % TQ == 0
    qbs, kbs, fls = _pair_tables(T // TQ, mode)
    n_tiles = int(np.max(qbs * TQ - kbs * TK)) // TQ + 1
    if mode == "sel":
        n_tiles = min(n_tiles, (MAX_DIST + TK - 1) // TQ + 2)
        assert (n_tiles - 1) * TQ - (TK - 1) >= MAX_DIST or n_tiles == int(np.max(qbs * TQ - kbs * TK)) // TQ + 1
    in_specs = [_SMEM_SPEC,
                pl.BlockSpec((1, TQ, nq_w), lambda b, i, qb, kb, fl: (b, qb[i], 0)),
                pl.BlockSpec((1, TK, GH), lambda b, i, qb, kb, fl: (b, kb[i], 0)),
                pl.BlockSpec((1, TK, GH), lambda b, i, qb, kb, fl: (b, kb[i], 1))]
    args = [rel_bias, qn, kn, rows]
    if mode == "sel":
        nblk = TK // SEL_BLOCK
        in_specs.append(pl.BlockSpec((1, KV_GROUPS, 1, nblk, TQ), lambda b, i, qb, kb, fl: (b, 0, kb[i], 0, qb[i])))
        args.append(selmask.reshape(B, KV_GROUPS, T // TK, nblk, T))
    in_specs.append(pl.BlockSpec((1, LANES, TQ), lambda b, i, qb, kb, fl: (b, 0, qb[i])))
    args.append(gates_t)
    cols = HPG * TQ
    grid_spec = pltpu.PrefetchScalarGridSpec(
        num_scalar_prefetch=3,
        grid=(B, int(qbs.shape[0])),
        in_specs=in_specs,
        out_specs=pl.BlockSpec((1, TQ, nq_w), lambda b, i, qb, kb, fl: (b, qb[i], 0)),
        scratch_shapes=[pltpu.VMEM((KV_GROUPS, HEAD_DIM, cols), BF16),
                        pltpu.VMEM((KV_GROUPS, 1, cols), F32),
                        pltpu.VMEM((KV_GROUPS, HEAD_DIM + ONES_ROWS, cols), F32),
                        pltpu.VMEM((n_tiles, TK, C_HEADS * TQ), F32),
                        pltpu.VMEM((nq_w, TQ), F32)],
    )
    return pl.pallas_call(
        functools.partial(_pattn_kernel, mode=mode, gate_row0=C_HEADS * (1 if mode == "sel" else 2), n_tiles=n_tiles),
        grid_spec=grid_spec,
        out_shape=jax.ShapeDtypeStruct((B, T, nq_w), BF16),
        compiler_params=_params("arbitrary", "arbitrary"),
        name="attn_prompt_" + mode,
    )(jnp.asarray(qbs), jnp.asarray(kbs), jnp.asarray(fls), *args)


def _combine_kernel(x_ref, gt_ref, oc_ref, os_ref, ow_ref, gate_ref, w_ref, o_ref, mix_ref):
    gate = gate_ref[0]
    for h in range(C_HEADS):
        hs = slice(h * HEAD_DIM, (h + 1) * HEAD_DIM)
        mix_ref[:, hs] = (gate[:, h:h + 1] * oc_ref[0, :, hs]
                          + gate[:, C_HEADS + h:C_HEADS + h + 1] * os_ref[0, :, hs]
                          + gate[:, 2 * C_HEADS + h:2 * C_HEADS + h + 1] * ow_ref[0, :, hs]).astype(BF16)
    o_ref[0] = x_ref[0] + gt_ref[0] * _dot(mix_ref[...], w_ref[...])


def _combine_call(x3, gt, oc, os_, ow, gates, w_out, tm):
    NB, R, D = x3.shape
    nq = C_HEADS * HEAD_DIM
    return pl.pallas_call(
        _combine_kernel,
        grid=(NB, R // tm),
        in_specs=[_tok_spec(tm, D), _mod_spec(gt, tm), _tok_spec(tm, nq), _tok_spec(tm, nq), _tok_spec(tm, nq),
                  _tok_spec(tm, LANES), _full_spec(w_out.shape)],
        out_specs=_tok_spec(tm, D),
        out_shape=jax.ShapeDtypeStruct(x3.shape, F32),
        scratch_shapes=[pltpu.VMEM((tm, nq), BF16)],
        compiler_params=_params("arbitrary", "arbitrary"),
        name="nsa_combine",
    )(x3, gt, oc, os_, ow, gates, w_out)


def _sum_proj_kernel(x_ref, gt_ref, oc_ref, os_ref, ow_ref, w_ref, o_ref):
    mix = (oc_ref[0].astype(F32) + os_ref[0].astype(F32) + ow_ref[0].astype(F32)).astype(BF16)
    o_ref[0] = x_ref[0] + gt_ref[0] * _dot(mix, w_ref[...])


def _sum_proj_call(x3, gt, oc, os_, ow, w_out, tm):
    NB, R, D = x3.shape
    nq = C_HEADS * HEAD_DIM
    return pl.pallas_call(
        _sum_proj_kernel,
        grid=(NB, R // tm),
        in_specs=[_tok_spec(tm, D), _mod_spec(gt, tm), _tok_spec(tm, nq), _tok_spec(tm, nq), _tok_spec(tm, nq),
                  _full_spec(w_out.shape)],
        out_specs=_tok_spec(tm, D),
        out_shape=jax.ShapeDtypeStruct(x3.shape, F32),
        compiler_params=_params("arbitrary", "arbitrary"),
        name="nsa_out_proj",
    )(x3, gt, oc, os_, ow, w_out)


def _lut_kernel(rb_ref, d_ref, h_ref, o_ref):
    d = d_ref[...]
    hh = h_ref[...]
    out = jnp.zeros(d.shape, F32)
    for h in range(C_HEADS):
        out = jnp.where(hh == h, _bias_lut(d, rb_ref, h), out)
    o_ref[...] = out


def _lut_call(rel_bias, dist, head):
    dist = np.ascontiguousarray(np.broadcast_to(dist, head.shape)).astype(np.int32)
    head = np.ascontiguousarray(head).astype(np.int32)
    return pl.pallas_call(
        _lut_kernel,
        grid=(1,),
        in_specs=[_SMEM_SPEC, _full_spec(dist.shape), _full_spec(head.shape)],
        out_specs=_full_spec(dist.shape),
        out_shape=jax.ShapeDtypeStruct(dist.shape, F32),
        compiler_params=_params("arbitrary"),
        name="bias_lut",
    )(rel_bias, jnp.asarray(dist), jnp.asarray(head))


def _cmp_s_kernel(q_ref, kv_ref, bias_ref, ov_ref, o_ref, sel_ref, imp_ref, *, n_cmp, nsb, ts, past):
    npad = kv_ref.shape[2]
    rows = HPG * ts
    t = past + lax.broadcasted_iota(jnp.int32, (rows, npad), 0) % ts
    nidx = lax.broadcasted_iota(jnp.int32, (rows, npad), 1)
    mask = ((t - (nidx * CMP_STRIDE + CMP_BLOCK - 1)) >= 0) & (nidx < n_cmp)
    for gi in range(KV_GROUPS):
        kc = kv_ref[0, 0, :, gi * HEAD_DIM:(gi + 1) * HEAD_DIM]
        vc = kv_ref[0, 1, :, gi * HEAD_DIM:(gi + 1) * HEAD_DIM]
        s = _dot_nt(q_ref[0, gi], kc) + bias_ref[gi]
        s = jnp.where(mask, s, NEG)
        m = jnp.max(s, axis=-1, keepdims=True)
        e = jnp.where(mask, jnp.exp(s - m), 0.0)
        prob = e / jnp.maximum(jnp.sum(e, axis=-1, keepdims=True), 1e-30)
        o_ref[0, gi] = _dot(prob.astype(BF16), vc)
        hi, lo = _split_bf16(prob)
        imp16 = _dot(hi, ov_ref[...]) + _dot(lo, ov_ref[...])
        imp = imp16[0:ts]
        for p in range(1, HPG):
            imp = imp + imp16[p * ts:(p + 1) * ts]
        imp_ref[gi * ts:(gi + 1) * ts, :] = imp
    tpos = past + lax.broadcasted_iota(jnp.int32, (KV_GROUPS * ts, 1), 0) % ts
    sel_all = _select(imp_ref[...], tpos, nsb).astype(F32)
    for gi in range(KV_GROUPS):
        sel_ref[0, gi] = sel_all[gi * ts:(gi + 1) * ts]


def _cmp_s_bias(rel_bias, npad, ts, past):
    rows = HPG * ts
    r = np.arange(KV_GROUPS * rows)
    dist = (past + r % ts)[:, None] - (np.arange(npad)[None, :] * CMP_STRIDE + CMP_BLOCK - 1)
    head = np.broadcast_to((r // ts)[:, None], dist.shape)
    return _lut_call(rel_bias, dist, head).reshape(KV_GROUPS, rows, npad)


def _cmp_s_call(q2, kvc, bias, n_cmp, nsb, ts, past):
    Bs = q2.shape[0]
    npad = kvc.shape[2]
    rows = HPG * ts
    lpad = -(-nsb // LANES) * LANES
    ov = jnp.asarray(_overlap_np(n_cmp, nsb, npad, lpad), BF16)
    return pl.pallas_call(
        functools.partial(_cmp_s_kernel, n_cmp=n_cmp, nsb=nsb, ts=ts, past=past),
        grid=(Bs,),
        in_specs=[pl.BlockSpec((1, KV_GROUPS, rows, HEAD_DIM), lambda b: (b, 0, 0, 0)),
                  pl.BlockSpec((1, 2, npad, GH), lambda b: (b, 0, 0, 0)),
                  _full_spec(bias.shape), _full_spec(ov.shape)],
        out_specs=[pl.BlockSpec((1, KV_GROUPS, rows, HEAD_DIM), lambda b: (b, 0, 0, 0)),
                   pl.BlockSpec((1, KV_GROUPS, ts, lpad), lambda b: (b, 0, 0, 0))],
        out_shape=[jax.ShapeDtypeStruct((Bs, KV_GROUPS, rows, HEAD_DIM), F32),
                   jax.ShapeDtypeStruct((Bs, KV_GROUPS, ts, lpad), F32)],
        scratch_shapes=[pltpu.VMEM((KV_GROUPS * ts, lpad), F32)],
        compiler_params=_params("arbitrary"),
        name="cmp_attn_sample",
    )(q2, kvc, bias, ov)


def _decode_core(kp, vp, kn, vn, qblk, kng_col, eexp, bias_p, bias_n, mask_p, mask_n):
    qb = (qblk * kng_col).astype(BF16)

    def logits(k, bias):
        hi, lo = _split_bf16(k * k)
        ss = _dot(hi, eexp) + _dot(lo, eexp)
        r = lax.rsqrt(ss * (1.0 / HEAD_DIM) + EPS)
        return _dot(k.astype(BF16), qb) * r + bias

    lp = jnp.where(mask_p, logits(kp, bias_p), NEG)
    ln = jnp.where(mask_n, logits(kn, bias_n), NEG)
    m = jnp.maximum(jnp.max(lp, axis=0, keepdims=True), jnp.max(ln, axis=0, keepdims=True))
    ep = jnp.where(mask_p, jnp.exp(lp - m), 0.0)
    en = jnp.where(mask_n, jnp.exp(ln - m), 0.0)
    denom = jnp.sum(ep, axis=0, keepdims=True) + jnp.sum(en, axis=0, keepdims=True)
    inv = 1.0 / jnp.maximum(denom, 1e-30)
    of = _dot_tn((ep * inv).astype(BF16), vp.astype(BF16)) + _dot_tn((en * inv).astype(BF16), vn.astype(BF16))
    ncol = of.shape[0]
    per = ncol // KV_GROUPS
    rg = lax.broadcasted_iota(jnp.int32, (ncol, HEAD_DIM), 0) // per
    o = jnp.zeros((ncol, HEAD_DIM), F32)
    for gi in range(KV_GROUPS):
        o = o + jnp.where(rg == gi, of[:, gi * HEAD_DIM:(gi + 1) * HEAD_DIM], 0.0)
    return o


def _new_key_mask(ts, ncol, rows):
    jn = lax.broadcasted_iota(jnp.int32, (rows, ncol), 0)
    tn = lax.broadcasted_iota(jnp.int32, (rows, ncol), 1) % ts
    return (jn <= tn) & (jn < ts)


def _page_copy(cache_ref, buf_ref, sem_ref, page, slot, p, col, width, page_size):
    return pltpu.make_async_copy(
        cache_ref.at[page, :, pl.ds(col, width)],
        buf_ref.at[slot, pl.ds(p * page_size, page_size), :],
        sem_ref.at[slot])


def _sel_s_kernel(pt_ref, cache_ref, new_ref, q_ref, kng_ref, eexp_ref, bp_ref, bn_ref, selp_ref, seln_ref,
                  o_ref, buf_ref, sem_ref, *, n_pages, page_size, col0, ts):
    b = pl.program_id(0)
    nb = pl.num_programs(0)
    past = n_pages * page_size
    ncol = q_ref.shape[2]

    def fetch(sb, start):
        slot = sb % 2
        for p in range(n_pages):
            cp = _page_copy(cache_ref, buf_ref, sem_ref, pt_ref[sb, p], slot, p, col0, 2 * GH, page_size)
            cp.start() if start else cp.wait()

    @pl.when(b == 0)
    def _():
        fetch(b, True)

    @pl.when(b + 1 < nb)
    def _():
        fetch(b + 1, True)

    fetch(b, False)
    slot = b % 2
    kp = buf_ref[slot, :, 0:GH]
    vp = buf_ref[slot, :, GH:2 * GH]
    nblk = past // SEL_BLOCK
    mask_p = jnp.broadcast_to(selp_ref[0], (nblk, SEL_BLOCK, ncol)).reshape(past, ncol) > 0.5
    mask_n = _new_key_mask(ts, ncol, new_ref.shape[1]) & (seln_ref[0] > 0.5)
    o_ref[0] = _decode_core(kp, vp, new_ref[0, :, 0:GH], new_ref[0, :, GH:2 * GH], q_ref[0], kng_ref[...],
                            eexp_ref[...], bp_ref[...], bn_ref[...], mask_p, mask_n)


def _win_s_kernel(st_ref, new_ref, q_ref, kng_ref, eexp_ref, bp_ref, bn_ref, o_ref, nw_ref, *, ts, kpos0):
    wb = st_ref.shape[2]
    ncol = q_ref.shape[2]
    jp = lax.broadcasted_iota(jnp.int32, (wb, ncol), 0)
    tp = lax.broadcasted_iota(jnp.int32, (wb, ncol), 1) % ts
    dist = wb + tp - jp
    mask_p = (dist >= 0) & (dist < WINDOW) & (kpos0 + jp >= 0)
    mask_n = _new_key_mask(ts, ncol, new_ref.shape[1])
    st = st_ref[0, 0]
    o_ref[0] = _decode_core(st[:, 0:GH], st[:, GH:2 * GH], new_ref[0, :, 0:GH], new_ref[0, :, GH:2 * GH],
                            q_ref[0], kng_ref[...], eexp_ref[...], bp_ref[...], bn_ref[...], mask_p, mask_n)
    wout = nw_ref.shape[2]
    keep = wout - ts
    nw_ref[0, 0, 0:keep, :] = st_ref[0, 0, pl.ds(wb - keep, keep), :]
    nw_ref[0, 0, keep:wout, :] = new_ref[0, 0:ts, :]


def _decode_bias(rel_bias, ts, key_dist):
    ncol = C_HEADS * ts
    dist = key_dist[:, np.arange(ncol) % ts]
    head = np.broadcast_to((np.arange(ncol) // ts)[None, :], dist.shape)
    return _lut_call(rel_bias, dist, head)


def _sample_bias_tables(rel_bias, ts, past, wb, npad):
    tq = np.arange(ts)[None, :]
    tail = 2 * MAX_DIST
    assert past >= tail and tail - ts >= MAX_DIST
    sel_tail = _decode_bias(rel_bias, ts, tail + tq - np.arange(tail)[:, None])
    sel_past = jnp.concatenate([jnp.broadcast_to(sel_tail[0:1], (past - tail, sel_tail.shape[1])), sel_tail], axis=0)
    return dict(
        new=_decode_bias(rel_bias, ts, tq - np.arange(SUBLANES)[:, None]),
        sel=sel_past,
        win=_decode_bias(rel_bias, ts, wb + tq - np.arange(wb)[:, None]),
        cmp=_cmp_s_bias(rel_bias, npad, ts, past))


def _sel_s_call(cache3, page_table, layer, new_rows, qblk, kng_col, eexp, bias_p, bias_n, selp, seln, ts):
    Bs, n_pages = page_table.shape
    page_size = cache3.shape[1]
    past = n_pages * page_size
    ncol = qblk.shape[2]
    nblk = past // SEL_BLOCK
    m3 = lambda b, pt: (b, 0, 0)
    c2 = lambda b, pt: (0, 0)
    grid_spec = pltpu.PrefetchScalarGridSpec(
        num_scalar_prefetch=1,
        grid=(Bs,),
        in_specs=[pl.BlockSpec(memory_space=pl.ANY),
                  pl.BlockSpec((1,) + new_rows.shape[1:], m3),
                  pl.BlockSpec((1, GH, ncol), m3),
                  pl.BlockSpec(kng_col.shape, c2), pl.BlockSpec(eexp.shape, c2),
                  pl.BlockSpec(bias_p.shape, c2), pl.BlockSpec(bias_n.shape, c2),
                  pl.BlockSpec((1, nblk, 1, ncol), lambda b, pt: (b, 0, 0, 0)),
                  pl.BlockSpec((1, 1, ncol), m3)],
        out_specs=pl.BlockSpec((1, ncol, HEAD_DIM), m3),
        scratch_shapes=[pltpu.VMEM((2, past, 2 * GH), F32), pltpu.SemaphoreType.DMA((2,))],
    )
    return pl.pallas_call(
        functools.partial(_sel_s_kernel, n_pages=n_pages, page_size=page_size, col0=layer * 2 * GH, ts=ts),
        grid_spec=grid_spec,
        out_shape=jax.ShapeDtypeStruct((Bs, ncol, HEAD_DIM), F32),
        compiler_params=_params("arbitrary"),
        name="sel_attn_sample",
    )(page_table, cache3, new_rows, qblk, kng_col, eexp, bias_p, bias_n, selp, seln)


def _win_s_call(state4, layer, new_rows, qblk, kng_col, eexp, bias_p, bias_n, ts, past):
    n_l, Bs, wb, _ = state4.shape
    ncol = qblk.shape[2]
    wout = min(WINDOW, wb + ts)
    m3 = lambda b: (b, 0, 0)
    c2 = lambda b: (0, 0)
    return pl.pallas_call(
        functools.partial(_win_s_kernel, ts=ts, kpos0=past - wb),
        grid=(Bs,),
        in_specs=[pl.BlockSpec((1, 1, wb, 2 * GH), lambda b: (layer, b, 0, 0)),
                  pl.BlockSpec((1,) + new_rows.shape[1:], m3),
                  pl.BlockSpec((1, GH, ncol), m3),
                  pl.BlockSpec(kng_col.shape, c2), pl.BlockSpec(eexp.shape, c2),
                  pl.BlockSpec(bias_p.shape, c2), pl.BlockSpec(bias_n.shape, c2)],
        out_specs=[pl.BlockSpec((1, ncol, HEAD_DIM), m3),
                   pl.BlockSpec((1, 1, wout, 2 * GH), lambda b: (0, b, 0, 0))],
        out_shape=[jax.ShapeDtypeStruct((Bs, ncol, HEAD_DIM), F32),
                   jax.ShapeDtypeStruct((1, Bs, wout, 2 * GH), F32)],
        compiler_params=_params("arbitrary"),
        name="win_attn_sample",
    )(state4, new_rows, qblk, kng_col, eexp, bias_p, bias_n)


def _nsa_prompt(x, mods, g, w_in_pad, w_out, qn_g, kn_g, pe, w1, w2, rel_bias, tm):
    B, T, D = x.shape
    N = B * T
    nq = C_HEADS * HEAD_DIM
    qn, cmp_rows, sel_rows, win_rows, seln, winn, _, gates_t = _nsa_proj_call(
        x, mods[3], mods[4], g, w_in_pad, qn_g, kn_g, SCALE * LOG2E, tm)
    n_cmp = (T - CMP_BLOCK) // CMP_STRIDE + 1
    kvc = _compress_p_call(cmp_rows, pe, w1, w2, kn_g)
    o_cmp, selmask = _cmp_p_call(qn, kvc, rel_bias, gates_t, n_cmp)
    o_sel = _pattn_call(qn, seln, sel_rows, selmask, gates_t, rel_bias, "sel")
    o_win = _pattn_call(qn, winn, win_rows, None, gates_t, rel_bias, "win")
    x = _sum_proj_call(x, mods[5], o_cmp, o_sel, o_win, w_out, tm)
    wk = min(WINDOW, T)
    return x, cmp_rows, sel_rows, win_rows[:, T - wk:]


def _nsa_sample(x, mods, g, w_in_pad, w_out, qn_g, kn_g, pe, w1, w2, bias_tabs, cache_cmp3, cache_sel3,
                page_table, state_win4, layer, ts, bsz):
    R = ts * bsz
    nq = C_HEADS * HEAD_DIM
    past = page_table.shape[1] * cache_sel3.shape[1]
    assert past % CMP_STRIDE == 0 and past % SEL_BLOCK == 0 and ts <= SUBLANES and ts <= CMP_STRIDE
    qn, cmp_rows, sel_rows, win_rows, _, _, gates, _ = _nsa_proj_call(
        x, mods[3], mods[4], g, w_in_pad, qn_g, kn_g, SCALE, R)
    q5 = jnp.transpose(qn.reshape(ts, bsz, KV_GROUPS, HPG, HEAD_DIM), (1, 2, 3, 0, 4))
    q2 = q5.reshape(bsz, KV_GROUPS, HPG * ts, HEAD_DIM)
    qd = jnp.transpose(q5.astype(F32).reshape(bsz, KV_GROUPS, HPG * ts, HEAD_DIM), (0, 1, 3, 2))
    qblk = (qd[:, :, :, None, :] * jnp.eye(KV_GROUPS, dtype=F32)[None, :, None, :, None]).reshape(bsz, GH, C_HEADS * ts)
    ncol = C_HEADS * ts
    tk = past + ts
    n_cmp = (tk - CMP_BLOCK) // CMP_STRIDE + 1
    assert n_cmp <= past // CMP_STRIDE - 1 + 1 and (n_cmp - 1) * CMP_STRIDE + CMP_BLOCK <= past
    nsb = -(-tk // SEL_BLOCK)
    kvc = _compress_s_call(cache_cmp3, page_table, layer, pe, w1, w2, kn_g)
    o_cmp, selw = _cmp_s_call(q2, kvc, bias_tabs["cmp"], n_cmp, nsb, ts, past)
    selt = jnp.transpose(selw, (0, 3, 1, 2))
    selt = jnp.broadcast_to(selt[:, :, :, None, :], selt.shape[:3] + (HPG, ts)).reshape(bsz, -1, 1, ncol)
    nblk = past // SEL_BLOCK
    assert nsb == nblk + 1
    selp, seln = selt[:, :nblk], selt[:, nblk]
    to_seq = lambda a: jnp.pad(jnp.transpose(a.reshape(ts, bsz, 2 * GH), (1, 0, 2)), ((0, 0), (0, SUBLANES - ts), (0, 0)))
    kng_col = jnp.tile(kn_g, KV_GROUPS).reshape(GH, 1)
    eexp = np.zeros((GH, ncol), np.float32)
    for gi in range(KV_GROUPS):
        eexp[gi * HEAD_DIM:(gi + 1) * HEAD_DIM, gi * HPG * ts:(gi + 1) * HPG * ts] = 1.0
    eexp = jnp.asarray(eexp, BF16)
    o_sel = _sel_s_call(cache_sel3, page_table, layer, to_seq(sel_rows), qblk, kng_col, eexp, bias_tabs["sel"],
                        bias_tabs["new"], selp, seln, ts)
    o_win, new_win = _win_s_call(state_win4, layer, to_seq(win_rows), qblk, kng_col, eexp, bias_tabs["win"],
                                 bias_tabs["new"], ts, past)
    back = lambda o: jnp.transpose(o.reshape(bsz, KV_GROUPS, HPG, ts, HEAD_DIM), (3, 0, 1, 2, 4)).reshape(1, R, nq)
    x = _combine_call(x, mods[5], back(o_cmp), back(o_sel), back(o_win), gates, w_out, R)
    rows_out = lambda a: jnp.transpose(a.reshape(ts, bsz, 2, KV_GROUPS, HEAD_DIM), (1, 0, 2, 3, 4))
    return x, rows_out(cmp_rows), rows_out(sel_rows), new_win[0].reshape(bsz, -1, 2, KV_GROUPS, HEAD_DIM)


def kernel(x_prompt, x_sample, c_prompt, c_sample, cache_cmp_kv, cache_sel_kv, page_table, state_win_kv, state_conv, ada_w, ada_b, norm_g, ffn_w1, ffn_w2, even_w_in, even_w_out, gmlp_v_g, gmlp_ws, gmlp_bs, conv_w, conv_b, conv_ln_g, conv_ln_b, nsa_w_in, nsa_w_out, q_norm_g, k_norm_g, cmp_pe, cmp_w1, cmp_w2, rel_bias):
    B, T, D = x_prompt.shape
    Bs, Ts, _ = x_sample.shape
    depth = ada_w.shape[0]
    n_odd = nsa_w_in.shape[0]
    tm = min(512, T)
    Rs = Ts * Bs

    ada = _ada_call(jnp.concatenate([c_prompt, c_sample], axis=0), ada_w, ada_b)
    xp = x_prompt
    xs = jnp.transpose(x_sample, (1, 0, 2)).reshape(1, Rs, D)
    n_pool, page_size = cache_cmp_kv.shape[:2]
    assert page_size % CMP_STRIDE == 0
    cache_cmp3 = _chunk_major_cache(cache_cmp_kv)
    cache_sel3 = cache_sel_kv.reshape(n_pool, page_size, -1)
    state_win4 = state_win_kv.reshape(state_win_kv.shape[:3] + (2 * GH,))
    past = page_table.shape[1] * page_size
    bias_tabs = _sample_bias_tables(rel_bias, Ts, past, state_win_kv.shape[2], past // CMP_STRIDE)
    pad_cols = (-nsa_w_in.shape[2] + C_HEADS * HEAD_DIM + 6 * GH + LANES)

    cmp_p, cmp_s, sel_p, sel_s, win_p, win_s, conv_p, conv_s, v_s = ([] for _ in range(9))
    for l in range(depth):
        mp = [ada[l, :B, k * D:(k + 1) * D].reshape(B, 1, D) for k in range(9)]
        ms = [jnp.tile(ada[l, B:, k * D:(k + 1) * D], (Ts, 1)).reshape(1, Rs, D) for k in range(9)]
        w1 = ffn_w1[l].astype(BF16)
        w2 = ffn_w2[l].astype(BF16)
        xp = _ffn_call(xp, mp[0], mp[1], mp[2], norm_g[l, 0], w1[0], w2[0], tm)
        xs = _ffn_call(xs, ms[0], ms[1], ms[2], norm_g[l, 0], w1[0], w2[0], Rs)
        if l % 2 == 0:
            e = l // 2
            w_in = even_w_in[e].astype(BF16)
            w_out = even_w_out[e].astype(BF16)
            prm = (gmlp_v_g[e], gmlp_ws[e], gmlp_bs[e])
            cprm = (conv_w[e], conv_b[e], conv_ln_g[e], conv_ln_b[e])
            xp, cst_p = _even_call(xp, mp[3], mp[4], mp[5], norm_g[l, 1], w_in, w_out, *prm, *cprm, tm)
            xs, cst_s, vrow = _even_s_call(xs, ms[3], ms[4], ms[5], norm_g[l, 1], w_in, w_out, *prm,
                                           state_conv[e], *cprm, Ts, Bs)
            conv_p.append(cst_p)
            conv_s.append(cst_s)
            v_s.append(vrow)
        else:
            o = l // 2
            w_in_pad = jnp.pad(nsa_w_in[o], ((0, 0), (0, pad_cols))).astype(BF16)
            w_out = nsa_w_out[o].astype(BF16)
            prm = (norm_g[l, 1], w_in_pad, w_out, q_norm_g[o], k_norm_g[o], cmp_pe[o], cmp_w1[o], cmp_w2[o])
            xp, rc_p, rs_p, w_p = _nsa_prompt(xp, mp, *prm, rel_bias, tm)
            xs, rc_s, rs_s, w_s = _nsa_sample(xs, ms, *prm, bias_tabs, cache_cmp3, cache_sel3, page_table, state_win4,
                                              o, Ts, Bs)
            kv6 = lambda a: a.reshape(B, -1, 2, KV_GROUPS, HEAD_DIM)
            cmp_p.append(kv6(rc_p))
            sel_p.append(kv6(rs_p))
            win_p.append(kv6(w_p))
            cmp_s.append(rc_s)
            sel_s.append(rs_s)
            win_s.append(w_s)
        xp = _ffn_call(xp, mp[6], mp[7], mp[8], norm_g[l, 2], w1[1], w2[1], tm)
        xs = _ffn_call(xs, ms[6], ms[7], ms[8], norm_g[l, 2], w1[1], w2[1], Rs)
    y_sample = jnp.transpose(xs.reshape(Ts, Bs, D), (1, 0, 2))
    return (xp, y_sample, jnp.stack(cmp_p, axis=2), jnp.stack(cmp_s, axis=2), jnp.stack(sel_p, axis=2),
            jnp.stack(sel_s, axis=2), jnp.stack(win_p, axis=0), jnp.stack(win_s, axis=0),
            jnp.stack(conv_p, axis=0), jnp.stack(conv_s, axis=0), jnp.stack(v_s, axis=0))
```

```python
import functools
import math

import numpy as np
import jax
import jax.numpy as jnp
from jax import lax
from jax.experimental import pallas as pl
from jax.experimental.pallas import tpu as pltpu

F32 = jnp.float32
BF16 = jnp.bfloat16

A_GROUPS = 8
CHUNK = 128
CONV_W = 31
C_HEADS = 16
KV_GROUPS = 4
HEAD_DIM = 64
HPG = C_HEADS // KV_GROUPS
GH = KV_GROUPS * HEAD_DIM
CMP_BLOCK = 32
CMP_STRIDE = 16
SEL_BLOCK = 64
N_SEL = 16
WINDOW = 512
N_BUCKETS = 32
MAX_DIST = 128
SCALE = HEAD_DIM ** -0.5
EPS = 1e-6
NEG = -1e30
FORCE = 1e6

VMEM_LIMIT_BYTES = 60 * 2 ** 20
LANES = 128
SUBLANES = 8
MXU_DIM = 256
TQ = 256
TK = 256
STACK = 4
HIST = 32


def _params(*sem):
    return pltpu.CompilerParams(dimension_semantics=sem, vmem_limit_bytes=VMEM_LIMIT_BYTES)


def _dot(a, b):
    return jnp.dot(a, b, preferred_element_type=F32)


def _dot_nt(a, b):
    return lax.dot_general(a, b, (((1,), (1,)), ((), ())), preferred_element_type=F32)


def _dot_tn(a, b):
    return lax.dot_general(a, b, (((0,), (0,)), ((), ())), preferred_element_type=F32)


def _split_bf16(x):
    hi = x.astype(BF16)
    lo = (x - hi.astype(F32)).astype(BF16)
    return hi, lo


def _sigmoid(x):
    return 1.0 / (1.0 + jnp.exp(-x))


def _silu(x):
    return x * _sigmoid(x)


def _gelu(x):
    return 0.5 * x * (1.0 + jnp.tanh(math.sqrt(2.0 / math.pi) * (x + 0.044715 * (x * x * x))))


def _modulate(x, g, shift, scale):
    y = x * lax.rsqrt(jnp.mean(x * x, axis=-1, keepdims=True) + EPS)
    return (y * g) * (1.0 + scale) + shift


def _bucket_np(dist):
    exact = N_BUCKETS // 2
    d = np.maximum(dist, 0)
    df = np.maximum(d, 1).astype(np.float32)
    large = exact + (np.log(df / np.float32(exact)) / np.float32(math.log(MAX_DIST / exact))
                     * np.float32(N_BUCKETS - exact)).astype(np.int32)
    return np.where(d < exact, d, np.minimum(large, N_BUCKETS - 1)).astype(np.int32)


def _bucket_thresholds():
    b = _bucket_np(np.arange(4 * MAX_DIST))
    assert (np.diff(b) >= 0).all() and b[MAX_DIST] == N_BUCKETS - 1
    return [int(np.argmax(b >= k)) for k in range(N_BUCKETS)]


_THR = _bucket_thresholds()


def _bias_lut(d, rb_ref, h):
    val = jnp.full(d.shape, rb_ref[0, h], F32)
    for k in range(1, N_BUCKETS):
        val = jnp.where(d >= _THR[k], rb_ref[k, h], val)
    return val


_SMEM_SPEC = pl.BlockSpec(memory_space=pltpu.SMEM)


def _full_spec(shape):
    n = len(shape)
    return pl.BlockSpec(shape, lambda *_: (0,) * n)


def _tok_spec(tm, width):
    return pl.BlockSpec((1, tm, width), lambda b, i: (b, i, 0))


def _mod_spec(mod, tm):
    if mod.shape[1] == 1:
        return pl.BlockSpec((1, 1, mod.shape[2]), lambda b, i: (b, 0, 0))
    return pl.BlockSpec((1, tm, mod.shape[2]), lambda b, i: (b, i, 0))


def _ada_kernel(c_ref, w_ref, b_ref, o_ref):
    c = c_ref[...]
    o_ref[0] = _dot(_silu(c).astype(BF16), w_ref[0].astype(BF16)) + b_ref[0]


def _ada_call(c_all, ada_w, ada_b):
    L, D, N = ada_w.shape
    M = c_all.shape[0]
    tn = 1024
    return pl.pallas_call(
        _ada_kernel,
        grid=(L, N // tn),
        in_specs=[pl.BlockSpec((M, D), lambda l, j: (0, 0)),
                  pl.BlockSpec((1, D, tn), lambda l, j: (l, 0, j)),
                  pl.BlockSpec((1, 1, tn), lambda l, j: (l, 0, j))],
        out_specs=pl.BlockSpec((1, M, tn), lambda l, j: (l, 0, j)),
        out_shape=jax.ShapeDtypeStruct((L, M, N), F32),
        compiler_params=_params("arbitrary", "arbitrary"),
        name="ada",
    )(c_all, ada_w, ada_b.reshape(L, 1, N))


def _ffn_kernel(x_ref, sh_ref, sc_ref, gt_ref, g_ref, w1_ref, w2_ref, o_ref, acc_ref, *, ff, tf):
    x = x_ref[0]
    h = _modulate(x, g_ref[...], sh_ref[0], sc_ref[0]).astype(BF16)
    for c in range(ff // tf):
        a = _dot(h, w1_ref[:, c * tf:(c + 1) * tf])
        b = _dot(h, w1_ref[:, ff + c * tf:ff + (c + 1) * tf])
        t = (_silu(a) * b).astype(BF16)
        part = _dot(t, w2_ref[c * tf:(c + 1) * tf, :])
        if c == 0:
            acc_ref[...] = part
        else:
            acc_ref[...] += part
    o_ref[0] = x + 0.5 * gt_ref[0] * acc_ref[...]


def _ffn_call(x3, sh, sc, gt, g, w1, w2, tm):
    NB, R, D = x3.shape
    ff = w2.shape[0]
    tf = MXU_DIM
    return pl.pallas_call(
        functools.partial(_ffn_kernel, ff=ff, tf=tf),
        grid=(NB, R // tm),
        in_specs=[_tok_spec(tm, D), _mod_spec(sh, tm), _mod_spec(sc, tm), _mod_spec(gt, tm),
                  _full_spec((1, D)), _full_spec(w1.shape), _full_spec(w2.shape)],
        out_specs=_tok_spec(tm, D),
        out_shape=jax.ShapeDtypeStruct(x3.shape, F32),
        scratch_shapes=[pltpu.VMEM((tm, D), F32)],
        compiler_params=_params("arbitrary", "arbitrary"),
        name="ffn",
    )(x3, sh, sc, gt, g.reshape(1, D), w1, w2)


def _even_kernel(x_ref, sh_ref, sc_ref, gt_ref, g_ref, win_ref, wout_ref, vg_ref, ws_ref, bs_ref,
                 cw_ref, cb_ref, lg_ref, lb_ref, o_ref, cs_ref, ext_ref, sa_ref, shf_ref, *, tm, aw):
    @pl.when(pl.program_id(1) == 0)
    def _():
        ext_ref[0:HIST, :] = jnp.zeros((HIST, ext_ref.shape[1]), F32)

    x = x_ref[0]
    h = _modulate(x, g_ref[...], sh_ref[0], sc_ref[0]).astype(BF16)
    z = _dot(h, win_ref[...])
    u = _gelu(z[:, 0:aw])
    gv = _gelu(z[:, aw:2 * aw])
    v = gv * lax.rsqrt(jnp.mean(gv * gv, axis=-1, keepdims=True) + EPS) * vg_ref[...]
    vb = v.astype(BF16)
    row = lax.broadcasted_iota(jnp.int32, (CHUNK, CHUNK), 0)
    col = lax.broadcasted_iota(jnp.int32, (CHUNK, CHUNK), 1)
    wm = [jnp.where(row >= col, ws_ref[gi], 0.0).astype(BF16) for gi in range(A_GROUPS)]
    gw = aw // A_GROUPS
    first_half = col < gw
    for c in range(tm // CHUNK):
        for q in range(aw // LANES):
            vq = vb[c * CHUNK:(c + 1) * CHUNK, q * LANES:(q + 1) * LANES]
            s0 = _dot(wm[2 * q], vq)
            s1 = _dot(wm[2 * q + 1], vq)
            sa_ref[c * CHUNK:(c + 1) * CHUNK, q * LANES:(q + 1) * LANES] = (
                jnp.where(first_half, s0, s1) + bs_ref[:, q * LANES:(q + 1) * LANES])
    a_out = u * sa_ref[...]
    glu = z[:, 2 * aw:3 * aw] * _sigmoid(z[:, 3 * aw:4 * aw])
    ext_ref[HIST:HIST + tm, :] = glu
    off = HIST - (CONV_W - 1)
    conv = jnp.broadcast_to(cb_ref[...], (tm, ext_ref.shape[1]))
    for r in range(SUBLANES):
        na = (CONV_W - 1 - r) // SUBLANES + 1
        nr = tm + SUBLANES * (na - 1)
        shf_ref[0:nr, :] = ext_ref[pl.ds(off + r, nr), :]
        for a in range(na):
            k = SUBLANES * a + r
            conv = conv + shf_ref[SUBLANES * a:SUBLANES * a + tm, :] * cw_ref[k:k + 1, :]
    cs_ref[0] = ext_ref[pl.ds(tm + off, CONV_W - 1), :]
    ext_ref[0:HIST, :] = ext_ref[tm:tm + HIST, :]
    mu = jnp.mean(conv, axis=-1, keepdims=True)
    cc = conv - mu
    var = jnp.mean(cc * cc, axis=-1, keepdims=True)
    b_out = _silu(cc * lax.rsqrt(var + EPS) * lg_ref[...] + lb_ref[...])
    out = _dot(a_out.astype(BF16), wout_ref[0:aw, :]) + _dot(b_out.astype(BF16), wout_ref[aw:, :])
    o_ref[0] = x + gt_ref[0] * out


def _even_call(x3, sh, sc, gt, g, w_in, w_out, v_g, ws, bs, cw, cb, ln_g, ln_b, tm):
    B, T, D = x3.shape
    aw = v_g.shape[0]
    bw = cw.shape[1]
    assert aw == bw and aw // A_GROUPS * 2 == LANES and T % tm == 0 and tm % CHUNK == 0
    bs_exp = jnp.repeat(bs.T, aw // A_GROUPS, axis=1)
    cw_pad = jnp.pad(cw, ((0, 1), (0, 0)))
    row = lambda a: a.reshape(1, -1)
    return pl.pallas_call(
        functools.partial(_even_kernel, tm=tm, aw=aw),
        grid=(B, T // tm),
        in_specs=[_tok_spec(tm, D), _mod_spec(sh, tm), _mod_spec(sc, tm), _mod_spec(gt, tm),
                  _full_spec((1, D)), _full_spec(w_in.shape), _full_spec(w_out.shape),
                  _full_spec((1, aw)), _full_spec(ws.shape), _full_spec(bs_exp.shape),
                  _full_spec(cw_pad.shape), _full_spec((1, bw)), _full_spec((1, bw)), _full_spec((1, bw))],
        out_specs=[_tok_spec(tm, D), pl.BlockSpec((1, CONV_W - 1, bw), lambda b, i: (b, 0, 0))],
        out_shape=[jax.ShapeDtypeStruct(x3.shape, F32), jax.ShapeDtypeStruct((B, CONV_W - 1, bw), F32)],
        scratch_shapes=[pltpu.VMEM((tm + HIST, bw), F32), pltpu.VMEM((tm, aw), F32),
                        pltpu.VMEM((tm + HIST, bw), F32)],
        compiler_params=_params("arbitrary", "arbitrary"),
        name="even_prompt",
    )(x3, sh, sc, gt, row(g), w_in, w_out, row(v_g), ws, bs_exp, cw_pad, row(cb), row(ln_g), row(ln_b))


def _even_s_kernel(x_ref, sh_ref, sc_ref, gt_ref, g_ref, win_ref, wout_ref, vg_ref, coef_ref, bsa_ref,
                   st_ref, cw_ref, cb_ref, lg_ref, lb_ref, o_ref, cs_ref, v_ref, *, ts, bsz, aw):
    x = x_ref[0]
    h = _modulate(x, g_ref[...], sh_ref[0], sc_ref[0]).astype(BF16)
    z = _dot(h, win_ref[...])
    u = _gelu(z[:, 0:aw])
    gv = _gelu(z[:, aw:2 * aw])
    v = gv * lax.rsqrt(jnp.mean(gv * gv, axis=-1, keepdims=True) + EPS) * vg_ref[...]
    glu = z[:, 2 * aw:3 * aw] * _sigmoid(z[:, 3 * aw:4 * aw])
    hist = CONV_W - 1
    sl = lambda a, t: a[t * bsz:(t + 1) * bsz]
    a_parts, b_parts = [], []
    for t in range(ts):
        v_ref[t] = sl(v, t)
        s = bsa_ref[t]
        for j in range(t + 1):
            s = s + coef_ref[t, j] * sl(v, j)
        a_parts.append(sl(u, t) * s)
        conv = cb_ref[...]
        for m in range(t, hist):
            conv = conv + st_ref[m] * cw_ref[m - t:m - t + 1, :]
        for j in range(t + 1):
            conv = conv + sl(glu, j) * cw_ref[hist - t + j:hist - t + j + 1, :]
        mu = jnp.mean(conv, axis=-1, keepdims=True)
        cc = conv - mu
        var = jnp.mean(cc * cc, axis=-1, keepdims=True)
        b_parts.append(_silu(cc * lax.rsqrt(var + EPS) * lg_ref[...] + lb_ref[...]))
    for i in range(hist):
        cs_ref[i] = st_ref[i + ts] if i + ts < hist else sl(glu, i + ts - hist)
    a_out = jnp.concatenate(a_parts, axis=0).astype(BF16)
    b_out = jnp.concatenate(b_parts, axis=0).astype(BF16)
    out = _dot(a_out, wout_ref[0:aw, :]) + _dot(b_out, wout_ref[aw:, :])
    o_ref[0] = x + gt_ref[0] * out


def _even_s_call(x3, sh, sc, gt, g, w_in, w_out, v_g, ws, bs, state, cw, cb, ln_g, ln_b, ts, bsz):
    _, R, D = x3.shape
    aw = v_g.shape[0]
    bw = cw.shape[1]
    gw = aw // A_GROUPS
    hist = CONV_W - 1
    assert ts <= CHUNK and ts <= hist
    coef = jnp.repeat(jnp.transpose(ws[:, :ts, :ts], (1, 2, 0)), gw, axis=2).reshape(ts, ts, 1, aw)
    bsa = jnp.repeat(bs[:, :ts].T, gw, axis=1).reshape(ts, 1, aw)
    st = jnp.transpose(state, (1, 0, 2))
    row = lambda a: a.reshape(1, -1)
    args = (x3, sh, sc, gt, row(g), w_in, w_out, row(v_g), coef, bsa, st, cw, row(cb), row(ln_g), row(ln_b))
    x_new, cs, v = pl.pallas_call(
        functools.partial(_even_s_kernel, ts=ts, bsz=bsz, aw=aw),
        grid=(1,),
        in_specs=[_full_spec(a.shape) for a in args],
        out_specs=[_full_spec(x3.shape), _full_spec((hist, bsz, bw)), _full_spec((ts, bsz, aw))],
        out_shape=[jax.ShapeDtypeStruct(x3.shape, F32), jax.ShapeDtypeStruct((hist, bsz, bw), F32),
                   jax.ShapeDtypeStruct((ts, bsz, aw), F32)],
        compiler_params=_params("arbitrary"),
        name="even_sample",
    )(*args)
    return x_new, jnp.transpose(cs, (1, 0, 2)), jnp.transpose(v, (1, 0, 2))


def _head_rmsnorm(x, g2):
    lo = lax.broadcasted_iota(jnp.int32, (1, LANES), 1) < HEAD_DIM
    sq = x * x
    s_lo = jnp.sum(jnp.where(lo, sq, 0.0), axis=-1, keepdims=True)
    s_hi = jnp.sum(jnp.where(lo, 0.0, sq), axis=-1, keepdims=True)
    return x * lax.rsqrt(jnp.where(lo, s_lo, s_hi) * (1.0 / HEAD_DIM) + EPS) * g2


def _nsa_proj_kernel(x_ref, sh_ref, sc_ref, g_ref, w_ref, qg_ref, kg_ref, qn_ref, cmp_ref, sel_ref, win_ref,
                     seln_ref, winn_ref, gate_ref, gatet_ref, *, qscale):
    x = x_ref[0]
    h = _modulate(x, g_ref[...], sh_ref[0], sc_ref[0]).astype(BF16)
    z = _dot(h, w_ref[...])
    nq = C_HEADS * HEAD_DIM
    for c in range(nq // LANES):
        qn_ref[0, :, c * LANES:(c + 1) * LANES] = (
            _head_rmsnorm(z[:, c * LANES:(c + 1) * LANES], qg_ref[...]) * qscale).astype(BF16)
    cmp_ref[0] = z[:, nq:nq + 2 * GH]
    sel_ref[0] = z[:, nq + 2 * GH:nq + 4 * GH]
    win_ref[0] = z[:, nq + 4 * GH:nq + 6 * GH]
    for c in range(GH // LANES):
        sk = nq + 2 * GH + c * LANES
        wk = nq + 4 * GH + c * LANES
        seln_ref[0, :, c * LANES:(c + 1) * LANES] = _head_rmsnorm(z[:, sk:sk + LANES], kg_ref[...]).astype(BF16)
        winn_ref[0, :, c * LANES:(c + 1) * LANES] = _head_rmsnorm(z[:, wk:wk + LANES], kg_ref[...]).astype(BF16)
    gate = _sigmoid(z[:, nq + 6 * GH:nq + 6 * GH + LANES])
    gate_ref[0] = gate
    gatet_ref[0] = gate.T


def _nsa_proj_call(x3, sh, sc, g, w_in_pad, qn_g, kn_g, qscale, tm):
    NB, R, D = x3.shape
    nq = C_HEADS * HEAD_DIM
    outs = ((nq, BF16), (2 * GH, F32), (2 * GH, F32), (2 * GH, F32), (GH, BF16), (GH, BF16), (LANES, F32))
    tile2 = lambda a: jnp.tile(a, LANES // HEAD_DIM).reshape(1, LANES)
    return pl.pallas_call(
        functools.partial(_nsa_proj_kernel, qscale=qscale),
        grid=(NB, R // tm),
        in_specs=[_tok_spec(tm, D), _mod_spec(sh, tm), _mod_spec(sc, tm), _full_spec((1, D)),
                  _full_spec(w_in_pad.shape), _full_spec((1, LANES)), _full_spec((1, LANES))],
        out_specs=[_tok_spec(tm, w) for w, _ in outs] + [pl.BlockSpec((1, LANES, tm), lambda b, i: (b, 0, i))],
        out_shape=[jax.ShapeDtypeStruct((NB, R, w), dt) for w, dt in outs]
        + [jax.ShapeDtypeStruct((NB, LANES, R), F32)],
        compiler_params=_params("arbitrary", "arbitrary"),
        name="nsa_proj",
    )(x3, sh, sc, g.reshape(1, D), w_in_pad, tile2(qn_g), tile2(kn_g))


def _compress_core(load_rows, n, pe_ref, w1_ref, w2_ref, kng_ref, o_ref, is_k):
    hid = w2_ref.shape[1]
    nstk = CMP_STRIDE // STACK
    accs = [None] * KV_GROUPS
    ctop = jnp.zeros((1, hid), F32)
    cbot = jnp.zeros((1, hid), F32)
    pe = pe_ref[0].astype(BF16)
    gpl = LANES // HEAD_DIM
    for l4 in range(nstk):
        xs = [[load_rows(l4 * STACK + i, s) for s in range(KV_GROUPS // gpl)] for i in range(STACK)]
        w = w1_ref[0, l4]
        r = _dot(pe, w)
        ctop = ctop + r[l4:l4 + 1, 0:hid]
        cbot = cbot + r[nstk + l4:nstk + l4 + 1, hid:2 * hid]
        for gi in range(KV_GROUPS):
            lo = (gi % gpl) * HEAD_DIM
            xcat = jnp.concatenate([x[gi // gpl][:, lo:lo + HEAD_DIM] for x in xs], axis=1).astype(BF16)
            part = _dot(xcat, w)
            accs[gi] = part if accs[gi] is None else accs[gi] + part
    for gi in range(KV_GROUPS):
        a = accs[gi][:, 0:hid] + ctop
        b = pltpu.roll(accs[gi][:, hid:2 * hid] + cbot, n - 1, 0)
        y = _dot(_gelu(a + b).astype(BF16), w2_ref[0])
        yn = y * lax.rsqrt(jnp.mean(y * y, axis=-1, keepdims=True) + EPS) * kng_ref[...]
        o_ref[0, 0, :, gi * HEAD_DIM:(gi + 1) * HEAD_DIM] = jnp.where(is_k, yn, y).astype(o_ref.dtype)


def _compress_p_kernel(x_ref, pe_ref, w1_ref, w2_ref, kng_ref, o_ref, *, n):
    kv = pl.program_id(1)
    slabs = 2 * GH // LANES
    load = lambda l, s: x_ref[0, pl.ds(l * slabs + kv * (slabs // 2) + s, n, stride=CMP_STRIDE * slabs), :]
    _compress_core(load, n, pe_ref, w1_ref, w2_ref, kng_ref, o_ref, kv == 0)


def _compress_weights(pe, w1, w2):
    hid = w1.shape[-1]
    w1r = w1.reshape(2, CMP_BLOCK, HEAD_DIM, hid)
    pair = jnp.concatenate([w1r[:, :CMP_STRIDE], w1r[:, CMP_STRIDE:]], axis=-1)
    w1c = pair.reshape(2, CMP_STRIDE // STACK, STACK * HEAD_DIM, 2 * hid).astype(BF16)
    return pe.reshape(2, 2 * CMP_STRIDE // STACK, STACK * HEAD_DIM), w1c, w2.astype(BF16)


def _compress_p_call(cmp_rows, pe, w1, w2, kn_g):
    B, T, _ = cmp_rows.shape
    n = T // CMP_STRIDE
    pe, w1r, w2r = _compress_weights(pe, w1, w2)
    hid = w1.shape[-1]
    slabs = 2 * GH // LANES
    cmp_rows = cmp_rows.reshape(B, T * slabs, LANES)
    return pl.pallas_call(
        functools.partial(_compress_p_kernel, n=n),
        grid=(B, 2),
        in_specs=[pl.BlockSpec((1, T * slabs, LANES), lambda b, kv: (b, 0, 0)),
                  pl.BlockSpec((1,) + pe.shape[1:], lambda b, kv: (kv, 0, 0)),
                  pl.BlockSpec((1,) + w1r.shape[1:], lambda b, kv: (kv, 0, 0, 0)),
                  pl.BlockSpec((1, hid, HEAD_DIM), lambda b, kv: (kv, 0, 0)),
                  _full_spec((1, HEAD_DIM))],
        out_specs=pl.BlockSpec((1, 1, n, GH), lambda b, kv: (b, kv, 0, 0)),
        out_shape=jax.ShapeDtypeStruct((B, 2, n, GH), BF16),
        compiler_params=_params("arbitrary", "arbitrary"),
        name="compress_prompt",
    )(cmp_rows, pe, w1r, w2r, kn_g.reshape(1, HEAD_DIM))


def _compress_s_kernel(pt_ref, cache_ref, pe_ref, w1_ref, w2_ref, kng_ref, o_ref, buf_ref, sem_ref,
                       *, n, n_pages, page_size, layer):
    b = pl.program_id(0)
    kv = pl.program_id(1)
    step = b * 2 + kv
    n_steps = pl.num_programs(0) * 2
    cpp = page_size // CMP_STRIDE

    def fetch(s, start):
        sb = s // 2
        slot = s % 2
        for p in range(n_pages):
            cp = pltpu.make_async_copy(
                cache_ref.at[pt_ref[sb, p], layer * 2 + s % 2],
                buf_ref.at[slot, :, pl.ds(p * cpp, cpp), :],
                sem_ref.at[slot])
            cp.start() if start else cp.wait()

    @pl.when(step == 0)
    def _():
        fetch(step, True)

    @pl.when(step + 1 < n_steps)
    def _():
        fetch(step + 1, True)

    fetch(step, False)
    slot = step % 2
    load = lambda l, s: buf_ref[slot, l, :, s * LANES:(s + 1) * LANES]
    _compress_core(load, n, pe_ref, w1_ref, w2_ref, kng_ref, o_ref, kv == 0)


def _chunk_major_cache(cache):
    n_pool, page_size = cache.shape[:2]
    c = cache.reshape(n_pool, page_size // CMP_STRIDE, CMP_STRIDE, -1, GH)
    return jnp.transpose(c, (0, 3, 2, 1, 4))


def _compress_s_call(cache_cm, page_table, layer, pe, w1, w2, kn_g):
    Bs, n_pages = page_table.shape
    page_size = cache_cm.shape[2] * cache_cm.shape[3]
    past = n_pages * page_size
    n = past // CMP_STRIDE
    pe, w1r, w2r = _compress_weights(pe, w1, w2)
    hid = w1.shape[-1]
    grid_spec = pltpu.PrefetchScalarGridSpec(
        num_scalar_prefetch=1,
        grid=(Bs, 2),
        in_specs=[pl.BlockSpec(memory_space=pl.ANY),
                  pl.BlockSpec((1,) + pe.shape[1:], lambda b, kv, pt: (kv, 0, 0)),
                  pl.BlockSpec((1,) + w1r.shape[1:], lambda b, kv, pt: (kv, 0, 0, 0)),
                  pl.BlockSpec((1, hid, HEAD_DIM), lambda b, kv, pt: (kv, 0, 0)),
                  pl.BlockSpec((1, HEAD_DIM), lambda b, kv, pt: (0, 0))],
        out_specs=pl.BlockSpec((1, 1, n, GH), lambda b, kv, pt: (b, kv, 0, 0)),
        scratch_shapes=[pltpu.VMEM((2, CMP_STRIDE, n, GH), F32), pltpu.SemaphoreType.DMA((2,))],
    )
    return pl.pallas_call(
        functools.partial(_compress_s_kernel, n=n, n_pages=n_pages, page_size=page_size, layer=layer),
        grid_spec=grid_spec,
        out_shape=jax.ShapeDtypeStruct((Bs, 2, n, GH), BF16),
        compiler_params=_params("arbitrary", "arbitrary"),
        name="compress_sample",
    )(page_table, cache_cm, pe, w1r, w2r, kn_g.reshape(1, HEAD_DIM))


def _select(imp, tpos, nsb):
    j = lax.broadcasted_iota(jnp.int32, imp.shape, 1)
    cur = tpos // SEL_BLOCK
    valid = (j * SEL_BLOCK <= tpos) & (j < nsb)
    forced = (j == 0) | (j == cur) | (j == cur - 1)
    impf = jnp.where(valid, jnp.where(forced, FORCE, imp), NEG)
    rank = jnp.zeros(imp.shape, jnp.int32)
    for jp in range(nsb):
        c = impf[:, jp:jp + 1]
        beats = (c > impf) | ((c == impf) & (jp < j))
        rank = rank + beats.astype(jnp.int32)
    return (rank < min(N_SEL, nsb)) & valid


def _select_t(imp, tpos, nsb):
    j = lax.broadcasted_iota(jnp.int32, imp.shape, 0)
    cur = tpos // SEL_BLOCK
    valid = (j * SEL_BLOCK <= tpos) & (j < nsb)
    forced = (j == 0) | (j == cur) | (j == cur - 1)
    impf = jnp.where(valid, jnp.where(forced, FORCE, imp), NEG)
    rank = jnp.zeros(imp.shape, jnp.int32)
    for jp in range(nsb):
        c = impf[jp:jp + 1, :]
        beats = (c > impf) | ((c == impf) & (jp < j))
        rank = rank + beats.astype(jnp.int32)
    return (rank < min(N_SEL, nsb)) & valid


def _overlap_np(n_cmp, nsb, rows, cols):
    cs = np.arange(n_cmp)[:, None] * CMP_STRIDE
    ss = np.arange(nsb)[None, :] * SEL_BLOCK
    ov = np.clip(np.minimum(cs + CMP_BLOCK, ss + SEL_BLOCK) - np.maximum(cs, ss), 0, None).astype(np.float32) / CMP_STRIDE
    out = np.zeros((rows, cols), np.float32)
    out[:n_cmp, :nsb] = ov
    return out


def _stacked_qt(qt, gi):
    return jnp.concatenate([qt[(gi * HPG + p) * HEAD_DIM:(gi * HPG + p + 1) * HEAD_DIM] for p in range(HPG)],
                           axis=1).astype(BF16)


def _cmp_p_kernel(rb_ref, q_ref, kv_ref, ovt_ref, gt_ref, o_ref, sel_ref, pat_ref, ot_ref, *, n_cmp, nsb, nq_tiles):
    qb = pl.program_id(0)
    npad = kv_ref.shape[2]
    cols = HPG * TQ
    shift = TQ // CMP_STRIDE
    back = shift * (nq_tiles - 1)

    @pl.when((qb == 0) & (pl.program_id(1) == 0))
    def _():
        rows_p = pat_ref.shape[0]
        r = lax.broadcasted_iota(jnp.int32, (rows_p, TQ), 0)
        t = lax.broadcasted_iota(jnp.int32, (rows_p, TQ), 1)
        d = t - ((r - back) * CMP_STRIDE + CMP_BLOCK - 1)
        for h in range(C_HEADS):
            pat_ref[:, h * TQ:(h + 1) * TQ] = _bias_lut(d, rb_ref, h) * LOG2E

    prow = pl.multiple_of(back - shift * qb, SUBLANES)

    qt = q_ref[0].astype(F32).T
    n4 = lax.broadcasted_iota(jnp.int32, (npad, cols), 0)
    t4 = qb * TQ + (lax.broadcasted_iota(jnp.int32, (npad, cols), 1) & (TQ - 1))
    mask = ((t4 - (n4 * CMP_STRIDE + CMP_BLOCK - 1)) >= 0) & (n4 < n_cmp)
    kc = kv_ref[0, 0]
    vct = kv_ref[0, 1].astype(F32).T
    tpos = qb * TQ + lax.broadcasted_iota(jnp.int32, (nsb, TQ), 1)
    for gi in range(KV_GROUPS):
        s = (_dot(kc[:, gi * HEAD_DIM:(gi + 1) * HEAD_DIM], _stacked_qt(qt, gi))
             + pat_ref[pl.ds(prow, npad), gi * cols:(gi + 1) * cols])
        s = jnp.where(mask, s, NEG)
        m = jnp.max(s, axis=0, keepdims=True)
        e = jnp.where(mask, jnp.exp2(s - m), 0.0)
        prob = e * (1.0 / jnp.maximum(jnp.sum(e, axis=0, keepdims=True), 1e-30))
        o = _dot(vct[gi * HEAD_DIM:(gi + 1) * HEAD_DIM].astype(BF16), prob.astype(BF16))
        psum = prob[:, 0:TQ]
        for p in range(HPG):
            h = gi * HPG + p
            ot_ref[h * HEAD_DIM:(h + 1) * HEAD_DIM, :] = o[:, p * TQ:(p + 1) * TQ] * gt_ref[0, h:h + 1, :]
            if p:
                psum = psum + prob[:, p * TQ:(p + 1) * TQ]
        hi, lo = _split_bf16(psum)
        imp = _dot(ovt_ref[...], hi) + _dot(ovt_ref[...], lo)
        sel_ref[0, gi] = _select_t(imp, tpos, nsb).astype(F32)
    o_ref[0] = ot_ref[...].T.astype(o_ref.dtype)


def _cmp_p_call(qn, kvc, rel_bias, gates_t, n_cmp):
    B, T, nq = qn.shape
    npad = kvc.shape[2]
    nsb = -(-T // SEL_BLOCK)
    assert nsb % SUBLANES == 0
    ovt = jnp.asarray(_overlap_np(n_cmp, nsb, npad, nsb).T, BF16)
    nq_tiles = T // TQ
    assert TQ % CMP_STRIDE == 0 and (TQ // CMP_STRIDE) % SUBLANES == 0
    pat_rows = npad + (TQ // CMP_STRIDE) * (nq_tiles - 1)
    return pl.pallas_call(
        functools.partial(_cmp_p_kernel, n_cmp=n_cmp, nsb=nsb, nq_tiles=nq_tiles),
        grid=(nq_tiles, B),
        in_specs=[_SMEM_SPEC,
                  pl.BlockSpec((1, TQ, nq), lambda i, b: (b, i, 0)),
                  pl.BlockSpec((1, 2, npad, GH), lambda i, b: (b, 0, 0, 0)),
                  _full_spec(ovt.shape),
                  pl.BlockSpec((1, LANES, TQ), lambda i, b: (b, 0, i))],
        out_specs=[pl.BlockSpec((1, TQ, nq), lambda i, b: (b, i, 0)),
                   pl.BlockSpec((1, KV_GROUPS, nsb, TQ), lambda i, b: (b, 0, 0, i))],
        out_shape=[jax.ShapeDtypeStruct((B, T, nq), BF16),
                   jax.ShapeDtypeStruct((B, KV_GROUPS, nsb, T), F32)],
        scratch_shapes=[pltpu.VMEM((pat_rows, C_HEADS * TQ), F32), pltpu.VMEM((nq, TQ), F32)],
        compiler_params=_params("arbitrary", "arbitrary"),
        name="cmp_attn_prompt",
    )(rel_bias, qn, kvc, ovt, gates_t)


ONES_ROWS = 16


LOG2E = math.log2(math.e)


def _pattn_kernel(qb_ref, kb_ref, fl_ref, rb_ref, q_ref, k_ref, v_ref, *rest, mode, gate_row0, n_tiles):
    if mode == "sel":
        sel_ref, gt_ref, o_ref, qst_ref, m_ref, acc_ref, bias_ref, ot_ref = rest
    else:
        gt_ref, o_ref, qst_ref, m_ref, acc_ref, bias_ref, ot_ref = rest
    i = pl.program_id(1)
    qb = qb_ref[i]
    kb = kb_ref[i]
    cols = HPG * TQ

    @pl.when((pl.program_id(0) == 0) & (i == 0))
    def _():
        row = lax.broadcasted_iota(jnp.int32, (TK, TQ), 0)
        col = lax.broadcasted_iota(jnp.int32, (TK, TQ), 1)
        for oi in range(n_tiles):
            d = oi * TQ + col - row
            vis = d >= 0
            if mode == "win":
                vis = vis & (d < WINDOW)
            for h in range(C_HEADS):
                bias_ref[oi, :, h * TQ:(h + 1) * TQ] = jnp.where(vis, _bias_lut(d, rb_ref, h) * LOG2E, NEG)

    @pl.when((fl_ref[i] & 1) == 1)
    def _():
        qt = q_ref[0].astype(F32).T
        for gi in range(KV_GROUPS):
            qst_ref[gi] = _stacked_qt(qt, gi)
        m_ref[...] = jnp.full(m_ref.shape, NEG, F32)
        acc_ref[...] = jnp.zeros(acc_ref.shape, F32)

    off = qb * TQ - kb * TK
    oi = jnp.minimum(off // TQ, n_tiles - 1)
    krow = pl.multiple_of(kb * TK, TK)
    k = k_ref[0, pl.ds(krow, TK), :]
    vt = v_ref[0, pl.ds(krow, TK), :].T
    ones = jnp.ones((ONES_ROWS, TK), BF16)
    ss, m_olds, m_news = [], [], []
    for gi in range(KV_GROUPS):
        s = _dot(k[:, gi * HEAD_DIM:(gi + 1) * HEAD_DIM], qst_ref[gi]) + bias_ref[oi, :, gi * cols:(gi + 1) * cols]
        if mode == "sel":
            sm = sel_ref[0, gi, 0]
            blk = jnp.concatenate([jnp.broadcast_to(sm[r:r + 1], (SEL_BLOCK, TQ)) for r in range(TK // SEL_BLOCK)], axis=0)
            s = jnp.where(jnp.concatenate([blk > 0.5] * HPG, axis=1), s, NEG)
        m_old = m_ref[gi]
        m_olds.append(m_old)
        m_news.append(jnp.maximum(m_old, jnp.max(s, axis=0, keepdims=True)))
        ss.append(s)
    es = [jnp.exp2(ss[gi] - m_news[gi]).astype(BF16) for gi in range(KV_GROUPS)]
    pvs = []
    for gi in range(KV_GROUPS):
        vext = jnp.concatenate([vt[gi * HEAD_DIM:(gi + 1) * HEAD_DIM].astype(BF16), ones], axis=0)
        pvs.append(_dot(vext, es[gi]))
    for gi in range(KV_GROUPS):
        alpha = jnp.exp2(m_olds[gi] - m_news[gi])
        acc_ref[gi] = alpha * acc_ref[gi] + pvs[gi]
        m_ref[gi] = m_news[gi]

    @pl.when((fl_ref[i] & 2) == 2)
    def _():
        for gi in range(KV_GROUPS):
            a = acc_ref[gi]
            o = a[0:HEAD_DIM] * (1.0 / jnp.maximum(a[HEAD_DIM:HEAD_DIM + 1], 1e-30))
            for p in range(HPG):
                h = gi * HPG + p
                ot_ref[h * HEAD_DIM:(h + 1) * HEAD_DIM, :] = (
                    o[:, p * TQ:(p + 1) * TQ] * gt_ref[0, gate_row0 + h:gate_row0 + h + 1, :])
        o_ref[0] = ot_ref[...].T.astype(o_ref.dtype)


def _pair_tables(nq, mode):
    qbs, kbs, fls = [], [], []
    for qb in range(nq):
        hi = (qb * TQ + TQ - 1) // TK
        lo = 0 if mode == "sel" else max((qb * TQ - WINDOW + 1) // TK, 0)
        for kb in range(lo, hi + 1):
            qbs.append(qb)
            kbs.append(kb)
            fls.append((1 if kb == lo else 0) | (2 if kb == hi else 0))
    return tuple(np.array(a, np.int32) for a in (qbs, kbs, fls))


def _pattn_call(qn, kn, rows, selmask, gates_t, rel_bias, mode):
    B, T, nq_w = qn.shape
    assert T % TK == 0 and TK % TQ == 0
    qbs, kbs, fls = _pair_tables(T // TQ, mode)
    n_tiles = int(np.max(qbs * TQ - kbs * TK)) // TQ + 1
    if mode == "sel":
        n_tiles = min(n_tiles, (MAX_DIST + TK - 1) // TQ + 2)
        assert (n_tiles - 1) * TQ - (TK - 1) >= MAX_DIST or n_tiles == int(np.max(qbs * TQ - kbs * TK)) // TQ + 1
    in_specs = [_SMEM_SPEC,
                pl.BlockSpec((1, TQ, nq_w), lambda b, i, qb, kb, fl: (b, qb[i], 0)),
                pl.BlockSpec((1, T, GH), lambda b, i, qb, kb, fl: (b, 0, 0)),
                pl.BlockSpec((1, T, GH), lambda b, i, qb, kb, fl: (b, 0, 1))]
    args = [rel_bias, qn, kn, rows]
    if mode == "sel":
        nblk = TK // SEL_BLOCK
        in_specs.append(pl.BlockSpec((1, KV_GROUPS, 1, nblk, TQ), lambda b, i, qb, kb, fl: (b, 0, kb[i], 0, qb[i])))
        args.append(selmask.reshape(B, KV_GROUPS, T // TK, nblk, T))
    in_specs.append(pl.BlockSpec((1, LANES, TQ), lambda b, i, qb, kb, fl: (b, 0, qb[i])))
    args.append(gates_t)
    cols = HPG * TQ
    grid_spec = pltpu.PrefetchScalarGridSpec(
        num_scalar_prefetch=3,
        grid=(B, int(qbs.shape[0])),
        in_specs=in_specs,
        out_specs=pl.BlockSpec((1, TQ, nq_w), lambda b, i, qb, kb, fl: (b, qb[i], 0)),
        scratch_shapes=[pltpu.VMEM((KV_GROUPS, HEAD_DIM, cols), BF16),
                        pltpu.VMEM((KV_GROUPS, 1, cols), F32),
                        pltpu.VMEM((KV_GROUPS, HEAD_DIM + ONES_ROWS, cols), F32),
                        pltpu.VMEM((n_tiles, TK, C_HEADS * TQ), F32),
                        pltpu.VMEM((nq_w, TQ), F32)],
    )
    return pl.pallas_call(
        functools.partial(_pattn_kernel, mode=mode, gate_row0=C_HEADS * (1 if mode == "sel" else 2), n_tiles=n_tiles),
        grid_spec=grid_spec,
        out_shape=jax.ShapeDtypeStruct((B, T, nq_w), BF16),
        compiler_params=_params("arbitrary", "arbitrary"),
        name="attn_prompt_" + mode,
    )(jnp.asarray(qbs), jnp.asarray(kbs), jnp.asarray(fls), *args)


def _combine_kernel(x_ref, gt_ref, oc_ref, os_ref, ow_ref, gate_ref, w_ref, o_ref, mix_ref):
    gate = gate_ref[0]
    for h in range(C_HEADS):
        hs = slice(h * HEAD_DIM, (h + 1) * HEAD_DIM)
        mix_ref[:, hs] = (gate[:, h:h + 1] * oc_ref[0, :, hs]
                          + gate[:, C_HEADS + h:C_HEADS + h + 1] * os_ref[0, :, hs]
                          + gate[:, 2 * C_HEADS + h:2 * C_HEADS + h + 1] * ow_ref[0, :, hs]).astype(BF16)
    o_ref[0] = x_ref[0] + gt_ref[0] * _dot(mix_ref[...], w_ref[...])


def _combine_call(x3, gt, oc, os_, ow, gates, w_out, tm):
    NB, R, D = x3.shape
    nq = C_HEADS * HEAD_DIM
    return pl.pallas_call(
        _combine_kernel,
        grid=(NB, R // tm),
        in_specs=[_tok_spec(tm, D), _mod_spec(gt, tm), _tok_spec(tm, nq), _tok_spec(tm, nq), _tok_spec(tm, nq),
                  _tok_spec(tm, LANES), _full_spec(w_out.shape)],
        out_specs=_tok_spec(tm, D),
        out_shape=jax.ShapeDtypeStruct(x3.shape, F32),
        scratch_shapes=[pltpu.VMEM((tm, nq), BF16)],
        compiler_params=_params("arbitrary", "arbitrary"),
        name="nsa_combine",
    )(x3, gt, oc, os_, ow, gates, w_out)


def _sum_proj_kernel(x_ref, gt_ref, oc_ref, os_ref, ow_ref, w_ref, o_ref):
    mix = (oc_ref[0].astype(F32) + os_ref[0].astype(F32) + ow_ref[0].astype(F32)).astype(BF16)
    o_ref[0] = x_ref[0] + gt_ref[0] * _dot(mix, w_ref[...])


def _sum_proj_call(x3, gt, oc, os_, ow, w_out, tm):
    NB, R, D = x3.shape
    nq = C_HEADS * HEAD_DIM
    return pl.pallas_call(
        _sum_proj_kernel,
        grid=(NB, R // tm),
        in_specs=[_tok_spec(tm, D), _mod_spec(gt, tm), _tok_spec(tm, nq), _tok_spec(tm, nq), _tok_spec(tm, nq),
                  _full_spec(w_out.shape)],
        out_specs=_tok_spec(tm, D),
        out_shape=jax.ShapeDtypeStruct(x3.shape, F32),
        compiler_params=_params("arbitrary", "arbitrary"),
        name="nsa_out_proj",
    )(x3, gt, oc, os_, ow, w_out)


def _lut_kernel(rb_ref, d_ref, h_ref, o_ref):
    d = d_ref[...]
    hh = h_ref[...]
    out = jnp.zeros(d.shape, F32)
    for h in range(C_HEADS):
        out = jnp.where(hh == h, _bias_lut(d, rb_ref, h), out)
    o_ref[...] = out


def _lut_call(rel_bias, dist, head):
    dist = np.ascontiguousarray(np.broadcast_to(dist, head.shape)).astype(np.int32)
    head = np.ascontiguousarray(head).astype(np.int32)
    return pl.pallas_call(
        _lut_kernel,
        grid=(1,),
        in_specs=[_SMEM_SPEC, _full_spec(dist.shape), _full_spec(head.shape)],
        out_specs=_full_spec(dist.shape),
        out_shape=jax.ShapeDtypeStruct(dist.shape, F32),
        compiler_params=_params("arbitrary"),
        name="bias_lut",
    )(rel_bias, jnp.asarray(dist), jnp.asarray(head))


def _cmp_s_kernel(q_ref, kv_ref, bias_ref, ov_ref, o_ref, sel_ref, imp_ref, *, n_cmp, nsb, ts, past):
    npad = kv_ref.shape[2]
    rows = HPG * ts
    t = past + lax.broadcasted_iota(jnp.int32, (rows, npad), 0) % ts
    nidx = lax.broadcasted_iota(jnp.int32, (rows, npad), 1)
    mask = ((t - (nidx * CMP_STRIDE + CMP_BLOCK - 1)) >= 0) & (nidx < n_cmp)
    for gi in range(KV_GROUPS):
        kc = kv_ref[0, 0, :, gi * HEAD_DIM:(gi + 1) * HEAD_DIM]
        vc = kv_ref[0, 1, :, gi * HEAD_DIM:(gi + 1) * HEAD_DIM]
        s = _dot_nt(q_ref[0, gi], kc) + bias_ref[gi]
        s = jnp.where(mask, s, NEG)
        m = jnp.max(s, axis=-1, keepdims=True)
        e = jnp.where(mask, jnp.exp(s - m), 0.0)
        prob = e / jnp.maximum(jnp.sum(e, axis=-1, keepdims=True), 1e-30)
        o_ref[0, gi] = _dot(prob.astype(BF16), vc)
        hi, lo = _split_bf16(prob)
        imp16 = _dot(hi, ov_ref[...]) + _dot(lo, ov_ref[...])
        imp = imp16[0:ts]
        for p in range(1, HPG):
            imp = imp + imp16[p * ts:(p + 1) * ts]
        imp_ref[gi * ts:(gi + 1) * ts, :] = imp
    tpos = past + lax.broadcasted_iota(jnp.int32, (KV_GROUPS * ts, 1), 0) % ts
    sel_all = _select(imp_ref[...], tpos, nsb).astype(F32)
    for gi in range(KV_GROUPS):
        sel_ref[0, gi] = sel_all[gi * ts:(gi + 1) * ts]


def _cmp_s_bias(rel_bias, npad, ts, past):
    rows = HPG * ts
    r = np.arange(KV_GROUPS * rows)
    dist = (past + r % ts)[:, None] - (np.arange(npad)[None, :] * CMP_STRIDE + CMP_BLOCK - 1)
    head = np.broadcast_to((r // ts)[:, None], dist.shape)
    return _lut_call(rel_bias, dist, head).reshape(KV_GROUPS, rows, npad)


def _cmp_s_call(q2, kvc, bias, n_cmp, nsb, ts, past):
    Bs = q2.shape[0]
    npad = kvc.shape[2]
    rows = HPG * ts
    lpad = -(-nsb // LANES) * LANES
    ov = jnp.asarray(_overlap_np(n_cmp, nsb, npad, lpad), BF16)
    return pl.pallas_call(
        functools.partial(_cmp_s_kernel, n_cmp=n_cmp, nsb=nsb, ts=ts, past=past),
        grid=(Bs,),
        in_specs=[pl.BlockSpec((1, KV_GROUPS, rows, HEAD_DIM), lambda b: (b, 0, 0, 0)),
                  pl.BlockSpec((1, 2, npad, GH), lambda b: (b, 0, 0, 0)),
                  _full_spec(bias.shape), _full_spec(ov.shape)],
        out_specs=[pl.BlockSpec((1, KV_GROUPS, rows, HEAD_DIM), lambda b: (b, 0, 0, 0)),
                   pl.BlockSpec((1, KV_GROUPS, ts, lpad), lambda b: (b, 0, 0, 0))],
        out_shape=[jax.ShapeDtypeStruct((Bs, KV_GROUPS, rows, HEAD_DIM), F32),
                   jax.ShapeDtypeStruct((Bs, KV_GROUPS, ts, lpad), F32)],
        scratch_shapes=[pltpu.VMEM((KV_GROUPS * ts, lpad), F32)],
        compiler_params=_params("arbitrary"),
        name="cmp_attn_sample",
    )(q2, kvc, bias, ov)


def _decode_core(kp, vp, kn, vn, qblk, kng_col, eexp, bias_p, bias_n, mask_p, mask_n):
    qb = (qblk * kng_col).astype(BF16)

    def logits(k, bias):
        hi, lo = _split_bf16(k * k)
        ss = _dot(hi, eexp) + _dot(lo, eexp)
        r = lax.rsqrt(ss * (1.0 / HEAD_DIM) + EPS)
        return _dot(k.astype(BF16), qb) * r + bias

    lp = jnp.where(mask_p, logits(kp, bias_p), NEG)
    ln = jnp.where(mask_n, logits(kn, bias_n), NEG)
    m = jnp.maximum(jnp.max(lp, axis=0, keepdims=True), jnp.max(ln, axis=0, keepdims=True))
    ep = jnp.where(mask_p, jnp.exp(lp - m), 0.0)
    en = jnp.where(mask_n, jnp.exp(ln - m), 0.0)
    denom = jnp.sum(ep, axis=0, keepdims=True) + jnp.sum(en, axis=0, keepdims=True)
    inv = 1.0 / jnp.maximum(denom, 1e-30)
    of = _dot_tn((ep * inv).astype(BF16), vp.astype(BF16)) + _dot_tn((en * inv).astype(BF16), vn.astype(BF16))
    ncol = of.shape[0]
    per = ncol // KV_GROUPS
    rg = lax.broadcasted_iota(jnp.int32, (ncol, HEAD_DIM), 0) // per
    o = jnp.zeros((ncol, HEAD_DIM), F32)
    for gi in range(KV_GROUPS):
        o = o + jnp.where(rg == gi, of[:, gi * HEAD_DIM:(gi + 1) * HEAD_DIM], 0.0)
    return o


def _new_key_mask(ts, ncol, rows):
    jn = lax.broadcasted_iota(jnp.int32, (rows, ncol), 0)
    tn = lax.broadcasted_iota(jnp.int32, (rows, ncol), 1) % ts
    return (jn <= tn) & (jn < ts)


def _page_copy(cache_ref, buf_ref, sem_ref, page, slot, p, col, width, page_size):
    return pltpu.make_async_copy(
        cache_ref.at[page, :, pl.ds(col, width)],
        buf_ref.at[slot, pl.ds(p * page_size, page_size), :],
        sem_ref.at[slot])


def _sel_s_kernel(pt_ref, cache_ref, new_ref, q_ref, kng_ref, eexp_ref, bp_ref, bn_ref, selp_ref, seln_ref,
                  o_ref, buf_ref, sem_ref, *, n_pages, page_size, col0, ts):
    b = pl.program_id(0)
    nb = pl.num_programs(0)
    past = n_pages * page_size
    ncol = q_ref.shape[2]

    def fetch(sb, start):
        slot = sb % 2
        for p in range(n_pages):
            cp = _page_copy(cache_ref, buf_ref, sem_ref, pt_ref[sb, p], slot, p, col0, 2 * GH, page_size)
            cp.start() if start else cp.wait()

    @pl.when(b == 0)
    def _():
        fetch(b, True)

    @pl.when(b + 1 < nb)
    def _():
        fetch(b + 1, True)

    fetch(b, False)
    slot = b % 2
    kp = buf_ref[slot, :, 0:GH]
    vp = buf_ref[slot, :, GH:2 * GH]
    nblk = past // SEL_BLOCK
    mask_p = jnp.broadcast_to(selp_ref[0], (nblk, SEL_BLOCK, ncol)).reshape(past, ncol) > 0.5
    mask_n = _new_key_mask(ts, ncol, new_ref.shape[1]) & (seln_ref[0] > 0.5)
    o_ref[0] = _decode_core(kp, vp, new_ref[0, :, 0:GH], new_ref[0, :, GH:2 * GH], q_ref[0], kng_ref[...],
                            eexp_ref[...], bp_ref[...], bn_ref[...], mask_p, mask_n)


def _win_s_kernel(st_ref, new_ref, q_ref, kng_ref, eexp_ref, bp_ref, bn_ref, o_ref, nw_ref, *, ts, kpos0):
    wb = st_ref.shape[2]
    ncol = q_ref.shape[2]
    jp = lax.broadcasted_iota(jnp.int32, (wb, ncol), 0)
    tp = lax.broadcasted_iota(jnp.int32, (wb, ncol), 1) % ts
    dist = wb + tp - jp
    mask_p = (dist >= 0) & (dist < WINDOW) & (kpos0 + jp >= 0)
    mask_n = _new_key_mask(ts, ncol, new_ref.shape[1])
    st = st_ref[0, 0]
    o_ref[0] = _decode_core(st[:, 0:GH], st[:, GH:2 * GH], new_ref[0, :, 0:GH], new_ref[0, :, GH:2 * GH],
                            q_ref[0], kng_ref[...], eexp_ref[...], bp_ref[...], bn_ref[...], mask_p, mask_n)
    wout = nw_ref.shape[2]
    keep = wout - ts
    nw_ref[0, 0, 0:keep, :] = st_ref[0, 0, pl.ds(wb - keep, keep), :]
    nw_ref[0, 0, keep:wout, :] = new_ref[0, 0:ts, :]


def _decode_bias(rel_bias, ts, key_dist):
    ncol = C_HEADS * ts
    dist = key_dist[:, np.arange(ncol) % ts]
    head = np.broadcast_to((np.arange(ncol) // ts)[None, :], dist.shape)
    return _lut_call(rel_bias, dist, head)


def _sample_bias_tables(rel_bias, ts, past, wb, npad):
    tq = np.arange(ts)[None, :]
    tail = 2 * MAX_DIST
    assert past >= tail and tail - ts >= MAX_DIST
    sel_tail = _decode_bias(rel_bias, ts, tail + tq - np.arange(tail)[:, None])
    sel_past = jnp.concatenate([jnp.broadcast_to(sel_tail[0:1], (past - tail, sel_tail.shape[1])), sel_tail], axis=0)
    return dict(
        new=_decode_bias(rel_bias, ts, tq - np.arange(SUBLANES)[:, None]),
        sel=sel_past,
        win=_decode_bias(rel_bias, ts, wb + tq - np.arange(wb)[:, None]),
        cmp=_cmp_s_bias(rel_bias, npad, ts, past))


def _sel_s_call(cache3, page_table, layer, new_rows, qblk, kng_col, eexp, bias_p, bias_n, selp, seln, ts):
    Bs, n_pages = page_table.shape
    page_size = cache3.shape[1]
    past = n_pages * page_size
    ncol = qblk.shape[2]
    nblk = past // SEL_BLOCK
    m3 = lambda b, pt: (b, 0, 0)
    c2 = lambda b, pt: (0, 0)
    grid_spec = pltpu.PrefetchScalarGridSpec(
        num_scalar_prefetch=1,
        grid=(Bs,),
        in_specs=[pl.BlockSpec(memory_space=pl.ANY),
                  pl.BlockSpec((1,) + new_rows.shape[1:], m3),
                  pl.BlockSpec((1, GH, ncol), m3),
                  pl.BlockSpec(kng_col.shape, c2), pl.BlockSpec(eexp.shape, c2),
                  pl.BlockSpec(bias_p.shape, c2), pl.BlockSpec(bias_n.shape, c2),
                  pl.BlockSpec((1, nblk, 1, ncol), lambda b, pt: (b, 0, 0, 0)),
                  pl.BlockSpec((1, 1, ncol), m3)],
        out_specs=pl.BlockSpec((1, ncol, HEAD_DIM), m3),
        scratch_shapes=[pltpu.VMEM((2, past, 2 * GH), F32), pltpu.SemaphoreType.DMA((2,))],
    )
    return pl.pallas_call(
        functools.partial(_sel_s_kernel, n_pages=n_pages, page_size=page_size, col0=layer * 2 * GH, ts=ts),
        grid_spec=grid_spec,
        out_shape=jax.ShapeDtypeStruct((Bs, ncol, HEAD_DIM), F32),
        compiler_params=_params("arbitrary"),
        name="sel_attn_sample",
    )(page_table, cache3, new_rows, qblk, kng_col, eexp, bias_p, bias_n, selp, seln)


def _win_s_call(state4, layer, new_rows, qblk, kng_col, eexp, bias_p, bias_n, ts, past):
    n_l, Bs, wb, _ = state4.shape
    ncol = qblk.shape[2]
    wout = min(WINDOW, wb + ts)
    m3 = lambda b: (b, 0, 0)
    c2 = lambda b: (0, 0)
    return pl.pallas_call(
        functools.partial(_win_s_kernel, ts=ts, kpos0=past - wb),
        grid=(Bs,),
        in_specs=[pl.BlockSpec((1, 1, wb, 2 * GH), lambda b: (layer, b, 0, 0)),
                  pl.BlockSpec((1,) + new_rows.shape[1:], m3),
                  pl.BlockSpec((1, GH, ncol), m3),
                  pl.BlockSpec(kng_col.shape, c2), pl.BlockSpec(eexp.shape, c2),
                  pl.BlockSpec(bias_p.shape, c2), pl.BlockSpec(bias_n.shape, c2)],
        out_specs=[pl.BlockSpec((1, ncol, HEAD_DIM), m3),
                   pl.BlockSpec((1, 1, wout, 2 * GH), lambda b: (0, b, 0, 0))],
        out_shape=[jax.ShapeDtypeStruct((Bs, ncol, HEAD_DIM), F32),
                   jax.ShapeDtypeStruct((1, Bs, wout, 2 * GH), F32)],
        compiler_params=_params("arbitrary"),
        name="win_attn_sample",
    )(state4, new_rows, qblk, kng_col, eexp, bias_p, bias_n)


def _nsa_prompt(x, mods, g, w_in_pad, w_out, qn_g, kn_g, pe, w1, w2, rel_bias, tm):
    B, T, D = x.shape
    N = B * T
    nq = C_HEADS * HEAD_DIM
    qn, cmp_rows, sel_rows, win_rows, seln, winn, _, gates_t = _nsa_proj_call(
        x, mods[3], mods[4], g, w_in_pad, qn_g, kn_g, SCALE * LOG2E, tm)
    n_cmp = (T - CMP_BLOCK) // CMP_STRIDE + 1
    kvc = _compress_p_call(cmp_rows, pe, w1, w2, kn_g)
    o_cmp, selmask = _cmp_p_call(qn, kvc, rel_bias, gates_t, n_cmp)
    o_sel = _pattn_call(qn, seln, sel_rows, selmask, gates_t, rel_bias, "sel")
    o_win = _pattn_call(qn, winn, win_rows, None, gates_t, rel_bias, "win")
    x = _sum_proj_call(x, mods[5], o_cmp, o_sel, o_win, w_out, tm)
    wk = min(WINDOW, T)
    return x, cmp_rows, sel_rows, win_rows[:, T - wk:]


def _nsa_sample(x, mods, g, w_in_pad, w_out, qn_g, kn_g, pe, w1, w2, bias_tabs, cache_cmp3, cache_sel3,
                page_table, state_win4, layer, ts, bsz):
    R = ts * bsz
    nq = C_HEADS * HEAD_DIM
    past = page_table.shape[1] * cache_sel3.shape[1]
    assert past % CMP_STRIDE == 0 and past % SEL_BLOCK == 0 and ts <= SUBLANES and ts <= CMP_STRIDE
    qn, cmp_rows, sel_rows, win_rows, _, _, gates, _ = _nsa_proj_call(
        x, mods[3], mods[4], g, w_in_pad, qn_g, kn_g, SCALE, R)
    q5 = jnp.transpose(qn.reshape(ts, bsz, KV_GROUPS, HPG, HEAD_DIM), (1, 2, 3, 0, 4))
    q2 = q5.reshape(bsz, KV_GROUPS, HPG * ts, HEAD_DIM)
    qd = jnp.transpose(q5.astype(F32).reshape(bsz, KV_GROUPS, HPG * ts, HEAD_DIM), (0, 1, 3, 2))
    qblk = (qd[:, :, :, None, :] * jnp.eye(KV_GROUPS, dtype=F32)[None, :, None, :, None]).reshape(bsz, GH, C_HEADS * ts)
    ncol = C_HEADS * ts
    tk = past + ts
    n_cmp = (tk - CMP_BLOCK) // CMP_STRIDE + 1
    assert n_cmp <= past // CMP_STRIDE - 1 + 1 and (n_cmp - 1) * CMP_STRIDE + CMP_BLOCK <= past
    nsb = -(-tk // SEL_BLOCK)
    kvc = _compress_s_call(cache_cmp3, page_table, layer, pe, w1, w2, kn_g)
    o_cmp, selw = _cmp_s_call(q2, kvc, bias_tabs["cmp"], n_cmp, nsb, ts, past)
    selt = jnp.transpose(selw, (0, 3, 1, 2))
    selt = jnp.broadcast_to(selt[:, :, :, None, :], selt.shape[:3] + (HPG, ts)).reshape(bsz, -1, 1, ncol)
    nblk = past // SEL_BLOCK
    assert nsb == nblk + 1
    selp, seln = selt[:, :nblk], selt[:, nblk]
    to_seq = lambda a: jnp.pad(jnp.transpose(a.reshape(ts, bsz, 2 * GH), (1, 0, 2)), ((0, 0), (0, SUBLANES - ts), (0, 0)))
    kng_col = jnp.tile(kn_g, KV_GROUPS).reshape(GH, 1)
    eexp = np.zeros((GH, ncol), np.float32)
    for gi in range(KV_GROUPS):
        eexp[gi * HEAD_DIM:(gi + 1) * HEAD_DIM, gi * HPG * ts:(gi + 1) * HPG * ts] = 1.0
    eexp = jnp.asarray(eexp, BF16)
    o_sel = _sel_s_call(cache_sel3, page_table, layer, to_seq(sel_rows), qblk, kng_col, eexp, bias_tabs["sel"],
                        bias_tabs["new"], selp, seln, ts)
    o_win, new_win = _win_s_call(state_win4, layer, to_seq(win_rows), qblk, kng_col, eexp, bias_tabs["win"],
                                 bias_tabs["new"], ts, past)
    back = lambda o: jnp.transpose(o.reshape(bsz, KV_GROUPS, HPG, ts, HEAD_DIM), (3, 0, 1, 2, 4)).reshape(1, R, nq)
    x = _combine_call(x, mods[5], back(o_cmp), back(o_sel), back(o_win), gates, w_out, R)
    rows_out = lambda a: jnp.transpose(a.reshape(ts, bsz, 2, KV_GROUPS, HEAD_DIM), (1, 0, 2, 3, 4))
    return x, rows_out(cmp_rows), rows_out(sel_rows), new_win[0].reshape(bsz, -1, 2, KV_GROUPS, HEAD_DIM)


def kernel(x_prompt, x_sample, c_prompt, c_sample, cache_cmp_kv, cache_sel_kv, page_table, state_win_kv, state_conv, ada_w, ada_b, norm_g, ffn_w1, ffn_w2, even_w_in, even_w_out, gmlp_v_g, gmlp_ws, gmlp_bs, conv_w, conv_b, conv_ln_g, conv_ln_b, nsa_w_in, nsa_w_out, q_norm_g, k_norm_g, cmp_pe, cmp_w1, cmp_w2, rel_bias):
    B, T, D = x_prompt.shape
    Bs, Ts, _ = x_sample.shape
    depth = ada_w.shape[0]
    n_odd = nsa_w_in.shape[0]
    tm = min(512, T)
    Rs = Ts * Bs

    ada = _ada_call(jnp.concatenate([c_prompt, c_sample], axis=0), ada_w, ada_b)
    xp = x_prompt
    xs = jnp.transpose(x_sample, (1, 0, 2)).reshape(1, Rs, D)
    n_pool, page_size = cache_cmp_kv.shape[:2]
    assert page_size % CMP_STRIDE == 0
    cache_cmp3 = _chunk_major_cache(cache_cmp_kv)
    cache_sel3 = cache_sel_kv.reshape(n_pool, page_size, -1)
    state_win4 = state_win_kv.reshape(state_win_kv.shape[:3] + (2 * GH,))
    past = page_table.shape[1] * page_size
    bias_tabs = _sample_bias_tables(rel_bias, Ts, past, state_win_kv.shape[2], past // CMP_STRIDE)
    pad_cols = (-nsa_w_in.shape[2] + C_HEADS * HEAD_DIM + 6 * GH + LANES)

    cmp_p, cmp_s, sel_p, sel_s, win_p, win_s, conv_p, conv_s, v_s = ([] for _ in range(9))
    for l in range(depth):
        mp = [ada[l, :B, k * D:(k + 1) * D].reshape(B, 1, D) for k in range(9)]
        ms = [jnp.tile(ada[l, B:, k * D:(k + 1) * D], (Ts, 1)).reshape(1, Rs, D) for k in range(9)]
        w1 = ffn_w1[l].astype(BF16)
        w2 = ffn_w2[l].astype(BF16)
        xp = _ffn_call(xp, mp[0], mp[1], mp[2], norm_g[l, 0], w1[0], w2[0], tm)
        xs = _ffn_call(xs, ms[0], ms[1], ms[2], norm_g[l, 0], w1[0], w2[0], Rs)
        if l % 2 == 0:
            e = l // 2
            w_in = even_w_in[e].astype(BF16)
            w_out = even_w_out[e].astype(BF16)
            prm = (gmlp_v_g[e], gmlp_ws[e], gmlp_bs[e])
            cprm = (conv_w[e], conv_b[e], conv_ln_g[e], conv_ln_b[e])
            xp, cst_p = _even_call(xp, mp[3], mp[4], mp[5], norm_g[l, 1], w_in, w_out, *prm, *cprm, tm)
            xs, cst_s, vrow = _even_s_call(xs, ms[3], ms[4], ms[5], norm_g[l, 1], w_in, w_out, *prm,
                                           state_conv[e], *cprm, Ts, Bs)
            conv_p.append(cst_p)
            conv_s.append(cst_s)
            v_s.append(vrow)
        else:
            o = l // 2
            w_in_pad = jnp.pad(nsa_w_in[o], ((0, 0), (0, pad_cols))).astype(BF16)
            w_out = nsa_w_out[o].astype(BF16)
            prm = (norm_g[l, 1], w_in_pad, w_out, q_norm_g[o], k_norm_g[o], cmp_pe[o], cmp_w1[o], cmp_w2[o])
            xp, rc_p, rs_p, w_p = _nsa_prompt(xp, mp, *prm, rel_bias, tm)
            xs, rc_s, rs_s, w_s = _nsa_sample(xs, ms, *prm, bias_tabs, cache_cmp3, cache_sel3, page_table, state_win4,
                                              o, Ts, Bs)
            kv6 = lambda a: a.reshape(B, -1, 2, KV_GROUPS, HEAD_DIM)
            cmp_p.append(kv6(rc_p))
            sel_p.append(kv6(rs_p))
            win_p.append(kv6(w_p))
            cmp_s.append(rc_s)
            sel_s.append(rs_s)
            win_s.append(w_s)
        xp = _ffn_call(xp, mp[6], mp[7], mp[8], norm_g[l, 2], w1[1], w2[1], tm)
        xs = _ffn_call(xs, ms[6], ms[7], ms[8], norm_g[l, 2], w1[1], w2[1], Rs)
    y_sample = jnp.transpose(xs.reshape(Ts, Bs, D), (1, 0, 2))
    return (xp, y_sample, jnp.stack(cmp_p, axis=2), jnp.stack(cmp_s, axis=2), jnp.stack(sel_p, axis=2),
            jnp.stack(sel_s, axis=2), jnp.stack(win_p, axis=0), jnp.stack(win_s, axis=0),
            jnp.stack(conv_p, axis=0), jnp.stack(conv_s, axis=0), jnp.stack(v_s, axis=0))
```

```python
import functools
import math

import numpy as np
import jax
import jax.numpy as jnp
from jax import lax
from jax.experimental import pallas as pl
from jax.experimental.pallas import tpu as pltpu

F32 = jnp.float32
BF16 = jnp.bfloat16

A_GROUPS = 8
CHUNK = 128
CONV_W = 31
C_HEADS = 16
KV_GROUPS = 4
HEAD_DIM = 64
HPG = C_HEADS // KV_GROUPS
GH = KV_GROUPS * HEAD_DIM
CMP_BLOCK = 32
CMP_STRIDE = 16
SEL_BLOCK = 64
N_SEL = 16
WINDOW = 512
N_BUCKETS = 32
MAX_DIST = 128
SCALE = HEAD_DIM ** -0.5
EPS = 1e-6
NEG = -1e30
FORCE = 1e6

VMEM_LIMIT_BYTES = 60 * 2 ** 20
LANES = 128
SUBLANES = 8
MXU_DIM = 256
TQ = 256
TK = 256
STACK = 4
HIST = 32


def _params(*sem):
    return pltpu.CompilerParams(dimension_semantics=sem, vmem_limit_bytes=VMEM_LIMIT_BYTES)


def _dot(a, b):
    return jnp.dot(a, b, preferred_element_type=F32)


def _dot_nt(a, b):
    return lax.dot_general(a, b, (((1,), (1,)), ((), ())), preferred_element_type=F32)


def _dot_tn(a, b):
    return lax.dot_general(a, b, (((0,), (0,)), ((), ())), preferred_element_type=F32)


def _split_bf16(x):
    hi = x.astype(BF16)
    lo = (x - hi.astype(F32)).astype(BF16)
    return hi, lo


def _sigmoid(x):
    return 1.0 / (1.0 + jnp.exp(-x))


def _silu(x):
    return x * _sigmoid(x)


def _gelu(x):
    return 0.5 * x * (1.0 + jnp.tanh(math.sqrt(2.0 / math.pi) * (x + 0.044715 * (x * x * x))))


def _modulate(x, g, shift, scale):
    y = x * lax.rsqrt(jnp.mean(x * x, axis=-1, keepdims=True) + EPS)
    return (y * g) * (1.0 + scale) + shift


def _bucket_np(dist):
    exact = N_BUCKETS // 2
    d = np.maximum(dist, 0)
    df = np.maximum(d, 1).astype(np.float32)
    large = exact + (np.log(df / np.float32(exact)) / np.float32(math.log(MAX_DIST / exact))
                     * np.float32(N_BUCKETS - exact)).astype(np.int32)
    return np.where(d < exact, d, np.minimum(large, N_BUCKETS - 1)).astype(np.int32)


def _bucket_thresholds():
    b = _bucket_np(np.arange(4 * MAX_DIST))
    assert (np.diff(b) >= 0).all() and b[MAX_DIST] == N_BUCKETS - 1
    return [int(np.argmax(b >= k)) for k in range(N_BUCKETS)]


_THR = _bucket_thresholds()


def _bias_lut(d, rb_ref, h):
    val = jnp.full(d.shape, rb_ref[0, h], F32)
    for k in range(1, N_BUCKETS):
        val = jnp.where(d >= _THR[k], rb_ref[k, h], val)
    return val


_SMEM_SPEC = pl.BlockSpec(memory_space=pltpu.SMEM)


def _full_spec(shape):
    n = len(shape)
    return pl.BlockSpec(shape, lambda *_: (0,) * n)


def _tok_spec(tm, width):
    return pl.BlockSpec((1, tm, width), lambda b, i: (b, i, 0))


def _mod_spec(mod, tm):
    if mod.shape[1] == 1:
        return pl.BlockSpec((1, 1, mod.shape[2]), lambda b, i: (b, 0, 0))
    return pl.BlockSpec((1, tm, mod.shape[2]), lambda b, i: (b, i, 0))


def _ada_kernel(c_ref, w_ref, b_ref, o_ref):
    c = c_ref[...]
    o_ref[0] = _dot(_silu(c).astype(BF16), w_ref[0].astype(BF16)) + b_ref[0]


def _ada_call(c_all, ada_w, ada_b):
    L, D, N = ada_w.shape
    M = c_all.shape[0]
    tn = 1024
    return pl.pallas_call(
        _ada_kernel,
        grid=(L, N // tn),
        in_specs=[pl.BlockSpec((M, D), lambda l, j: (0, 0)),
                  pl.BlockSpec((1, D, tn), lambda l, j: (l, 0, j)),
                  pl.BlockSpec((1, 1, tn), lambda l, j: (l, 0, j))],
        out_specs=pl.BlockSpec((1, M, tn), lambda l, j: (l, 0, j)),
        out_shape=jax.ShapeDtypeStruct((L, M, N), F32),
        compiler_params=_params("arbitrary", "arbitrary"),
        name="ada",
    )(c_all, ada_w, ada_b.reshape(L, 1, N))


def _ffn_kernel(x_ref, sh_ref, sc_ref, gt_ref, g_ref, w1_ref, w2_ref, o_ref, acc_ref, *, ff, tf):
    x = x_ref[0]
    h = _modulate(x, g_ref[...], sh_ref[0], sc_ref[0]).astype(BF16)
    for c in range(ff // tf):
        a = _dot(h, w1_ref[:, c * tf:(c + 1) * tf])
        b = _dot(h, w1_ref[:, ff + c * tf:ff + (c + 1) * tf])
        t = (_silu(a) * b).astype(BF16)
        part = _dot(t, w2_ref[c * tf:(c + 1) * tf, :])
        if c == 0:
            acc_ref[...] = part
        else:
            acc_ref[...] += part
    o_ref[0] = x + 0.5 * gt_ref[0] * acc_ref[...]


def _ffn_call(x3, sh, sc, gt, g, w1, w2, tm):
    NB, R, D = x3.shape
    ff = w2.shape[0]
    tf = MXU_DIM
    return pl.pallas_call(
        functools.partial(_ffn_kernel, ff=ff, tf=tf),
        grid=(NB, R // tm),
        in_specs=[_tok_spec(tm, D), _mod_spec(sh, tm), _mod_spec(sc, tm), _mod_spec(gt, tm),
                  _full_spec((1, D)), _full_spec(w1.shape), _full_spec(w2.shape)],
        out_specs=_tok_spec(tm, D),
        out_shape=jax.ShapeDtypeStruct(x3.shape, F32),
        scratch_shapes=[pltpu.VMEM((tm, D), F32)],
        compiler_params=_params("arbitrary", "arbitrary"),
        name="ffn",
    )(x3, sh, sc, gt, g.reshape(1, D), w1, w2)


def _even_kernel(x_ref, sh_ref, sc_ref, gt_ref, g_ref, win_ref, wout_ref, vg_ref, ws_ref, bs_ref,
                 cw_ref, cb_ref, lg_ref, lb_ref, o_ref, cs_ref, ext_ref, sa_ref, shf_ref, *, tm, aw):
    @pl.when(pl.program_id(1) == 0)
    def _():
        ext_ref[0:HIST, :] = jnp.zeros((HIST, ext_ref.shape[1]), F32)

    x = x_ref[0]
    h = _modulate(x, g_ref[...], sh_ref[0], sc_ref[0]).astype(BF16)
    z = _dot(h, win_ref[...])
    u = _gelu(z[:, 0:aw])
    gv = _gelu(z[:, aw:2 * aw])
    v = gv * lax.rsqrt(jnp.mean(gv * gv, axis=-1, keepdims=True) + EPS) * vg_ref[...]
    vb = v.astype(BF16)
    row = lax.broadcasted_iota(jnp.int32, (CHUNK, CHUNK), 0)
    col = lax.broadcasted_iota(jnp.int32, (CHUNK, CHUNK), 1)
    wm = [jnp.where(row >= col, ws_ref[gi], 0.0).astype(BF16) for gi in range(A_GROUPS)]
    gw = aw // A_GROUPS
    first_half = col < gw
    for c in range(tm // CHUNK):
        for q in range(aw // LANES):
            vq = vb[c * CHUNK:(c + 1) * CHUNK, q * LANES:(q + 1) * LANES]
            s0 = _dot(wm[2 * q], vq)
            s1 = _dot(wm[2 * q + 1], vq)
            sa_ref[c * CHUNK:(c + 1) * CHUNK, q * LANES:(q + 1) * LANES] = (
                jnp.where(first_half, s0, s1) + bs_ref[:, q * LANES:(q + 1) * LANES])
    a_out = u * sa_ref[...]
    glu = z[:, 2 * aw:3 * aw] * _sigmoid(z[:, 3 * aw:4 * aw])
    ext_ref[HIST:HIST + tm, :] = glu
    off = HIST - (CONV_W - 1)
    conv = jnp.broadcast_to(cb_ref[...], (tm, ext_ref.shape[1]))
    for r in range(SUBLANES):
        na = (CONV_W - 1 - r) // SUBLANES + 1
        nr = tm + SUBLANES * (na - 1)
        shf_ref[0:nr, :] = ext_ref[pl.ds(off + r, nr), :]
        for a in range(na):
            k = SUBLANES * a + r
            conv = conv + shf_ref[SUBLANES * a:SUBLANES * a + tm, :] * cw_ref[k:k + 1, :]
    cs_ref[0] = ext_ref[pl.ds(tm + off, CONV_W - 1), :]
    ext_ref[0:HIST, :] = ext_ref[tm:tm + HIST, :]
    mu = jnp.mean(conv, axis=-1, keepdims=True)
    cc = conv - mu
    var = jnp.mean(cc * cc, axis=-1, keepdims=True)
    b_out = _silu(cc * lax.rsqrt(var + EPS) * lg_ref[...] + lb_ref[...])
    out = _dot(a_out.astype(BF16), wout_ref[0:aw, :]) + _dot(b_out.astype(BF16), wout_ref[aw:, :])
    o_ref[0] = x + gt_ref[0] * out


def _even_call(x3, sh, sc, gt, g, w_in, w_out, v_g, ws, bs, cw, cb, ln_g, ln_b, tm):
    B, T, D = x3.shape
    aw = v_g.shape[0]
    bw = cw.shape[1]
    assert aw == bw and aw // A_GROUPS * 2 == LANES and T % tm == 0 and tm % CHUNK == 0
    bs_exp = jnp.repeat(bs.T, aw // A_GROUPS, axis=1)
    cw_pad = jnp.pad(cw, ((0, 1), (0, 0)))
    row = lambda a: a.reshape(1, -1)
    return pl.pallas_call(
        functools.partial(_even_kernel, tm=tm, aw=aw),
        grid=(B, T // tm),
        in_specs=[_tok_spec(tm, D), _mod_spec(sh, tm), _mod_spec(sc, tm), _mod_spec(gt, tm),
                  _full_spec((1, D)), _full_spec(w_in.shape), _full_spec(w_out.shape),
                  _full_spec((1, aw)), _full_spec(ws.shape), _full_spec(bs_exp.shape),
                  _full_spec(cw_pad.shape), _full_spec((1, bw)), _full_spec((1, bw)), _full_spec((1, bw))],
        out_specs=[_tok_spec(tm, D), pl.BlockSpec((1, CONV_W - 1, bw), lambda b, i: (b, 0, 0))],
        out_shape=[jax.ShapeDtypeStruct(x3.shape, F32), jax.ShapeDtypeStruct((B, CONV_W - 1, bw), F32)],
        scratch_shapes=[pltpu.VMEM((tm + HIST, bw), F32), pltpu.VMEM((tm, aw), F32),
                        pltpu.VMEM((tm + HIST, bw), F32)],
        compiler_params=_params("arbitrary", "arbitrary"),
        name="even_prompt",
    )(x3, sh, sc, gt, row(g), w_in, w_out, row(v_g), ws, bs_exp, cw_pad, row(cb), row(ln_g), row(ln_b))


def _even_s_kernel(x_ref, sh_ref, sc_ref, gt_ref, g_ref, win_ref, wout_ref, vg_ref, coef_ref, bsa_ref,
                   st_ref, cw_ref, cb_ref, lg_ref, lb_ref, o_ref, cs_ref, v_ref, *, ts, bsz, aw):
    x = x_ref[0]
    h = _modulate(x, g_ref[...], sh_ref[0], sc_ref[0]).astype(BF16)
    z = _dot(h, win_ref[...])
    u = _gelu(z[:, 0:aw])
    gv = _gelu(z[:, aw:2 * aw])
    v = gv * lax.rsqrt(jnp.mean(gv * gv, axis=-1, keepdims=True) + EPS) * vg_ref[...]
    glu = z[:, 2 * aw:3 * aw] * _sigmoid(z[:, 3 * aw:4 * aw])
    hist = CONV_W - 1
    sl = lambda a, t: a[t * bsz:(t + 1) * bsz]
    a_parts, b_parts = [], []
    for t in range(ts):
        v_ref[t] = sl(v, t)
        s = bsa_ref[t]
        for j in range(t + 1):
            s = s + coef_ref[t, j] * sl(v, j)
        a_parts.append(sl(u, t) * s)
        conv = cb_ref[...]
        for m in range(t, hist):
            conv = conv + st_ref[m] * cw_ref[m - t:m - t + 1, :]
        for j in range(t + 1):
            conv = conv + sl(glu, j) * cw_ref[hist - t + j:hist - t + j + 1, :]
        mu = jnp.mean(conv, axis=-1, keepdims=True)
        cc = conv - mu
        var = jnp.mean(cc * cc, axis=-1, keepdims=True)
        b_parts.append(_silu(cc * lax.rsqrt(var + EPS) * lg_ref[...] + lb_ref[...]))
    for i in range(hist):
        cs_ref[i] = st_ref[i + ts] if i + ts < hist else sl(glu, i + ts - hist)
    a_out = jnp.concatenate(a_parts, axis=0).astype(BF16)
    b_out = jnp.concatenate(b_parts, axis=0).astype(BF16)
    out = _dot(a_out, wout_ref[0:aw, :]) + _dot(b_out, wout_ref[aw:, :])
    o_ref[0] = x + gt_ref[0] * out


def _even_s_call(x3, sh, sc, gt, g, w_in, w_out, v_g, ws, bs, state, cw, cb, ln_g, ln_b, ts, bsz):
    _, R, D = x3.shape
    aw = v_g.shape[0]
    bw = cw.shape[1]
    gw = aw // A_GROUPS
    hist = CONV_W - 1
    assert ts <= CHUNK and ts <= hist
    coef = jnp.repeat(jnp.transpose(ws[:, :ts, :ts], (1, 2, 0)), gw, axis=2).reshape(ts, ts, 1, aw)
    bsa = jnp.repeat(bs[:, :ts].T, gw, axis=1).reshape(ts, 1, aw)
    st = jnp.transpose(state, (1, 0, 2))
    row = lambda a: a.reshape(1, -1)
    args = (x3, sh, sc, gt, row(g), w_in, w_out, row(v_g), coef, bsa, st, cw, row(cb), row(ln_g), row(ln_b))
    x_new, cs, v = pl.pallas_call(
        functools.partial(_even_s_kernel, ts=ts, bsz=bsz, aw=aw),
        grid=(1,),
        in_specs=[_full_spec(a.shape) for a in args],
        out_specs=[_full_spec(x3.shape), _full_spec((hist, bsz, bw)), _full_spec((ts, bsz, aw))],
        out_shape=[jax.ShapeDtypeStruct(x3.shape, F32), jax.ShapeDtypeStruct((hist, bsz, bw), F32),
                   jax.ShapeDtypeStruct((ts, bsz, aw), F32)],
        compiler_params=_params("arbitrary"),
        name="even_sample",
    )(*args)
    return x_new, jnp.transpose(cs, (1, 0, 2)), jnp.transpose(v, (1, 0, 2))


def _head_rmsnorm(x, g2):
    lo = lax.broadcasted_iota(jnp.int32, (1, LANES), 1) < HEAD_DIM
    sq = x * x
    s_lo = jnp.sum(jnp.where(lo, sq, 0.0), axis=-1, keepdims=True)
    s_hi = jnp.sum(jnp.where(lo, 0.0, sq), axis=-1, keepdims=True)
    return x * lax.rsqrt(jnp.where(lo, s_lo, s_hi) * (1.0 / HEAD_DIM) + EPS) * g2


def _nsa_proj_kernel(x_ref, sh_ref, sc_ref, g_ref, w_ref, qg_ref, kg_ref, qn_ref, cmp_ref, sel_ref, win_ref,
                     seln_ref, winn_ref, gate_ref, gatet_ref, *, qscale):
    x = x_ref[0]
    h = _modulate(x, g_ref[...], sh_ref[0], sc_ref[0]).astype(BF16)
    z = _dot(h, w_ref[...])
    nq = C_HEADS * HEAD_DIM
    for c in range(nq // LANES):
        qn_ref[0, :, c * LANES:(c + 1) * LANES] = (
            _head_rmsnorm(z[:, c * LANES:(c + 1) * LANES], qg_ref[...]) * qscale).astype(BF16)
    cmp_ref[0] = z[:, nq:nq + 2 * GH]
    sel_ref[0] = z[:, nq + 2 * GH:nq + 4 * GH]
    win_ref[0] = z[:, nq + 4 * GH:nq + 6 * GH]
    for c in range(GH // LANES):
        sk = nq + 2 * GH + c * LANES
        wk = nq + 4 * GH + c * LANES
        seln_ref[0, :, c * LANES:(c + 1) * LANES] = _head_rmsnorm(z[:, sk:sk + LANES], kg_ref[...]).astype(BF16)
        winn_ref[0, :, c * LANES:(c + 1) * LANES] = _head_rmsnorm(z[:, wk:wk + LANES], kg_ref[...]).astype(BF16)
    gate = _sigmoid(z[:, nq + 6 * GH:nq + 6 * GH + LANES])
    gate_ref[0] = gate
    gatet_ref[0] = gate.T


def _nsa_proj_call(x3, sh, sc, g, w_in_pad, qn_g, kn_g, qscale, tm):
    NB, R, D = x3.shape
    nq = C_HEADS * HEAD_DIM
    outs = ((nq, BF16), (2 * GH, F32), (2 * GH, F32), (2 * GH, F32), (GH, BF16), (GH, BF16), (LANES, F32))
    tile2 = lambda a: jnp.tile(a, LANES // HEAD_DIM).reshape(1, LANES)
    return pl.pallas_call(
        functools.partial(_nsa_proj_kernel, qscale=qscale),
        grid=(NB, R // tm),
        in_specs=[_tok_spec(tm, D), _mod_spec(sh, tm), _mod_spec(sc, tm), _full_spec((1, D)),
                  _full_spec(w_in_pad.shape), _full_spec((1, LANES)), _full_spec((1, LANES))],
        out_specs=[_tok_spec(tm, w) for w, _ in outs] + [pl.BlockSpec((1, LANES, tm), lambda b, i: (b, 0, i))],
        out_shape=[jax.ShapeDtypeStruct((NB, R, w), dt) for w, dt in outs]
        + [jax.ShapeDtypeStruct((NB, LANES, R), F32)],
        compiler_params=_params("arbitrary", "arbitrary"),
        name="nsa_proj",
    )(x3, sh, sc, g.reshape(1, D), w_in_pad, tile2(qn_g), tile2(kn_g))


def _compress_core(load_rows, n, pe_ref, w1_ref, w2_ref, kng_ref, o_ref, is_k):
    hid = w2_ref.shape[1]
    nstk = CMP_STRIDE // STACK
    accs = [None] * KV_GROUPS
    ctop = jnp.zeros((1, hid), F32)
    cbot = jnp.zeros((1, hid), F32)
    pe = pe_ref[0].astype(BF16)
    gpl = LANES // HEAD_DIM
    for l4 in range(nstk):
        xs = [[load_rows(l4 * STACK + i, s) for s in range(KV_GROUPS // gpl)] for i in range(STACK)]
        w = w1_ref[0, l4]
        r = _dot(pe, w)
        ctop = ctop + r[l4:l4 + 1, 0:hid]
        cbot = cbot + r[nstk + l4:nstk + l4 + 1, hid:2 * hid]
        for gi in range(KV_GROUPS):
            lo = (gi % gpl) * HEAD_DIM
            xcat = jnp.concatenate([x[gi // gpl][:, lo:lo + HEAD_DIM] for x in xs], axis=1).astype(BF16)
            part = _dot(xcat, w)
            accs[gi] = part if accs[gi] is None else accs[gi] + part
    for gi in range(KV_GROUPS):
        a = accs[gi][:, 0:hid] + ctop
        b = pltpu.roll(accs[gi][:, hid:2 * hid] + cbot, n - 1, 0)
        y = _dot(_gelu(a + b).astype(BF16), w2_ref[0])
        yn = y * lax.rsqrt(jnp.mean(y * y, axis=-1, keepdims=True) + EPS) * kng_ref[...]
        o_ref[0, 0, :, gi * HEAD_DIM:(gi + 1) * HEAD_DIM] = jnp.where(is_k, yn, y).astype(o_ref.dtype)


def _compress_p_kernel(x_ref, pe_ref, w1_ref, w2_ref, kng_ref, o_ref, *, n):
    kv = pl.program_id(1)
    slabs = 2 * GH // LANES
    load = lambda l, s: x_ref[0, pl.ds(l * slabs + kv * (slabs // 2) + s, n, stride=CMP_STRIDE * slabs), :]
    _compress_core(load, n, pe_ref, w1_ref, w2_ref, kng_ref, o_ref, kv == 0)


def _compress_weights(pe, w1, w2):
    hid = w1.shape[-1]
    w1r = w1.reshape(2, CMP_BLOCK, HEAD_DIM, hid)
    pair = jnp.concatenate([w1r[:, :CMP_STRIDE], w1r[:, CMP_STRIDE:]], axis=-1)
    w1c = pair.reshape(2, CMP_STRIDE // STACK, STACK * HEAD_DIM, 2 * hid).astype(BF16)
    return pe.reshape(2, 2 * CMP_STRIDE // STACK, STACK * HEAD_DIM), w1c, w2.astype(BF16)


def _compress_p_call(cmp_rows, pe, w1, w2, kn_g):
    B, T, _ = cmp_rows.shape
    n = T // CMP_STRIDE
    pe, w1r, w2r = _compress_weights(pe, w1, w2)
    hid = w1.shape[-1]
    slabs = 2 * GH // LANES
    cmp_rows = cmp_rows.reshape(B, T * slabs, LANES)
    return pl.pallas_call(
        functools.partial(_compress_p_kernel, n=n),
        grid=(B, 2),
        in_specs=[pl.BlockSpec((1, T * slabs, LANES), lambda b, kv: (b, 0, 0)),
                  pl.BlockSpec((1,) + pe.shape[1:], lambda b, kv: (kv, 0, 0)),
                  pl.BlockSpec((1,) + w1r.shape[1:], lambda b, kv: (kv, 0, 0, 0)),
                  pl.BlockSpec((1, hid, HEAD_DIM), lambda b, kv: (kv, 0, 0)),
                  _full_spec((1, HEAD_DIM))],
        out_specs=pl.BlockSpec((1, 1, n, GH), lambda b, kv: (b, kv, 0, 0)),
        out_shape=jax.ShapeDtypeStruct((B, 2, n, GH), BF16),
        compiler_params=_params("arbitrary", "arbitrary"),
        name="compress_prompt",
    )(cmp_rows, pe, w1r, w2r, kn_g.reshape(1, HEAD_DIM))


def _compress_s_kernel(pt_ref, cache_ref, pe_ref, w1_ref, w2_ref, kng_ref, o_ref, buf_ref, sem_ref,
                       *, n, n_pages, page_size, layer):
    b = pl.program_id(0)
    kv = pl.program_id(1)
    step = b * 2 + kv
    n_steps = pl.num_programs(0) * 2
    cpp = page_size // CMP_STRIDE

    def fetch(s, start):
        sb = s // 2
        slot = s % 2
        for p in range(n_pages):
            cp = pltpu.make_async_copy(
                cache_ref.at[pt_ref[sb, p], layer * 2 + s % 2],
                buf_ref.at[slot, :, pl.ds(p * cpp, cpp), :],
                sem_ref.at[slot])
            cp.start() if start else cp.wait()

    @pl.when(step == 0)
    def _():
        fetch(step, True)

    @pl.when(step + 1 < n_steps)
    def _():
        fetch(step + 1, True)

    fetch(step, False)
    slot = step % 2
    load = lambda l, s: buf_ref[slot, l, :, s * LANES:(s + 1) * LANES]
    _compress_core(load, n, pe_ref, w1_ref, w2_ref, kng_ref, o_ref, kv == 0)


def _chunk_major_cache(cache):
    n_pool, page_size = cache.shape[:2]
    c = cache.reshape(n_pool, page_size // CMP_STRIDE, CMP_STRIDE, -1, GH)
    return jnp.transpose(c, (0, 3, 2, 1, 4))


def _compress_s_call(cache_cm, page_table, layer, pe, w1, w2, kn_g):
    Bs, n_pages = page_table.shape
    page_size = cache_cm.shape[2] * cache_cm.shape[3]
    past = n_pages * page_size
    n = past // CMP_STRIDE
    pe, w1r, w2r = _compress_weights(pe, w1, w2)
    hid = w1.shape[-1]
    grid_spec = pltpu.PrefetchScalarGridSpec(
        num_scalar_prefetch=1,
        grid=(Bs, 2),
        in_specs=[pl.BlockSpec(memory_space=pl.ANY),
                  pl.BlockSpec((1,) + pe.shape[1:], lambda b, kv, pt: (kv, 0, 0)),
                  pl.BlockSpec((1,) + w1r.shape[1:], lambda b, kv, pt: (kv, 0, 0, 0)),
                  pl.BlockSpec((1, hid, HEAD_DIM), lambda b, kv, pt: (kv, 0, 0)),
                  pl.BlockSpec((1, HEAD_DIM), lambda b, kv, pt: (0, 0))],
        out_specs=pl.BlockSpec((1, 1, n, GH), lambda b, kv, pt: (b, kv, 0, 0)),
        scratch_shapes=[pltpu.VMEM((2, CMP_STRIDE, n, GH), F32), pltpu.SemaphoreType.DMA((2,))],
    )
    return pl.pallas_call(
        functools.partial(_compress_s_kernel, n=n, n_pages=n_pages, page_size=page_size, layer=layer),
        grid_spec=grid_spec,
        out_shape=jax.ShapeDtypeStruct((Bs, 2, n, GH), BF16),
        compiler_params=_params("arbitrary", "arbitrary"),
        name="compress_sample",
    )(page_table, cache_cm, pe, w1r, w2r, kn_g.reshape(1, HEAD_DIM))


def _select(imp, tpos, nsb):
    j = lax.broadcasted_iota(jnp.int32, imp.shape, 1)
    cur = tpos // SEL_BLOCK
    valid = (j * SEL_BLOCK <= tpos) & (j < nsb)
    forced = (j == 0) | (j == cur) | (j == cur - 1)
    impf = jnp.where(valid, jnp.where(forced, FORCE, imp), NEG)
    rank = jnp.zeros(imp.shape, jnp.int32)
    for jp in range(nsb):
        c = impf[:, jp:jp + 1]
        beats = (c > impf) | ((c == impf) & (jp < j))
        rank = rank + beats.astype(jnp.int32)
    return (rank < min(N_SEL, nsb)) & valid


def _select_t(imp, tpos, nsb):
    j = lax.broadcasted_iota(jnp.int32, imp.shape, 0)
    cur = tpos // SEL_BLOCK
    valid = (j * SEL_BLOCK <= tpos) & (j < nsb)
    forced = (j == 0) | (j == cur) | (j == cur - 1)
    impf = jnp.where(valid, jnp.where(forced, FORCE, imp), NEG)
    rank = jnp.zeros(imp.shape, jnp.int32)
    for jp in range(nsb):
        c = impf[jp:jp + 1, :]
        beats = (c > impf) | ((c == impf) & (jp < j))
        rank = rank + beats.astype(jnp.int32)
    return (rank < min(N_SEL, nsb)) & valid


def _overlap_np(n_cmp, nsb, rows, cols):
    cs = np.arange(n_cmp)[:, None] * CMP_STRIDE
    ss = np.arange(nsb)[None, :] * SEL_BLOCK
    ov = np.clip(np.minimum(cs + CMP_BLOCK, ss + SEL_BLOCK) - np.maximum(cs, ss), 0, None).astype(np.float32) / CMP_STRIDE
    out = np.zeros((rows, cols), np.float32)
    out[:n_cmp, :nsb] = ov
    return out


def _stacked_qt(qt, gi):
    return jnp.concatenate([qt[(gi * HPG + p) * HEAD_DIM:(gi * HPG + p + 1) * HEAD_DIM] for p in range(HPG)],
                           axis=1).astype(BF16)


def _cmp_p_kernel(rb_ref, q_ref, kv_ref, ovt_ref, gt_ref, o_ref, sel_ref, pat_ref, ot_ref, *, n_cmp, nsb, nq_tiles):
    qb = pl.program_id(0)
    npad = kv_ref.shape[2]
    cols = HPG * TQ
    shift = TQ // CMP_STRIDE
    back = shift * (nq_tiles - 1)

    @pl.when((qb == 0) & (pl.program_id(1) == 0))
    def _():
        rows_p = pat_ref.shape[0]
        r = lax.broadcasted_iota(jnp.int32, (rows_p, TQ), 0)
        t = lax.broadcasted_iota(jnp.int32, (rows_p, TQ), 1)
        d = t - ((r - back) * CMP_STRIDE + CMP_BLOCK - 1)
        for h in range(C_HEADS):
            pat_ref[:, h * TQ:(h + 1) * TQ] = _bias_lut(d, rb_ref, h) * LOG2E

    prow = pl.multiple_of(back - shift * qb, SUBLANES)

    qt = q_ref[0].astype(F32).T
    n4 = lax.broadcasted_iota(jnp.int32, (npad, cols), 0)
    t4 = qb * TQ + (lax.broadcasted_iota(jnp.int32, (npad, cols), 1) & (TQ - 1))
    mask = ((t4 - (n4 * CMP_STRIDE + CMP_BLOCK - 1)) >= 0) & (n4 < n_cmp)
    kc = kv_ref[0, 0]
    vct = kv_ref[0, 1].astype(F32).T
    tpos = qb * TQ + lax.broadcasted_iota(jnp.int32, (nsb, TQ), 1)
    for gi in range(KV_GROUPS):
        s = (_dot(kc[:, gi * HEAD_DIM:(gi + 1) * HEAD_DIM], _stacked_qt(qt, gi))
             + pat_ref[pl.ds(prow, npad), gi * cols:(gi + 1) * cols])
        s = jnp.where(mask, s, NEG)
        m = jnp.max(s, axis=0, keepdims=True)
        e = jnp.where(mask, jnp.exp2(s - m), 0.0)
        prob = e * (1.0 / jnp.maximum(jnp.sum(e, axis=0, keepdims=True), 1e-30))
        o = _dot(vct[gi * HEAD_DIM:(gi + 1) * HEAD_DIM].astype(BF16), prob.astype(BF16))
        psum = prob[:, 0:TQ]
        for p in range(HPG):
            h = gi * HPG + p
            ot_ref[h * HEAD_DIM:(h + 1) * HEAD_DIM, :] = o[:, p * TQ:(p + 1) * TQ] * gt_ref[0, h:h + 1, :]
            if p:
                psum = psum + prob[:, p * TQ:(p + 1) * TQ]
        hi, lo = _split_bf16(psum)
        imp = _dot(ovt_ref[...], hi) + _dot(ovt_ref[...], lo)
        sel_ref[0, gi] = _select_t(imp, tpos, nsb).astype(F32)
    o_ref[0] = ot_ref[...].T.astype(o_ref.dtype)


def _cmp_p_call(qn, kvc, rel_bias, gates_t, n_cmp):
    B, T, nq = qn.shape
    npad = kvc.shape[2]
    nsb = -(-T // SEL_BLOCK)
    assert nsb % SUBLANES == 0
    ovt = jnp.asarray(_overlap_np(n_cmp, nsb, npad, nsb).T, BF16)
    nq_tiles = T // TQ
    assert TQ % CMP_STRIDE == 0 and (TQ // CMP_STRIDE) % SUBLANES == 0
    pat_rows = npad + (TQ // CMP_STRIDE) * (nq_tiles - 1)
    return pl.pallas_call(
        functools.partial(_cmp_p_kernel, n_cmp=n_cmp, nsb=nsb, nq_tiles=nq_tiles),
        grid=(nq_tiles, B),
        in_specs=[_SMEM_SPEC,
                  pl.BlockSpec((1, TQ, nq), lambda i, b: (b, i, 0)),
                  pl.BlockSpec((1, 2, npad, GH), lambda i, b: (b, 0, 0, 0)),
                  _full_spec(ovt.shape),
                  pl.BlockSpec((1, LANES, TQ), lambda i, b: (b, 0, i))],
        out_specs=[pl.BlockSpec((1, TQ, nq), lambda i, b: (b, i, 0)),
                   pl.BlockSpec((1, KV_GROUPS, nsb, TQ), lambda i, b: (b, 0, 0, i))],
        out_shape=[jax.ShapeDtypeStruct((B, T, nq), BF16),
                   jax.ShapeDtypeStruct((B, KV_GROUPS, nsb, T), F32)],
        scratch_shapes=[pltpu.VMEM((pat_rows, C_HEADS * TQ), F32), pltpu.VMEM((nq, TQ), F32)],
        compiler_params=_params("arbitrary", "arbitrary"),
        name="cmp_attn_prompt",
    )(rel_bias, qn, kvc, ovt, gates_t)


ONES_ROWS = 16


LOG2E = math.log2(math.e)


def _pattn_kernel(qb_ref, kb_ref, fl_ref, rb_ref, q_ref, k_ref, v_ref, *rest, mode, gate_row0, n_tiles):
    if mode == "sel":
        sel_ref, gt_ref, o_ref, qst_ref, m_ref, acc_ref, bias_ref, ot_ref = rest
    else:
        gt_ref, o_ref, qst_ref, m_ref, acc_ref, bias_ref, ot_ref = rest
    i = pl.program_id(1)
    qb = qb_ref[i]
    kb = kb_ref[i]
    cols = HPG * TQ

    @pl.when((pl.program_id(0) == 0) & (i == 0))
    def _():
        row = lax.broadcasted_iota(jnp.int32, (TK, TQ), 0)
        col = lax.broadcasted_iota(jnp.int32, (TK, TQ), 1)
        for oi in range(n_tiles):
            d = oi * TQ + col - row
            vis = d >= 0
            if mode == "win":
                vis = vis & (d < WINDOW)
            for h in range(C_HEADS):
                bias_ref[oi, :, h * TQ:(h + 1) * TQ] = jnp.where(vis, _bias_lut(d, rb_ref, h) * LOG2E, NEG)

    @pl.when((fl_ref[i] & 1) == 1)
    def _():
        qt = q_ref[0].astype(F32).T
        for gi in range(KV_GROUPS):
            qst_ref[gi] = _stacked_qt(qt, gi)
        m_ref[...] = jnp.full(m_ref.shape, NEG, F32)
        acc_ref[...] = jnp.zeros(acc_ref.shape, F32)

    off = qb * TQ - kb * TK
    oi = jnp.minimum(off // TQ, n_tiles - 1)
    krow = pl.multiple_of(kb * TK, TK)
    k = k_ref[0, pl.ds(krow, TK), :]
    vt = v_ref[0, pl.ds(krow, TK), :].T
    ones = jnp.ones((ONES_ROWS, TK), BF16)
    ss, m_olds, m_news = [], [], []
    for gi in range(KV_GROUPS):
        s = _dot(k[:, gi * HEAD_DIM:(gi + 1) * HEAD_DIM], qst_ref[gi]) + bias_ref[oi, :, gi * cols:(gi + 1) * cols]
        if mode == "sel":
            sm = sel_ref[0, gi, 0]
            blk = jnp.concatenate([jnp.broadcast_to(sm[r:r + 1], (SEL_BLOCK, TQ)) for r in range(TK // SEL_BLOCK)], axis=0)
            s = jnp.where(jnp.concatenate([blk > 0.5] * HPG, axis=1), s, NEG)
        m_old = m_ref[gi]
        m_olds.append(m_old)
        m_news.append(jnp.maximum(m_old, jnp.max(s, axis=0, keepdims=True)))
        ss.append(s)
    es = [jnp.exp2(ss[gi] - m_news[gi]).astype(BF16) for gi in range(KV_GROUPS)]
    pvs = []
    for gi in range(KV_GROUPS):
        vext = jnp.concatenate([vt[gi * HEAD_DIM:(gi + 1) * HEAD_DIM].astype(BF16), ones], axis=0)
        pvs.append(_dot(vext, es[gi]))
    for gi in range(KV_GROUPS):
        alpha = jnp.exp2(m_olds[gi] - m_news[gi])
        acc_ref[gi] = alpha * acc_ref[gi] + pvs[gi]
        m_ref[gi] = m_news[gi]

    @pl.when((fl_ref[i] & 2) == 2)
    def _():
        for gi in range(KV_GROUPS):
            a = acc_ref[gi]
            o = a[0:HEAD_DIM] * (1.0 / jnp.maximum(a[HEAD_DIM:HEAD_DIM + 1], 1e-30))
            for p in range(HPG):
                h = gi * HPG + p
                ot_ref[h * HEAD_DIM:(h + 1) * HEAD_DIM, :] = (
                    o[:, p * TQ:(p + 1) * TQ] * gt_ref[0, gate_row0 + h:gate_row0 + h + 1, :])
        o_ref[0] = ot_ref[...].T.astype(o_ref.dtype)


def _pair_tables(nq, mode):
    qbs, kbs, fls = [], [], []
    for qb in range(nq):
        hi = (qb * TQ + TQ - 1) // TK
        lo = 0 if mode == "sel" else max((qb * TQ - WINDOW + 1) // TK, 0)
        for kb in range(lo, hi + 1):
            qbs.append(qb)
            kbs.append(kb)
            fls.append((1 if kb == lo else 0) | (2 if kb == hi else 0))
    return tuple(np.array(a, np.int32) for a in (qbs, kbs, fls))


def _pattn_call(qn, kn, rows, selmask, gates_t, rel_bias, mode):
    B, T, nq_w = qn.shape
    assert T % TK == 0 and TK % TQ == 0
    qbs, kbs, fls = _pair_tables(T // TQ, mode)
    n_tiles = int(np.max(qbs * TQ - kbs * TK)) // TQ + 1
    if mode == "sel":
        n_tiles = min(n_tiles, (MAX_DIST + TK - 1) // TQ + 2)
        assert (n_tiles - 1) * TQ - (TK - 1) >= MAX_DIST or n_tiles == int(np.max(qbs * TQ - kbs * TK)) // TQ + 1
    in_specs = [_SMEM_SPEC,
                pl.BlockSpec((1, TQ, nq_w), lambda b, i, qb, kb, fl: (b, qb[i], 0)),
                pl.BlockSpec((1, T, GH), lambda b, i, qb, kb, fl: (b, 0, 0)),
                pl.BlockSpec((1, T, GH), lambda b, i, qb, kb, fl: (b, 0, 1))]
    args = [rel_bias, qn, kn, rows]
    if mode == "sel":
        nblk = TK // SEL_BLOCK
        in_specs.append(pl.BlockSpec((1, KV_GROUPS, 1, nblk, TQ), lambda b, i, qb, kb, fl: (b, 0, kb[i], 0, qb[i])))
        args.append(selmask.reshape(B, KV_GROUPS, T // TK, nblk, T))
    in_specs.append(pl.BlockSpec((1, LANES, TQ), lambda b, i, qb, kb, fl: (b, 0, qb[i])))
    args.append(gates_t)
    cols = HPG * TQ
    grid_spec = pltpu.PrefetchScalarGridSpec(
        num_scalar_prefetch=3,
        grid=(B, int(qbs.shape[0])),
        in_specs=in_specs,
        out_specs=pl.BlockSpec((1, TQ, nq_w), lambda b, i, qb, kb, fl: (b, qb[i], 0)),
        scratch_shapes=[pltpu.VMEM((KV_GROUPS, HEAD_DIM, cols), BF16),
                        pltpu.VMEM((KV_GROUPS, 1, cols), F32),
                        pltpu.VMEM((KV_GROUPS, HEAD_DIM + ONES_ROWS, cols), F32),
                        pltpu.VMEM((n_tiles, TK, C_HEADS * TQ), F32),
                        pltpu.VMEM((nq_w, TQ), F32)],
    )
    return pl.pallas_call(
        functools.partial(_pattn_kernel, mode=mode, gate_row0=C_HEADS * (1 if mode == "sel" else 2), n_tiles=n_tiles),
        grid_spec=grid_spec,
        out_shape=jax.ShapeDtypeStruct((B, T, nq_w), BF16),
        compiler_params=_params("arbitrary", "arbitrary"),
        name="attn_prompt_" + mode,
    )(jnp.asarray(qbs), jnp.asarray(kbs), jnp.asarray(fls), *args)


def _combine_kernel(x_ref, gt_ref, oc_ref, os_ref, ow_ref, gate_ref, w_ref, o_ref, mix_ref):
    gate = gate_ref[0]
    for h in range(C_HEADS):
        hs = slice(h * HEAD_DIM, (h + 1) * HEAD_DIM)
        mix_ref[:, hs] = (gate[:, h:h + 1] * oc_ref[0, :, hs]
                          + gate[:, C_HEADS + h:C_HEADS + h + 1] * os_ref[0, :, hs]
                          + gate[:, 2 * C_HEADS + h:2 * C_HEADS + h + 1] * ow_ref[0, :, hs]).astype(BF16)
    o_ref[0] = x_ref[0] + gt_ref[0] * _dot(mix_ref[...], w_ref[...])


def _combine_call(x3, gt, oc, os_, ow, gates, w_out, tm):
    NB, R, D = x3.shape
    nq = C_HEADS * HEAD_DIM
    return pl.pallas_call(
        _combine_kernel,
        grid=(NB, R // tm),
        in_specs=[_tok_spec(tm, D), _mod_spec(gt, tm), _tok_spec(tm, nq), _tok_spec(tm, nq), _tok_spec(tm, nq),
                  _tok_spec(tm, LANES), _full_spec(w_out.shape)],
        out_specs=_tok_spec(tm, D),
        out_shape=jax.ShapeDtypeStruct(x3.shape, F32),
        scratch_shapes=[pltpu.VMEM((tm, nq), BF16)],
        compiler_params=_params("arbitrary", "arbitrary"),
        name="nsa_combine",
    )(x3, gt, oc, os_, ow, gates, w_out)


def _sum_proj_kernel(x_ref, gt_ref, oc_ref, os_ref, ow_ref, w_ref, o_ref):
    mix = (oc_ref[0].astype(F32) + os_ref[0].astype(F32) + ow_ref[0].astype(F32)).astype(BF16)
    o_ref[0] = x_ref[0] + gt_ref[0] * _dot(mix, w_ref[...])


def _sum_proj_call(x3, gt, oc, os_, ow, w_out, tm):
    NB, R, D = x3.shape
    nq = C_HEADS * HEAD_DIM
    return pl.pallas_call(
        _sum_proj_kernel,
        grid=(NB, R // tm),
        in_specs=[_tok_spec(tm, D), _mod_spec(gt, tm), _tok_spec(tm, nq), _tok_spec(tm, nq), _tok_spec(tm, nq),
                  _full_spec(w_out.shape)],
        out_specs=_tok_spec(tm, D),
        out_shape=jax.ShapeDtypeStruct(x3.shape, F32),
        compiler_params=_params("arbitrary", "arbitrary"),
        name="nsa_out_proj",
    )(x3, gt, oc, os_, ow, w_out)


def _lut_kernel(rb_ref, d_ref, h_ref, o_ref):
    d = d_ref[...]
    hh = h_ref[...]
    out = jnp.zeros(d.shape, F32)
    for h in range(C_HEADS):
        out = jnp.where(hh == h, _bias_lut(d, rb_ref, h), out)
    o_ref[...] = out


def _lut_call(rel_bias, dist, head):
    dist = np.ascontiguousarray(np.broadcast_to(dist, head.shape)).astype(np.int32)
    head = np.ascontiguousarray(head).astype(np.int32)
    return pl.pallas_call(
        _lut_kernel,
        grid=(1,),
        in_specs=[_SMEM_SPEC, _full_spec(dist.shape), _full_spec(head.shape)],
        out_specs=_full_spec(dist.shape),
        out_shape=jax.ShapeDtypeStruct(dist.shape, F32),
        compiler_params=_params("arbitrary"),
        name="bias_lut",
    )(rel_bias, jnp.asarray(dist), jnp.asarray(head))


def _cmp_s_kernel(q_ref, kv_ref, bias_ref, ov_ref, o_ref, sel_ref, imp_ref, *, n_cmp, nsb, ts, past):
    npad = kv_ref.shape[2]
    rows = HPG * ts
    t = past + lax.broadcasted_iota(jnp.int32, (rows, npad), 0) % ts
    nidx = lax.broadcasted_iota(jnp.int32, (rows, npad), 1)
    mask = ((t - (nidx * CMP_STRIDE + CMP_BLOCK - 1)) >= 0) & (nidx < n_cmp)
    for gi in range(KV_GROUPS):
        kc = kv_ref[0, 0, :, gi * HEAD_DIM:(gi + 1) * HEAD_DIM]
        vc = kv_ref[0, 1, :, gi * HEAD_DIM:(gi + 1) * HEAD_DIM]
        s = _dot_nt(q_ref[0, gi], kc) + bias_ref[gi]
        s = jnp.where(mask, s, NEG)
        m = jnp.max(s, axis=-1, keepdims=True)
        e = jnp.where(mask, jnp.exp(s - m), 0.0)
        prob = e / jnp.maximum(jnp.sum(e, axis=-1, keepdims=True), 1e-30)
        o_ref[0, gi] = _dot(prob.astype(BF16), vc)
        hi, lo = _split_bf16(prob)
        imp16 = _dot(hi, ov_ref[...]) + _dot(lo, ov_ref[...])
        imp = imp16[0:ts]
        for p in range(1, HPG):
            imp = imp + imp16[p * ts:(p + 1) * ts]
        imp_ref[gi * ts:(gi + 1) * ts, :] = imp
    tpos = past + lax.broadcasted_iota(jnp.int32, (KV_GROUPS * ts, 1), 0) % ts
    sel_all = _select(imp_ref[...], tpos, nsb).astype(F32)
    for gi in range(KV_GROUPS):
        sel_ref[0, gi] = sel_all[gi * ts:(gi + 1) * ts]


def _cmp_s_bias(rel_bias, npad, ts, past):
    rows = HPG * ts
    r = np.arange(KV_GROUPS * rows)
    dist = (past + r % ts)[:, None] - (np.arange(npad)[None, :] * CMP_STRIDE + CMP_BLOCK - 1)
    head = np.broadcast_to((r // ts)[:, None], dist.shape)
    return _lut_call(rel_bias, dist, head).reshape(KV_GROUPS, rows, npad)


def _cmp_s_call(q2, kvc, bias, n_cmp, nsb, ts, past):
    Bs = q2.shape[0]
    npad = kvc.shape[2]
    rows = HPG * ts
    lpad = -(-nsb // LANES) * LANES
    ov = jnp.asarray(_overlap_np(n_cmp, nsb, npad, lpad), BF16)
    return pl.pallas_call(
        functools.partial(_cmp_s_kernel, n_cmp=n_cmp, nsb=nsb, ts=ts, past=past),
        grid=(Bs,),
        in_specs=[pl.BlockSpec((1, KV_GROUPS, rows, HEAD_DIM), lambda b: (b, 0, 0, 0)),
                  pl.BlockSpec((1, 2, npad, GH), lambda b: (b, 0, 0, 0)),
                  _full_spec(bias.shape), _full_spec(ov.shape)],
        out_specs=[pl.BlockSpec((1, KV_GROUPS, rows, HEAD_DIM), lambda b: (b, 0, 0, 0)),
                   pl.BlockSpec((1, KV_GROUPS, ts, lpad), lambda b: (b, 0, 0, 0))],
        out_shape=[jax.ShapeDtypeStruct((Bs, KV_GROUPS, rows, HEAD_DIM), F32),
                   jax.ShapeDtypeStruct((Bs, KV_GROUPS, ts, lpad), F32)],
        scratch_shapes=[pltpu.VMEM((KV_GROUPS * ts, lpad), F32)],
        compiler_params=_params("arbitrary"),
        name="cmp_attn_sample",
    )(q2, kvc, bias, ov)


def _decode_core(kp, vp, kn, vn, qblk, kng_col, eexp, bias_p, bias_n, mask_p, mask_n):
    qb = (qblk * kng_col).astype(BF16)

    def logits(k, bias):
        hi, lo = _split_bf16(k * k)
        ss = _dot(hi, eexp) + _dot(lo, eexp)
        r = lax.rsqrt(ss * (1.0 / HEAD_DIM) + EPS)
        return _dot(k.astype(BF16), qb) * r + bias

    lp = jnp.where(mask_p, logits(kp, bias_p), NEG)
    ln = jnp.where(mask_n, logits(kn, bias_n), NEG)
    m = jnp.maximum(jnp.max(lp, axis=0, keepdims=True), jnp.max(ln, axis=0, keepdims=True))
    ep = jnp.where(mask_p, jnp.exp(lp - m), 0.0)
    en = jnp.where(mask_n, jnp.exp(ln - m), 0.0)
    denom = jnp.sum(ep, axis=0, keepdims=True) + jnp.sum(en, axis=0, keepdims=True)
    inv = 1.0 / jnp.maximum(denom, 1e-30)
    of = _dot_tn((ep * inv).astype(BF16), vp.astype(BF16)) + _dot_tn((en * inv).astype(BF16), vn.astype(BF16))
    ncol = of.shape[0]
    per = ncol // KV_GROUPS
    rg = lax.broadcasted_iota(jnp.int32, (ncol, HEAD_DIM), 0) // per
    o = jnp.zeros((ncol, HEAD_DIM), F32)
    for gi in range(KV_GROUPS):
        o = o + jnp.where(rg == gi, of[:, gi * HEAD_DIM:(gi + 1) * HEAD_DIM], 0.0)
    return o


def _new_key_mask(ts, ncol, rows):
    jn = lax.broadcasted_iota(jnp.int32, (rows, ncol), 0)
    tn = lax.broadcasted_iota(jnp.int32, (rows, ncol), 1) % ts
    return (jn <= tn) & (jn < ts)


def _sel_s_kernel(pt_ref, cache_ref, new_ref, q_ref, kng_ref, e64_ref, bp_ref, bn_ref, selp_ref, seln_ref, exp_ref,
                  o_ref, buf_ref, sem_ref, *, n_pages, page_size, row0, ts):
    b = pl.program_id(0)
    nb = pl.num_programs(0)
    ncol = q_ref.shape[1]

    def fetch(sb, start):
        slot = sb % 2
        for p in range(n_pages):
            cp = pltpu.make_async_copy(
                cache_ref.at[pt_ref[sb, p], pl.ds(row0, 2 * GH), :],
                buf_ref.at[slot, :, pl.ds(p * page_size, page_size)],
                sem_ref.at[slot])
            cp.start() if start else cp.wait()

    @pl.when(b == 0)
    def _():
        fetch(b, True)

    @pl.when(b + 1 < nb)
    def _():
        fetch(b + 1, True)

    fetch(b, False)
    slot = b % 2
    qb = (q_ref[0] * kng_ref[...]).astype(BF16)
    e64 = e64_ref[...]

    def logits(kt, bias):
        hi, lo = _split_bf16(kt * kt)
        r = lax.rsqrt((_dot(e64, hi) + _dot(e64, lo)) * (1.0 / HEAD_DIM) + EPS)
        return _dot(qb, kt.astype(BF16)) * r + bias

    mask_p = _dot(selp_ref[0], exp_ref[...]) > 0.5
    nn = new_ref.shape[2]
    jn = lax.broadcasted_iota(jnp.int32, (ncol, nn), 1)
    tn = lax.broadcasted_iota(jnp.int32, (ncol, nn), 0) % ts
    mask_n = (jn <= tn) & (jn < ts) & (seln_ref[0] > 0.5)
    lp = jnp.where(mask_p, logits(buf_ref[slot, 0:GH, :], bp_ref[...]), NEG)
    ln = jnp.where(mask_n, logits(new_ref[0, 0:GH, :], bn_ref[...]), NEG)
    m = jnp.maximum(jnp.max(lp, axis=1, keepdims=True), jnp.max(ln, axis=1, keepdims=True))
    ep = jnp.where(mask_p, jnp.exp(lp - m), 0.0)
    en = jnp.where(mask_n, jnp.exp(ln - m), 0.0)
    inv = 1.0 / jnp.maximum(jnp.sum(ep, axis=1, keepdims=True) + jnp.sum(en, axis=1, keepdims=True), 1e-30)
    of = (_dot_nt((ep * inv).astype(BF16), buf_ref[slot, GH:2 * GH, :].astype(BF16))
          + _dot_nt((en * inv).astype(BF16), new_ref[0, GH:2 * GH, :].astype(BF16)))
    rg = lax.broadcasted_iota(jnp.int32, (ncol, HEAD_DIM), 0) // (ncol // KV_GROUPS)
    o = jnp.zeros((ncol, HEAD_DIM), F32)
    for gi in range(KV_GROUPS):
        o = o + jnp.where(rg == gi, of[:, gi * HEAD_DIM:(gi + 1) * HEAD_DIM], 0.0)
    o_ref[0] = o


def _win_s_kernel(st_ref, new_ref, q_ref, kng_ref, eexp_ref, bp_ref, bn_ref, o_ref, nw_ref, *, ts, kpos0):
    wb = st_ref.shape[2]
    ncol = q_ref.shape[2]
    jp = lax.broadcasted_iota(jnp.int32, (wb, ncol), 0)
    tp = lax.broadcasted_iota(jnp.int32, (wb, ncol), 1) % ts
    dist = wb + tp - jp
    mask_p = (dist >= 0) & (dist < WINDOW) & (kpos0 + jp >= 0)
    mask_n = _new_key_mask(ts, ncol, new_ref.shape[1])
    st = st_ref[0, 0]
    o_ref[0] = _decode_core(st[:, 0:GH], st[:, GH:2 * GH], new_ref[0, :, 0:GH], new_ref[0, :, GH:2 * GH],
                            q_ref[0], kng_ref[...], eexp_ref[...], bp_ref[...], bn_ref[...], mask_p, mask_n)
    wout = nw_ref.shape[2]
    keep = wout - ts
    nw_ref[0, 0, 0:keep, :] = st_ref[0, 0, pl.ds(wb - keep, keep), :]
    nw_ref[0, 0, keep:wout, :] = new_ref[0, 0:ts, :]


def _decode_bias(rel_bias, ts, key_dist):
    ncol = C_HEADS * ts
    dist = key_dist[:, np.arange(ncol) % ts]
    head = np.broadcast_to((np.arange(ncol) // ts)[None, :], dist.shape)
    return _lut_call(rel_bias, dist, head)


def _sample_bias_tables(rel_bias, ts, past, wb, npad):
    tq = np.arange(ts)[None, :]
    tail = 2 * MAX_DIST
    assert past >= tail and tail - ts >= MAX_DIST
    sel_tail = _decode_bias(rel_bias, ts, tail + tq - np.arange(tail)[:, None])
    sel_past = jnp.concatenate([jnp.broadcast_to(sel_tail[0:1], (past - tail, sel_tail.shape[1])), sel_tail], axis=0)
    return dict(
        new=_decode_bias(rel_bias, ts, tq - np.arange(SUBLANES)[:, None]),
        sel=sel_past,
        win=_decode_bias(rel_bias, ts, wb + tq - np.arange(wb)[:, None]),
        cmp=_cmp_s_bias(rel_bias, npad, ts, past))


def _sel_s_call(cache_fm, page_table, layer, new_rows, qblk, kn_g, eexp, bias_p, bias_n, selp, seln, ts):
    Bs, n_pages = page_table.shape
    page_size = cache_fm.shape[2]
    past = n_pages * page_size
    ncol = qblk.shape[2]
    nblk = past // SEL_BLOCK
    tr = lambda a: jnp.swapaxes(a, -1, -2)
    new_t = jnp.pad(tr(new_rows), ((0, 0), (0, 0), (0, LANES - new_rows.shape[1])))
    bias_nt = jnp.pad(tr(bias_n), ((0, 0), (0, LANES - bias_n.shape[0])))
    selp_t = tr(selp.reshape(Bs, nblk, ncol)).astype(BF16)
    seln_t = tr(seln)
    expand = np.zeros((nblk, past), np.float32)
    expand[np.arange(past) // SEL_BLOCK, np.arange(past)] = 1.0
    args = (page_table, cache_fm, new_t, tr(qblk), jnp.tile(kn_g, KV_GROUPS).reshape(1, GH), tr(eexp),
            tr(bias_p), bias_nt, selp_t, seln_t, jnp.asarray(expand, BF16))
    m3 = lambda b, pt: (b, 0, 0)
    c2 = lambda b, pt: (0, 0)
    grid_spec = pltpu.PrefetchScalarGridSpec(
        num_scalar_prefetch=1,
        grid=(Bs,),
        in_specs=[pl.BlockSpec(memory_space=pl.ANY),
                  pl.BlockSpec((1, 2 * GH, LANES), m3),
                  pl.BlockSpec((1, ncol, GH), m3),
                  pl.BlockSpec((1, GH), c2), pl.BlockSpec((ncol, GH), c2),
                  pl.BlockSpec((ncol, past), c2), pl.BlockSpec((ncol, LANES), c2),
                  pl.BlockSpec((1, ncol, nblk), m3),
                  pl.BlockSpec((1, ncol, 1), m3),
                  pl.BlockSpec((nblk, past), c2)],
        out_specs=pl.BlockSpec((1, ncol, HEAD_DIM), m3),
        scratch_shapes=[pltpu.VMEM((2, 2 * GH, past), F32), pltpu.SemaphoreType.DMA((2,))],
    )
    return pl.pallas_call(
        functools.partial(_sel_s_kernel, n_pages=n_pages, page_size=page_size, row0=layer * 2 * GH, ts=ts),
        grid_spec=grid_spec,
        out_shape=jax.ShapeDtypeStruct((Bs, ncol, HEAD_DIM), F32),
        compiler_params=_params("arbitrary"),
        name="sel_attn_sample",
    )(*args)


def _win_s_call(state4, layer, new_rows, qblk, kng_col, eexp, bias_p, bias_n, ts, past):
    n_l, Bs, wb, _ = state4.shape
    ncol = qblk.shape[2]
    wout = min(WINDOW, wb + ts)
    m3 = lambda b: (b, 0, 0)
    c2 = lambda b: (0, 0)
    return pl.pallas_call(
        functools.partial(_win_s_kernel, ts=ts, kpos0=past - wb),
        grid=(Bs,),
        in_specs=[pl.BlockSpec((1, 1, wb, 2 * GH), lambda b: (layer, b, 0, 0)),
                  pl.BlockSpec((1,) + new_rows.shape[1:], m3),
                  pl.BlockSpec((1, GH, ncol), m3),
                  pl.BlockSpec(kng_col.shape, c2), pl.BlockSpec(eexp.shape, c2),
                  pl.BlockSpec(bias_p.shape, c2), pl.BlockSpec(bias_n.shape, c2)],
        out_specs=[pl.BlockSpec((1, ncol, HEAD_DIM), m3),
                   pl.BlockSpec((1, 1, wout, 2 * GH), lambda b: (0, b, 0, 0))],
        out_shape=[jax.ShapeDtypeStruct((Bs, ncol, HEAD_DIM), F32),
                   jax.ShapeDtypeStruct((1, Bs, wout, 2 * GH), F32)],
        compiler_params=_params("arbitrary"),
        name="win_attn_sample",
    )(state4, new_rows, qblk, kng_col, eexp, bias_p, bias_n)


def _nsa_prompt(x, mods, g, w_in_pad, w_out, qn_g, kn_g, pe, w1, w2, rel_bias, tm):
    B, T, D = x.shape
    N = B * T
    nq = C_HEADS * HEAD_DIM
    qn, cmp_rows, sel_rows, win_rows, seln, winn, _, gates_t = _nsa_proj_call(
        x, mods[3], mods[4], g, w_in_pad, qn_g, kn_g, SCALE * LOG2E, tm)
    n_cmp = (T - CMP_BLOCK) // CMP_STRIDE + 1
    kvc = _compress_p_call(cmp_rows, pe, w1, w2, kn_g)
    o_cmp, selmask = _cmp_p_call(qn, kvc, rel_bias, gates_t, n_cmp)
    o_sel = _pattn_call(qn, seln, sel_rows, selmask, gates_t, rel_bias, "sel")
    o_win = _pattn_call(qn, winn, win_rows, None, gates_t, rel_bias, "win")
    x = _sum_proj_call(x, mods[5], o_cmp, o_sel, o_win, w_out, tm)
    wk = min(WINDOW, T)
    return x, cmp_rows, sel_rows, win_rows[:, T - wk:]


def _nsa_sample(x, mods, g, w_in_pad, w_out, qn_g, kn_g, pe, w1, w2, bias_tabs, cache_cmp3, cache_sel3,
                page_table, state_win4, layer, ts, bsz):
    R = ts * bsz
    nq = C_HEADS * HEAD_DIM
    past = page_table.shape[1] * cache_sel3.shape[2]
    assert past % CMP_STRIDE == 0 and past % SEL_BLOCK == 0 and ts <= SUBLANES and ts <= CMP_STRIDE
    qn, cmp_rows, sel_rows, win_rows, _, _, gates, _ = _nsa_proj_call(
        x, mods[3], mods[4], g, w_in_pad, qn_g, kn_g, SCALE, R)
    q5 = jnp.transpose(qn.reshape(ts, bsz, KV_GROUPS, HPG, HEAD_DIM), (1, 2, 3, 0, 4))
    q2 = q5.reshape(bsz, KV_GROUPS, HPG * ts, HEAD_DIM)
    qd = jnp.transpose(q5.astype(F32).reshape(bsz, KV_GROUPS, HPG * ts, HEAD_DIM), (0, 1, 3, 2))
    qblk = (qd[:, :, :, None, :] * jnp.eye(KV_GROUPS, dtype=F32)[None, :, None, :, None]).reshape(bsz, GH, C_HEADS * ts)
    ncol = C_HEADS * ts
    tk = past + ts
    n_cmp = (tk - CMP_BLOCK) // CMP_STRIDE + 1
    assert n_cmp <= past // CMP_STRIDE - 1 + 1 and (n_cmp - 1) * CMP_STRIDE + CMP_BLOCK <= past
    nsb = -(-tk // SEL_BLOCK)
    kvc = _compress_s_call(cache_cmp3, page_table, layer, pe, w1, w2, kn_g)
    o_cmp, selw = _cmp_s_call(q2, kvc, bias_tabs["cmp"], n_cmp, nsb, ts, past)
    selt = jnp.transpose(selw, (0, 3, 1, 2))
    selt = jnp.broadcast_to(selt[:, :, :, None, :], selt.shape[:3] + (HPG, ts)).reshape(bsz, -1, 1, ncol)
    nblk = past // SEL_BLOCK
    assert nsb == nblk + 1
    selp, seln = selt[:, :nblk], selt[:, nblk]
    to_seq = lambda a: jnp.pad(jnp.transpose(a.reshape(ts, bsz, 2 * GH), (1, 0, 2)), ((0, 0), (0, SUBLANES - ts), (0, 0)))
    kng_col = jnp.tile(kn_g, KV_GROUPS).reshape(GH, 1)
    eexp = np.zeros((GH, ncol), np.float32)
    for gi in range(KV_GROUPS):
        eexp[gi * HEAD_DIM:(gi + 1) * HEAD_DIM, gi * HPG * ts:(gi + 1) * HPG * ts] = 1.0
    eexp = jnp.asarray(eexp, BF16)
    o_sel = _sel_s_call(cache_sel3, page_table, layer, to_seq(sel_rows), qblk, kn_g, eexp, bias_tabs["sel"],
                        bias_tabs["new"], selp, seln, ts)
    o_win, new_win = _win_s_call(state_win4, layer, to_seq(win_rows), qblk, kng_col, eexp, bias_tabs["win"],
                                 bias_tabs["new"], ts, past)
    back = lambda o: jnp.transpose(o.reshape(bsz, KV_GROUPS, HPG, ts, HEAD_DIM), (3, 0, 1, 2, 4)).reshape(1, R, nq)
    x = _combine_call(x, mods[5], back(o_cmp), back(o_sel), back(o_win), gates, w_out, R)
    rows_out = lambda a: jnp.transpose(a.reshape(ts, bsz, 2, KV_GROUPS, HEAD_DIM), (1, 0, 2, 3, 4))
    return x, rows_out(cmp_rows), rows_out(sel_rows), new_win[0].reshape(bsz, -1, 2, KV_GROUPS, HEAD_DIM)


def kernel(x_prompt, x_sample, c_prompt, c_sample, cache_cmp_kv, cache_sel_kv, page_table, state_win_kv, state_conv, ada_w, ada_b, norm_g, ffn_w1, ffn_w2, even_w_in, even_w_out, gmlp_v_g, gmlp_ws, gmlp_bs, conv_w, conv_b, conv_ln_g, conv_ln_b, nsa_w_in, nsa_w_out, q_norm_g, k_norm_g, cmp_pe, cmp_w1, cmp_w2, rel_bias):
    B, T, D = x_prompt.shape
    Bs, Ts, _ = x_sample.shape
    depth = ada_w.shape[0]
    n_odd = nsa_w_in.shape[0]
    tm = min(512, T)
    Rs = Ts * Bs

    ada = _ada_call(jnp.concatenate([c_prompt, c_sample], axis=0), ada_w, ada_b)
    xp = x_prompt
    xs = jnp.transpose(x_sample, (1, 0, 2)).reshape(1, Rs, D)
    n_pool, page_size = cache_cmp_kv.shape[:2]
    assert page_size % CMP_STRIDE == 0
    cache_cmp3 = _chunk_major_cache(cache_cmp_kv)
    cache_sel3 = jnp.transpose(cache_sel_kv, (0, 2, 3, 4, 5, 1)).reshape(n_pool, -1, page_size)
    state_win4 = state_win_kv.reshape(state_win_kv.shape[:3] + (2 * GH,))
    past = page_table.shape[1] * page_size
    bias_tabs = _sample_bias_tables(rel_bias, Ts, past, state_win_kv.shape[2], past // CMP_STRIDE)
    pad_cols = (-nsa_w_in.shape[2] + C_HEADS * HEAD_DIM + 6 * GH + LANES)

    cmp_p, cmp_s, sel_p, sel_s, win_p, win_s, conv_p, conv_s, v_s = ([] for _ in range(9))
    for l in range(depth):
        mp = [ada[l, :B, k * D:(k + 1) * D].reshape(B, 1, D) for k in range(9)]
        ms = [jnp.tile(ada[l, B:, k * D:(k + 1) * D], (Ts, 1)).reshape(1, Rs, D) for k in range(9)]
        w1 = ffn_w1[l].astype(BF16)
        w2 = ffn_w2[l].astype(BF16)
        xp = _ffn_call(xp, mp[0], mp[1], mp[2], norm_g[l, 0], w1[0], w2[0], tm)
        xs = _ffn_call(xs, ms[0], ms[1], ms[2], norm_g[l, 0], w1[0], w2[0], Rs)
        if l % 2 == 0:
            e = l // 2
            w_in = even_w_in[e].astype(BF16)
            w_out = even_w_out[e].astype(BF16)
            prm = (gmlp_v_g[e], gmlp_ws[e], gmlp_bs[e])
            cprm = (conv_w[e], conv_b[e], conv_ln_g[e], conv_ln_b[e])
            xp, cst_p = _even_call(xp, mp[3], mp[4], mp[5], norm_g[l, 1], w_in, w_out, *prm, *cprm, tm)
            xs, cst_s, vrow = _even_s_call(xs, ms[3], ms[4], ms[5], norm_g[l, 1], w_in, w_out, *prm,
                                           state_conv[e], *cprm, Ts, Bs)
            conv_p.append(cst_p)
            conv_s.append(cst_s)
            v_s.append(vrow)
        else:
            o = l // 2
            w_in_pad = jnp.pad(nsa_w_in[o], ((0, 0), (0, pad_cols))).astype(BF16)
            w_out = nsa_w_out[o].astype(BF16)
            prm = (norm_g[l, 1], w_in_pad, w_out, q_norm_g[o], k_norm_g[o], cmp_pe[o], cmp_w1[o], cmp_w2[o])
            xp, rc_p, rs_p, w_p = _nsa_prompt(xp, mp, *prm, rel_bias, tm)
            xs, rc_s, rs_s, w_s = _nsa_sample(xs, ms, *prm, bias_tabs, cache_cmp3, cache_sel3, page_table, state_win4,
                                              o, Ts, Bs)
            kv6 = lambda a: a.reshape(B, -1, 2, KV_GROUPS, HEAD_DIM)
            cmp_p.append(kv6(rc_p))
            sel_p.append(kv6(rs_p))
            win_p.append(kv6(w_p))
            cmp_s.append(rc_s)
            sel_s.append(rs_s)
            win_s.append(w_s)
        xp = _ffn_call(xp, mp[6], mp[7], mp[8], norm_g[l, 2], w1[1], w2[1], tm)
        xs = _ffn_call(xs, ms[6], ms[7], ms[8], norm_g[l, 2], w1[1], w2[1], Rs)
    y_sample = jnp.transpose(xs.reshape(Ts, Bs, D), (1, 0, 2))
    return (xp, y_sample, jnp.stack(cmp_p, axis=2), jnp.stack(cmp_s, axis=2), jnp.stack(sel_p, axis=2),
            jnp.stack(sel_s, axis=2), jnp.stack(win_p, axis=0), jnp.stack(win_s, axis=0),
            jnp.stack(conv_p, axis=0), jnp.stack(conv_s, axis=0), jnp.stack(v_s, axis=0))
```
